```python
import functools
import jax, jax.numpy as jnp
from jax import lax
import numpy as np

D_MODEL = 1024
BATCH = 8
SEQ = 4096
DEPTH = 2
DEC_BATCH = 32
DEC_SEQ = 4
PAST_LEN = 16384
PAGE_SIZE = 128

HEAD_DIM = 64
N_HEADS_A = 6
D_A = N_HEADS_A * HEAD_DIM
D_B = 384
N_BLOCKS_B = 6
BLOCK_B = D_B // N_BLOCKS_B
N_HEADS_C = 4
DK_C = 64
DV_C = 64
D_C = N_HEADS_C * DV_C
D_MIX = D_A + D_B + D_C
DILATED_CONFIGS = ((128, 1), (512, 4), (2048, 16))
MAX_WINDOW = 2048
LOCAL_BLK = 128
CONV_W = 4
LRU_C = 8.0
MLSTM_CHUNK = 128
D_FF = 2816
D_PLE = 256
EPS = 1e-6
NEG = -1e30
SPLIT_SIZES = (D_A, D_A, D_A, D_B, D_B, D_C, D_C, D_C, D_C, N_HEADS_C, N_HEADS_C)
SPLIT_IDX = tuple(int(s) for s in np.cumsum(SPLIT_SIZES)[:-1])
N_IN = int(sum(SPLIT_SIZES))

kernel_name = "hybrid_dilated_rglru_mlstm_step"


def _rms(x, g):
    xf = x.astype(jnp.float32)
    y = xf * lax.rsqrt(jnp.mean(jnp.square(xf), axis=-1, keepdims=True) + EPS)
    return (y * g.astype(jnp.float32)).astype(x.dtype)


def _swiglu(x, wg, wu, wd):
    return (jax.nn.silu(x @ wg) * (x @ wu)) @ wd


def _dilated_prompt(q, k, v, window, dil):
    B, S, H, Dh = q.shape
    L = S // dil
    nb = -(-L // LOCAL_BLK)
    Lp = nb * LOCAL_BLK
    n_back = window // dil

    def to_blocks(t):
        t = t.reshape(B, L, dil, H, Dh)
        t = jnp.pad(t, ((0, 0), (0, Lp - L), (0, 0), (0, 0), (0, 0)))
        return t.reshape(B, nb, LOCAL_BLK, dil, H, Dh)

    def with_prev(t):
        prev = jnp.pad(t, ((0, 0), (1, 0), (0, 0), (0, 0), (0, 0), (0, 0)))[:, :nb]
        return jnp.concatenate([prev, t], axis=2)

    qb = to_blocks(q)
    kb = with_prev(to_blocks(k))
    vb = with_prev(to_blocks(v))
    s = jnp.einsum('bnqrhd,bnkrhd->bnrhqk', qb, kb).astype(jnp.float32) * (Dh ** -0.5)
    qi = jnp.arange(LOCAL_BLK)[:, None]
    ki = jnp.arange(2 * LOCAL_BLK)[None, :]
    dist = LOCAL_BLK + qi - ki
    band = (dist >= 0) & (dist <= n_back)
    not_first = jnp.arange(nb)[:, None, None] > 0
    valid = band[None] & (not_first | (ki >= LOCAL_BLK)[None])
    valid = valid[None, :, None, None]
    s = jnp.where(valid, s, NEG)
    m = jnp.max(s, axis=-1)
    p = jnp.where(valid, jnp.exp(s - m[..., None]), 0.0)
    den = jnp.sum(p, axis=-1)
    num = jnp.einsum('bnrhqk,bnkrhd->bnqrhd', p, vb.astype(jnp.float32))
    num = num.reshape(B, Lp, dil, H, Dh)[:, :L].reshape(B, S, H, Dh)

    def stat(t):
        return jnp.transpose(t, (0, 1, 4, 2, 3)).reshape(B, Lp, dil, H)[:, :L].reshape(B, S, H)

    return num, stat(m), stat(den)


def _dilated_sample(q, kc, vc, window, dil):
    B, T, H, Dh = q.shape
    W = kc.shape[1] - T
    n = window // dil + 1
    idx = W + jnp.arange(T)[:, None] - dil * jnp.arange(n)[None, :]
    valid = (idx >= 0)[None, :, None, :]
    flat = jnp.maximum(idx, 0).reshape(-1)
    kg = jnp.take(kc, flat, axis=1).reshape(B, T, n, H, Dh)
    vg = jnp.take(vc, flat, axis=1).reshape(B, T, n, H, Dh)
    s = jnp.einsum('bthd,btnhd->bthn', q, kg).astype(jnp.float32) * (Dh ** -0.5)
    s = jnp.where(valid, s, NEG)
    m = jnp.max(s, axis=-1)
    p = jnp.where(valid, jnp.exp(s - m[..., None]), 0.0)
    den = jnp.sum(p, axis=-1)
    num = jnp.einsum('bthn,btnhd->bthd', p, vg.astype(jnp.float32))
    return num, m, den


def _combine(parts):
    M = functools.reduce(jnp.maximum, [m for _, m, _ in parts])
    num = sum(nm * jnp.exp(m - M)[..., None] for nm, m, _ in parts)
    den = sum(d * jnp.exp(m - M) for _, m, d in parts)
    return num / den[..., None]


def _lin_op(left, right):
    a1, b1 = left
    a2, b2 = right
    return a1 * a2, a2 * b1 + b2


def _rglru(xb, conv_buf, h0, w):
    B, T, _ = xb.shape
    xp = jnp.concatenate([conv_buf.astype(xb.dtype), xb], axis=1)
    xc = w['conv_b'] + sum(xp[:, j:j + T] * w['conv_w'][j] for j in range(CONV_W))
    new_buf = xp[:, T:]
    xr = xc.reshape(B, T, N_BLOCKS_B, BLOCK_B)
    r = jax.nn.sigmoid(jnp.einsum('btnc,ncd->btnd', xr, w['w_rg_a']).reshape(B, T, D_B) + w['b_rg_a'])
    i = jax.nn.sigmoid(jnp.einsum('btnc,ncd->btnd', xr, w['w_rg_x']).reshape(B, T, D_B) + w['b_rg_x'])
    log_a = -LRU_C * r.astype(jnp.float32) * jax.nn.softplus(-w['lru_lambda'].astype(jnp.float32))
    a = jnp.exp(log_a)
    b = jnp.sqrt(-jnp.expm1(2.0 * log_a)) * (i * xc).astype(jnp.float32)
    b = b.at[:, 0].add(a[:, 0] * h0.astype(jnp.float32))
    _, h = lax.associative_scan(_lin_op, (a, b), axis=1)
    return h, h[:, -1], new_buf


def _mlstm_chunk(carry, inp):
    C, n, m = carry
    q, k, v, ig, lf = inp
    L = q.shape[1]
    b = jnp.cumsum(lf, axis=1)
    Dm = b[:, :, None, :] - b[:, None, :, :] + ig[:, None, :, :]
    causal = jnp.tril(jnp.ones((L, L), dtype=bool))[None, :, :, None]
    Dm = jnp.where(causal, Dm, NEG)
    inter = b + m[:, None, :]
    mt = jnp.maximum(inter, jnp.max(Dm, axis=2))
    Wt = jnp.exp(Dm - mt[:, :, None, :])
    wi = jnp.exp(inter - mt)
    A = Wt * jnp.einsum('bthd,bshd->btsh', q, k)
    num = jnp.einsum('btsh,bshd->bthd', A, v) + wi[..., None] * jnp.einsum('bthk,bhkv->bthv', q, C)
    den = jnp.sum(A, axis=2) + wi * jnp.einsum('bthk,bhk->bth', q, n)
    h = num / jnp.maximum(jnp.abs(den), jnp.exp(-mt))[..., None]
    m_new = mt[:, -1]
    ws = jnp.exp(b[:, -1:] - b + ig - m_new[:, None])
    wc = jnp.exp(b[:, -1] + m - m_new)
    C_new = wc[..., None, None] * C + jnp.einsum('bsh,bshk,bshv->bhkv', ws, k, v)
    n_new = wc[..., None] * n + jnp.einsum('bsh,bshk->bhk', ws, k)
    return (C_new, n_new, m_new), h


def _mlstm(q, k, v, ig, lf, C0, n0, m0):
    B, T = q.shape[:2]
    chunk = MLSTM_CHUNK if T % MLSTM_CHUNK == 0 else T
    nc = T // chunk

    def split(t):
        t = t.astype(jnp.float32)
        return jnp.moveaxis(t.reshape((B, nc, chunk) + t.shape[2:]), 1, 0)

    xs = (split(q), split(k), split(v), split(ig), split(lf))
    (C, n, m), h = lax.scan(_mlstm_chunk, (C0, n0, m0), xs)
    h = jnp.moveaxis(h, 0, 1).reshape(B, T, N_HEADS_C, DV_C)
    return h, C, n, m


def _token_mixing(u, w, cache):
    B, T, _ = u.shape
    z = u @ w['w_in']
    qa, ka, va, xb, gb, qc, kc, vc, oc, ic, fc = jnp.split(z, SPLIT_IDX, axis=-1)
    qa = qa.reshape(B, T, N_HEADS_A, HEAD_DIM)
    ka = ka.reshape(B, T, N_HEADS_A, HEAD_DIM)
    va = va.reshape(B, T, N_HEADS_A, HEAD_DIM)
    if cache is None:
        parts = [_dilated_prompt(qa, ka, va, wd, dl) for wd, dl in DILATED_CONFIGS]
        keep = min(MAX_WINDOW, T)
        new_k, new_v = ka[:, T - keep:], va[:, T - keep:]
        conv_buf = jnp.zeros((B, CONV_W - 1, D_B), u.dtype)
        h0 = jnp.zeros((B, D_B), jnp.float32)
        C0 = jnp.zeros((B, N_HEADS_C, DK_C, DV_C), jnp.float32)
        n0 = jnp.zeros((B, N_HEADS_C, DK_C), jnp.float32)
        m0 = jnp.zeros((B, N_HEADS_C), jnp.float32)
    else:
        k_cache, v_cache, conv_buf, h0, C0, n0, m0 = cache
        kcat = jnp.concatenate([k_cache.astype(ka.dtype), ka], axis=1)
        vcat = jnp.concatenate([v_cache.astype(va.dtype), va], axis=1)
        parts = [_dilated_sample(qa, kcat, vcat, wd, dl) for wd, dl in DILATED_CONFIGS]
        new_k, new_v = ka, va
    ya = _rms(_combine(parts).reshape(B, T, D_A).astype(u.dtype), w['g_out_a'])
    hb, hT, new_buf = _rglru(xb, conv_buf, h0, w)
    yb = _rms(jax.nn.gelu(gb) * hb.astype(u.dtype), w['g_out_b'])
    qc = qc.reshape(B, T, N_HEADS_C, DK_C)
    kc = kc.reshape(B, T, N_HEADS_C, DK_C) * (DK_C ** -0.5)
    vc = vc.reshape(B, T, N_HEADS_C, DV_C)
    igc = ic.astype(jnp.float32) + w['b_mlstm_i'].astype(jnp.float32)
    lfc = jax.nn.log_sigmoid(fc.astype(jnp.float32) + w['b_mlstm_f'].astype(jnp.float32))
    hc, C1, n1, m1 = _mlstm(qc, kc, vc, igc, lfc, C0.astype(jnp.float32),
                            n0.astype(jnp.float32), m0.astype(jnp.float32))
    hc = jax.nn.sigmoid(oc).reshape(B, T, N_HEADS_C, DV_C) * hc.astype(u.dtype)
    yc = _rms(hc, w['g_out_c'].reshape(N_HEADS_C, DV_C)).reshape(B, T, D_C)
    y = jnp.concatenate([ya, yb, yc], axis=-1) @ w['w_out']
    return y, (new_k, new_v, new_buf, hT, C1, n1, m1)


def _layer(x, pe, w, cache):
    f1 = _swiglu(_rms(x, w['g_f1_pre']), w['w_f1_gate'], w['w_f1_up'], w['w_f1_down'])
    x = x + 0.5 * _rms(f1, w['g_f1_post'])
    ymix, st = _token_mixing(_rms(x, w['g_mix_pre']), w, cache)
    x = x + _rms(ymix, w['g_mix_post'])
    f2 = _swiglu(_rms(x, w['g_f2_pre']), w['w_f2_gate'], w['w_f2_up'], w['w_f2_down'])
    x = x + 0.5 * _rms(f2, w['g_f2_post'])
    gate = jax.nn.sigmoid(_rms(x, w['g_ple_pre']) @ w['w_ple_gate'])
    x = x + _rms(gate * (pe.astype(x.dtype) @ w['w_ple_proj']), w['g_ple_post'])
    return x, st


def setup_inputs(seed: int = 0) -> dict:
    key = jax.random.key(seed)
    keys = iter(jax.random.split(key, 64))
    f32 = jnp.float32

    def nrm(shape, scale):
        return jax.random.normal(next(keys), shape, f32) * scale

    def gain(shape):
        return 1.0 + nrm(shape, 0.02)

    W_BUF = min(MAX_WINDOW, PAST_LEN)
    u = jax.random.uniform(next(keys), (DEPTH, D_B), f32, minval=0.9, maxval=0.999)
    s = u ** (1.0 / LRU_C)
    lam = jnp.log(s) - jnp.log1p(-s)
    return {
        'x_prompt': nrm((BATCH, SEQ, D_MODEL), 1.0),
        'x_sample': nrm((DEC_BATCH, DEC_SEQ, D_MODEL), 1.0),
        'cache_k': nrm((DEPTH, DEC_BATCH, W_BUF, N_HEADS_A, HEAD_DIM), 1.0),
        'cache_v': nrm((DEPTH, DEC_BATCH, W_BUF, N_HEADS_A, HEAD_DIM), 1.0),
        'state_conv': nrm((DEPTH, DEC_BATCH, CONV_W - 1, D_B), 1.0),
        'state_h': nrm((DEPTH, DEC_BATCH, D_B), 0.5),
        'state_C': nrm((DEPTH, DEC_BATCH, N_HEADS_C, DK_C, DV_C), 0.3),
        'state_n': nrm((DEPTH, DEC_BATCH, N_HEADS_C, DK_C), 0.3),
        'state_m': nrm((DEPTH, DEC_BATCH, N_HEADS_C), 1.0),
        'p_prompt': nrm((DEPTH, BATCH, SEQ, D_PLE), 1.0),
        'p_sample': nrm((DEPTH, DEC_BATCH, DEC_SEQ, D_PLE), 1.0),
        'g_f1_pre': gain((DEPTH, D_MODEL)),
        'w_f1_gate': nrm((DEPTH, D_MODEL, D_FF), D_MODEL ** -0.5),
        'w_f1_up': nrm((DEPTH, D_MODEL, D_FF), D_MODEL ** -0.5),
        'w_f1_down': nrm((DEPTH, D_FF, D_MODEL), D_FF ** -0.5),
        'g_f1_post': gain((DEPTH, D_MODEL)),
        'g_mix_pre': gain((DEPTH, D_MODEL)),
        'w_in': nrm((DEPTH, D_MODEL, N_IN), D_MODEL ** -0.5),
        'conv_w': nrm((DEPTH, CONV_W, D_B), CONV_W ** -0.5),
        'conv_b': nrm((DEPTH, D_B), 0.01),
        'w_rg_a': nrm((DEPTH, N_BLOCKS_B, BLOCK_B, BLOCK_B), BLOCK_B ** -0.5),
        'b_rg_a': nrm((DEPTH, D_B), 0.01),
        'w_rg_x': nrm((DEPTH, N_BLOCKS_B, BLOCK_B, BLOCK_B), BLOCK_B ** -0.5),
        'b_rg_x': nrm((DEPTH, D_B), 0.01),
        'lru_lambda': lam,
        'b_mlstm_i': nrm((DEPTH, N_HEADS_C), 0.1),
        'b_mlstm_f': jnp.linspace(3.0, 6.0, N_HEADS_C, dtype=f32)[None] + nrm((DEPTH, N_HEADS_C), 0.1),
        'g_out_a': gain((DEPTH, D_A)),
        'g_out_b': gain((DEPTH, D_B)),
        'g_out_c': gain((DEPTH, D_C)),
        'w_out': nrm((DEPTH, D_MIX, D_MODEL), D_MIX ** -0.5),
        'g_mix_post': gain((DEPTH, D_MODEL)),
        'g_f2_pre': gain((DEPTH, D_MODEL)),
        'w_f2_gate': nrm((DEPTH, D_MODEL, D_FF), D_MODEL ** -0.5),
        'w_f2_up': nrm((DEPTH, D_MODEL, D_FF), D_MODEL ** -0.5),
        'w_f2_down': nrm((DEPTH, D_FF, D_MODEL), D_FF ** -0.5),
        'g_f2_post': gain((DEPTH, D_MODEL)),
        'g_ple_pre': gain((DEPTH, D_MODEL)),
        'w_ple_gate': nrm((DEPTH, D_MODEL, D_MODEL), D_MODEL ** -0.5),
        'w_ple_proj': nrm((DEPTH, D_PLE, D_MODEL), D_PLE ** -0.5),
        'g_ple_post': gain((DEPTH, D_MODEL)),
    }


def reference(x_prompt, x_sample, cache_k, cache_v, state_conv, state_h, state_C, state_n, state_m,
              p_prompt, p_sample, g_f1_pre, w_f1_gate, w_f1_up, w_f1_down, g_f1_post, g_mix_pre,
              w_in, conv_w, conv_b, w_rg_a, b_rg_a, w_rg_x, b_rg_x, lru_lambda, b_mlstm_i, b_mlstm_f,
              g_out_a, g_out_b, g_out_c, w_out, g_mix_post, g_f2_pre, w_f2_gate, w_f2_up, w_f2_down,
              g_f2_post, g_ple_pre, w_ple_gate, w_ple_proj, g_ple_post):
    xp, xs = x_prompt, x_sample
    sp = [[] for _ in range(7)]
    ss = [[] for _ in range(7)]
    for i in range(DEPTH):
        w = dict(
            g_f1_pre=g_f1_pre[i], w_f1_gate=w_f1_gate[i], w_f1_up=w_f1_up[i], w_f1_down=w_f1_down[i],
            g_f1_post=g_f1_post[i], g_mix_pre=g_mix_pre[i], w_in=w_in[i], conv_w=conv_w[i],
            conv_b=conv_b[i], w_rg_a=w_rg_a[i], b_rg_a=b_rg_a[i], w_rg_x=w_rg_x[i], b_rg_x=b_rg_x[i],
            lru_lambda=lru_lambda[i], b_mlstm_i=b_mlstm_i[i], b_mlstm_f=b_mlstm_f[i],
            g_out_a=g_out_a[i], g_out_b=g_out_b[i], g_out_c=g_out_c[i], w_out=w_out[i],
            g_mix_post=g_mix_post[i], g_f2_pre=g_f2_pre[i], w_f2_gate=w_f2_gate[i],
            w_f2_up=w_f2_up[i], w_f2_down=w_f2_down[i], g_f2_post=g_f2_post[i],
            g_ple_pre=g_ple_pre[i], w_ple_gate=w_ple_gate[i], w_ple_proj=w_ple_proj[i],
            g_ple_post=g_ple_post[i])
        xp, st_p = _layer(xp, p_prompt[i], w, None)
        cache_i = (cache_k[i], cache_v[i], state_conv[i], state_h[i], state_C[i], state_n[i], state_m[i])
        xs, st_s = _layer(xs, p_sample[i], w, cache_i)
        for j in range(7):
            sp[j].append(st_p[j])
            ss[j].append(st_s[j])
    k_prompt, v_prompt, conv_prompt, h_prompt, C_prompt, n_prompt, m_prompt = [jnp.stack(a) for a in sp]
    k_sample, v_sample, conv_sample, h_sample, C_sample, n_sample, m_sample = [jnp.stack(a) for a in ss]
    return (xp, xs, k_prompt, v_prompt, k_sample, v_sample, conv_prompt, conv_sample,
            h_prompt, h_sample, C_prompt, C_sample, n_prompt, n_sample, m_prompt, m_sample)
```

```python
import functools

import jax
import jax.numpy as jnp
from jax import lax
from jax.experimental import pallas as pl
from jax.experimental.pallas import tpu as pltpu

F32 = jnp.float32
BF16 = jnp.bfloat16

EPS = 1e-6
NEG = -1e30
HEAD_DIM = 64
N_HEADS_A = 6
D_A = N_HEADS_A * HEAD_DIM
D_B = 384
N_HEADS_C = 4
D_C = N_HEADS_C * HEAD_DIM
DILATED_CONFIGS = ((128, 1), (512, 4), (2048, 16))
LOCAL_BLK = 128
CONV_W = 4
LRU_C = 8.0
MLSTM_CHUNK = 128
LANES = 128
Z_COLS = 3072
FF_CHUNK = 256
ROW_TILE = 512
VMEM_LIMIT = 56 * 1024 * 1024

ZC_QA, ZC_KA, ZC_VA, ZC_XB, ZC_GB = 0, 3, 6, 9, 12
ZC_QC, ZC_KC, ZC_VC, ZC_OC, ZC_GATE = 15, 17, 19, 21, 23


def _rms(x, g):
    return x * lax.rsqrt(jnp.mean(x * x, axis=-1, keepdims=True) + EPS) * g


def _dot(a, b):
    return jnp.dot(a, b, preferred_element_type=F32)


def _dot_nt(a, b):
    return lax.dot_general(a, b, (((1,), (1,)), ((), ())), preferred_element_type=F32)


def _dot_tn(a, b):
    return lax.dot_general(a, b, (((0,), (0,)), ((), ())), preferred_element_type=F32)


def _dot_exact(a, b):
    return jnp.dot(a, b, preferred_element_type=F32, precision=lax.Precision.HIGHEST)


def _softplus(x):
    return jnp.maximum(x, 0.0) + jnp.log1p(jnp.exp(-jnp.abs(x)))


def _gelu_tanh(x):
    return 0.5 * x * (1.0 + jnp.tanh(0.7978845608028654 * (x + 0.044715 * (x * x * x))))


def _const_spec(shape):
    nd = len(shape)
    return pl.BlockSpec(shape, lambda *_: (0,) * nd, pipeline_mode=pl.Buffered(1))


def _params(sem):
    return pltpu.CompilerParams(dimension_semantics=sem, vmem_limit_bytes=VMEM_LIMIT)


def _ffn_kernel(x_ref, gpre_ref, wg_ref, wu_ref, wd_ref, gpost_ref, o_ref, acc_ref):
    x = x_ref[...]
    h = _rms(x, gpre_ref[...]).astype(BF16)
    n_chunks = wg_ref.shape[1] // FF_CHUNK
    for c in range(n_chunks):
        sl = slice(c * FF_CHUNK, (c + 1) * FF_CHUNK)
        g = _dot(h, wg_ref[:, sl])
        u = _dot(h, wu_ref[:, sl])
        a = (g * jax.nn.sigmoid(g) * u).astype(BF16)
        d = _dot(a, wd_ref[sl, :])
        if c == 0:
            acc_ref[...] = d
        else:
            acc_ref[...] += d
    o_ref[...] = x + 0.5 * _rms(acc_ref[...], gpost_ref[...])


def _ffn(x, gpre, wg, wu, wd, gpost):
    m, d = x.shape
    ff = wg.shape[1]
    tm = min(ROW_TILE, m)
    row = pl.BlockSpec((tm, d), lambda i: (i, 0))
    return pl.pallas_call(
        _ffn_kernel,
        grid=(m // tm,),
        in_specs=[row, _const_spec((1, d)), _const_spec((d, ff)), _const_spec((d, ff)),
                  _const_spec((ff, d)), _const_spec((1, d))],
        out_specs=row,
        out_shape=jax.ShapeDtypeStruct((m, d), F32),
        scratch_shapes=[pltpu.VMEM((tm, d), F32)],
        compiler_params=_params(("parallel",)),
        name="ffn",
    )(x, gpre, wg, wu, wd, gpost)


def _inproj_kernel(x_ref, g_ref, w_ref, z_ref):
    h = _rms(x_ref[...], g_ref[...]).astype(BF16)
    for c in range(w_ref.shape[1] // 256):
        sl = slice(c * 256, (c + 1) * 256)
        z_ref[:, sl] = _dot(h, w_ref[:, sl])


def _inproj(x, g, w):
    m, d = x.shape
    n = w.shape[1]
    tm = min(ROW_TILE, m)
    return pl.pallas_call(
        _inproj_kernel,
        grid=(m // tm,),
        in_specs=[pl.BlockSpec((tm, d), lambda i: (i, 0)), _const_spec((1, d)), _const_spec((d, n))],
        out_specs=pl.BlockSpec((tm, n), lambda i: (i, 0)),
        out_shape=jax.ShapeDtypeStruct((m, n), F32),
        compiler_params=_params(("parallel",)),
        name="inproj",
    )(x, g, w)


def _outproj_kernel(x_ref, ya_ref, yb_ref, yc_ref, ga_ref, wa_ref, wb_ref, wc_ref, gpost_ref, o_ref):
    ya = _rms(ya_ref[...], ga_ref[...]).astype(BF16)
    y = _dot(ya, wa_ref[...])
    y = y + _dot(yb_ref[...].astype(BF16), wb_ref[...])
    y = y + _dot(yc_ref[...].astype(BF16), wc_ref[...])
    o_ref[...] = x_ref[...] + _rms(y, gpost_ref[...])


def _outproj(x, ya, yb, yc, ga, wa, wb, wc, gpost):
    m, d = x.shape
    tm = min(ROW_TILE, m)

    def row(width):
        return pl.BlockSpec((tm, width), lambda i: (i, 0))

    return pl.pallas_call(
        _outproj_kernel,
        grid=(m // tm,),
        in_specs=[row(d), row(D_A), row(D_B), row(D_C), _const_spec((1, D_A)),
                  _const_spec((D_A, d)), _const_spec((D_B, d)), _const_spec((D_C, d)),
                  _const_spec((1, d))],
        out_specs=row(d),
        out_shape=jax.ShapeDtypeStruct((m, d), F32),
        compiler_params=_params(("parallel",)),
        name="outproj",
    )(x, ya, yb, yc, ga, wa, wb, wc, gpost)


def _ple_kernel(x_ref, pe_ref, gpre_ref, wg_ref, wp_ref, gpost_ref, o_ref):
    x = x_ref[...]
    h = _rms(x, gpre_ref[...]).astype(BF16)
    gate = jax.nn.sigmoid(_dot(h, wg_ref[...]))
    proj = _dot(pe_ref[...].astype(BF16), wp_ref[...])
    o_ref[...] = x + _rms(gate * proj, gpost_ref[...])


def _ple(x, pe, gpre, wg, wp, gpost):
    m, d = x.shape
    dp = pe.shape[1]
    tm = min(ROW_TILE, m)
    return pl.pallas_call(
        _ple_kernel,
        grid=(m // tm,),
        in_specs=[pl.BlockSpec((tm, d), lambda i: (i, 0)), pl.BlockSpec((tm, dp), lambda i: (i, 0)),
                  _const_spec((1, d)), _const_spec((d, d)), _const_spec((dp, d)), _const_spec((1, d))],
        out_specs=pl.BlockSpec((tm, d), lambda i: (i, 0)),
        out_shape=jax.ShapeDtypeStruct((m, d), F32),
        compiler_params=_params(("parallel",)),
        name="ple",
    )(x, pe, gpre, wg, wp, gpost)


ATT_QT = 2048


def _attn_prompt_kernel(q_ref, k_ref, v_ref, o_ref, num_scr, m_scr, den_scr):
    t0 = pl.program_id(2) * ATT_QT
    blk = LOCAL_BLK
    lane = lax.broadcasted_iota(jnp.int32, (blk, LANES), 1)
    row = lax.broadcasted_iota(jnp.int32, (blk, LANES), 0)
    lo_head = lane < HEAD_DIM
    key_minus_query = lane - row
    mask_cur = key_minus_query <= 0
    scale = HEAD_DIM ** -0.5

    for ci, (_, dil) in enumerate(DILATED_CONFIGS):
        span = blk * dil

        def body(j, carry, ci=ci, dil=dil, span=span):
            r = j % dil
            n = j // dil
            qs = r + span * n
            ks = t0 + qs
            first = ks < span
            ps = jnp.where(first, ks, ks - span)
            prev_floor = jnp.where(first, 2 * blk, 0)
            mask_prev = key_minus_query >= prev_floor
            q2 = q_ref[pl.ds(qs, blk, stride=dil), :]
            kc = k_ref[pl.ds(ks, blk, stride=dil), :].astype(BF16)
            kp = k_ref[pl.ds(ps, blk, stride=dil), :].astype(BF16)
            vc = v_ref[pl.ds(ks, blk, stride=dil), :].astype(BF16)
            vp = v_ref[pl.ds(ps, blk, stride=dil), :].astype(BF16)
            outs = []
            for h in range(2):
                hm = lo_head if h == 0 else jnp.logical_not(lo_head)
                qh = jnp.where(hm, q2, 0.0).astype(BF16)
                sc = jnp.where(mask_cur, _dot_nt(qh, kc) * scale, NEG)
                sp = jnp.where(mask_prev, _dot_nt(qh, kp) * scale, NEG)
                m = jnp.maximum(jnp.max(sc, axis=-1, keepdims=True),
                                jnp.max(sp, axis=-1, keepdims=True))
                pc = jnp.exp(sc - m)
                pp = jnp.exp(sp - m)
                den = jnp.sum(pc, axis=-1, keepdims=True) + jnp.sum(pp, axis=-1, keepdims=True)
                num = _dot(pc.astype(BF16), vc) + _dot(pp.astype(BF16), vp)
                outs.append((num, m, den))
            (n0, m0, d0), (n1, m1, d1) = outs
            dst = pl.ds(qs, blk, stride=dil)
            num_scr[ci, dst, :] = jnp.where(lo_head, n0, n1)
            m_scr[ci, dst, :] = jnp.where(lo_head, m0, m1)
            den_scr[ci, dst, :] = jnp.where(lo_head, d0, d1)
            return carry

        lax.fori_loop(0, ATT_QT // blk, body, 0)

    rows = 256

    def combine(i, carry):
        sl = pl.ds(pl.multiple_of(i * rows, rows), rows)
        ms = [m_scr[c, sl, :] for c in range(3)]
        mx = jnp.maximum(jnp.maximum(ms[0], ms[1]), ms[2])
        ws = [jnp.exp(mc - mx) for mc in ms]
        num = num_scr[0, sl, :] * ws[0] + num_scr[1, sl, :] * ws[1] + num_scr[2, sl, :] * ws[2]
        den = den_scr[0, sl, :] * ws[0] + den_scr[1, sl, :] * ws[1] + den_scr[2, sl, :] * ws[2]
        o_ref[sl, :] = num / den
        return carry

    lax.fori_loop(0, ATT_QT // rows, combine, 0)


def _attn_prompt(z3):
    b, s, _ = z3.shape
    n_pairs = D_A // LANES
    q_spec = pl.BlockSpec((None, ATT_QT, LANES), lambda i, p, t: (i, t, ZC_QA + p))
    k_spec = pl.BlockSpec((None, s, LANES), lambda i, p, t: (i, 0, ZC_KA + p))
    v_spec = pl.BlockSpec((None, s, LANES), lambda i, p, t: (i, 0, ZC_VA + p))
    return pl.pallas_call(
        _attn_prompt_kernel,
        grid=(b, n_pairs, s // ATT_QT),
        in_specs=[q_spec, k_spec, v_spec],
        out_specs=pl.BlockSpec((None, ATT_QT, LANES), lambda i, p, t: (i, t, p)),
        out_shape=jax.ShapeDtypeStruct((b, s, D_A), F32),
        scratch_shapes=[pltpu.VMEM((3, ATT_QT, LANES), F32)] * 3,
        compiler_params=_params(("parallel", "parallel", "arbitrary")),
        name="attn_prompt",
    )(z3, z3, z3)


NEW_ROWS = 16


def _attn_sample_kernel(q_ref, kn_ref, vn_ref, k1_ref, k4_ref, k16_ref, v1_ref, v4_ref, v16_ref,
                        o_ref, kn_scr, vn_scr):
    t_new = q_ref.shape[0]
    nk = k1_ref.shape[0]
    head_of_lane = lax.broadcasted_iota(jnp.int32, (LANES, D_A), 1) // HEAD_DIM
    row_id = lax.broadcasted_iota(jnp.int32, (LANES, D_A), 0)
    head_sel = head_of_lane == row_id
    expand = jnp.where(head_sel, 1.0, 0.0).astype(F32)
    key_row = lax.broadcasted_iota(jnp.int32, (nk, LANES), 0)
    new_row = lax.broadcasted_iota(jnp.int32, (NEW_ROWS, LANES), 0)
    scale = HEAD_DIM ** -0.5
    kn_scr[...] = jnp.zeros((NEW_ROWS, D_A), F32)
    vn_scr[...] = jnp.zeros((NEW_ROWS, D_A), F32)
    kn_scr[0:t_new, :] = kn_ref[...]
    vn_scr[0:t_new, :] = vn_ref[...]
    knb = kn_scr[...].astype(BF16)
    vn = vn_scr[...]

    for t in range(t_new):
        qt = q_ref[pl.ds(t, 1), :]
        qm = jnp.where(head_sel, jnp.broadcast_to(qt, (LANES, D_A)), 0.0).astype(BF16)
        s_new = _dot_nt(knb, qm) * scale
        parts = []
        for ci, (_, dil) in enumerate(DILATED_CONFIGS):
            if dil == 1:
                kc = k1_ref[...]
                vc = v1_ref[...]
                cache_ok = key_row >= t
                new_ok = new_row <= t
            else:
                kref, vref = (k4_ref, v4_ref) if dil == 4 else (k16_ref, v16_ref)
                kc = kref[:, t * D_A:(t + 1) * D_A]
                vc = vref[:, t * D_A:(t + 1) * D_A]
                cache_ok = key_row >= 0
                new_ok = new_row == t
            s_c = jnp.where(cache_ok, _dot_nt(kc.astype(BF16), qm) * scale, NEG)
            s_n = jnp.where(new_ok, s_new, NEG)
            m = jnp.maximum(jnp.max(s_c, axis=0, keepdims=True), jnp.max(s_n, axis=0, keepdims=True))
            p_c = jnp.exp(s_c - m)
            p_n = jnp.exp(s_n - m)
            den = jnp.sum(p_c, axis=0, keepdims=True) + jnp.sum(p_n, axis=0, keepdims=True)
            parts.append((p_c, p_n, m, den, vc))
        mx = jnp.maximum(jnp.maximum(parts[0][2], parts[1][2]), parts[2][2])
        ws = [jnp.exp(p[2] - mx) for p in parts]
        den_all = parts[0][3] * ws[0] + parts[1][3] * ws[1] + parts[2][3] * ws[2]
        out = jnp.zeros((1, D_A), F32)
        for (p_c, p_n, _, _, vc), w in zip(parts, ws):
            coef = w / den_all
            pe_c = _dot_exact(p_c * coef, expand)
            pe_n = _dot_exact(p_n * coef, expand)
            out = out + jnp.sum(pe_c * vc, axis=0, keepdims=True) \
                + jnp.sum(pe_n * vn, axis=0, keepdims=True)
        o_ref[pl.ds(t, 1), :] = out


def _attn_sample(zs3, cache_k, cache_v):
    b, t_new, _ = zs3.shape
    w_buf = cache_k.shape[1]
    nk = LOCAL_BLK
    views = []
    specs = []
    for cache in (cache_k, cache_v):
        flat = cache.reshape(b, w_buf, D_A)
        views.append(flat)
        specs.append(pl.BlockSpec((None, nk, D_A), lambda i: (i, w_buf // nk - 1, 0)))
        for dil in (4, 16):
            views.append(flat.reshape(b, w_buf // dil, dil * D_A))
            specs.append(pl.BlockSpec((None, nk, t_new * D_A),
                                      lambda i, dil=dil: (i, w_buf // dil // nk - 1, 0)))

    def new_spec(col):
        return pl.BlockSpec((None, t_new, D_A), lambda i: (i, 0, col))

    return pl.pallas_call(
        _attn_sample_kernel,
        grid=(b,),
        in_specs=[new_spec(0), new_spec(1), new_spec(2)] + specs,
        out_specs=pl.BlockSpec((None, t_new, D_A), lambda i: (i, 0, 0)),
        out_shape=jax.ShapeDtypeStruct((b, t_new, D_A), F32),
        scratch_shapes=[pltpu.VMEM((NEW_ROWS, D_A), F32)] * 2,
        compiler_params=_params(("parallel",)),
        name="attn_sample",
    )(zs3, zs3, zs3, *views)


RG_CHUNK = 512
RG_PAD = 8


def _rglru_kernel(xb_ref, gb_ref, conv_ref, h0_ref, cw_ref, cb_ref, wa_ref, ba_ref, wx_ref, bx_ref,
                  lam_ref, g_ref, y_ref, convnew_ref, ht_ref, xp_scr, a_scr, b_scr):
    t_len = xb_ref.shape[0]
    tc = a_scr.shape[0]
    n_hist = CONV_W - 1
    xp_scr[0:RG_PAD, :] = jnp.zeros((RG_PAD, D_B), F32)
    xp_scr[RG_PAD - n_hist:RG_PAD, :] = conv_ref[...]
    if t_len % 8:
        xp_scr[RG_PAD:, :] = jnp.zeros((xp_scr.shape[0] - RG_PAD, D_B), F32)
    xp_scr[RG_PAD:RG_PAD + t_len, :] = xb_ref[...]
    convnew_ref[...] = xp_scr[RG_PAD + t_len - n_hist:RG_PAD + t_len, :]
    decay = _softplus(-lam_ref[...])
    sub8 = lax.broadcasted_iota(jnp.int32, (8, D_B), 0)
    h = h0_ref[...]
    n_scan = min(tc, t_len)
    for c in range(max(t_len // tc, 1)):
        base = c * tc
        xc = cb_ref[...]
        for j in range(CONV_W):
            xc = xc + xp_scr[base + RG_PAD - n_hist + j:base + RG_PAD - n_hist + j + tc, :] \
                * cw_ref[pl.ds(j, 1), :]
        xcb = xc.astype(BF16)
        r = jax.nn.sigmoid(_dot(xcb, wa_ref[...]) + ba_ref[...])
        gi = jax.nn.sigmoid(_dot(xcb, wx_ref[...]) + bx_ref[...])
        log_a = -LRU_C * r * decay
        a_scr[...] = jnp.exp(log_a)
        b_scr[...] = jnp.sqrt(1.0 - jnp.exp(2.0 * log_a)) * (gi * xc)

        def scan_tile(k, h, n_rows=8):
            rows8 = pl.ds(pl.multiple_of(k * 8, 8), 8)
            at = a_scr[rows8, :]
            bt = b_scr[rows8, :]
            ht = bt
            for u in range(n_rows):
                h = at[u:u + 1, :] * h + bt[u:u + 1, :]
                ht = jnp.where(sub8 == u, h, ht)
            b_scr[rows8, :] = ht
            return h

        h = lax.fori_loop(0, n_scan // 8, scan_tile, h)
        if n_scan % 8:
            h = scan_tile(n_scan // 8, h, n_scan % 8)
        gate = _gelu_tanh(gb_ref[base:base + n_scan, :])
        y_ref[base:base + n_scan, :] = _rms(gate * b_scr[0:n_scan, :], g_ref[...])
    ht_ref[...] = h


def _rglru(z3, conv_buf, h0, cw, cb, wa, ba, wx, bx, lam, g):
    b, t_len, _ = z3.shape
    tc = min(RG_CHUNK, -(-t_len // 8) * 8)
    n_hist = CONV_W - 1
    vec = _const_spec((1, D_B))
    return pl.pallas_call(
        _rglru_kernel,
        grid=(b,),
        in_specs=[pl.BlockSpec((None, t_len, D_B), lambda i: (i, 0, ZC_XB // 3)),
                  pl.BlockSpec((None, t_len, D_B), lambda i: (i, 0, ZC_GB // 3)),
                  pl.BlockSpec((None, n_hist, D_B), lambda i: (i, 0, 0)),
                  pl.BlockSpec((None, 1, D_B), lambda i: (i, 0, 0)),
                  _const_spec((CONV_W, D_B)), vec, _const_spec((D_B, D_B)), vec,
                  _const_spec((D_B, D_B)), vec, vec, vec],
        out_specs=[pl.BlockSpec((None, t_len, D_B), lambda i: (i, 0, 0)),
                   pl.BlockSpec((None, n_hist, D_B), lambda i: (i, 0, 0)),
                   pl.BlockSpec((None, 1, D_B), lambda i: (i, 0, 0))],
        out_shape=[jax.ShapeDtypeStruct((b, t_len, D_B), F32),
                   jax.ShapeDtypeStruct((b, n_hist, D_B), F32),
                   jax.ShapeDtypeStruct((b, 1, D_B), F32)],
        scratch_shapes=[pltpu.VMEM((RG_PAD + max(t_len, tc), D_B), F32),
                        pltpu.VMEM((tc, D_B), F32), pltpu.VMEM((tc, D_B), F32)],
        compiler_params=_params(("parallel",)),
        name="rglru",
    )(z3, z3, conv_buf, h0.reshape(b, 1, D_B), cw, cb, wa, ba, wx, bx, lam, g)


def _log_sigmoid(x):
    return -_softplus(-x)


def _mlstm_kernel(n_valid, q0_ref, q1_ref, k0_ref, k1_ref, v0_ref, v1_ref, o0_ref, o1_ref,
                  gc_ref, gr_ref, bias_r_ref, bias_c_ref, gain_ref, c0_ref, s0_ref,
                  y_ref, c_out_ref, s_out_ref, c_scr, s_scr):
    ck = pl.program_id(1)
    L = gc_ref.shape[0]

    @pl.when(ck == 0)
    def _():
        c_scr[...] = c0_ref[...]
        s_scr[...] = s0_ref[...]

    lane = lax.broadcasted_iota(jnp.int32, (L, LANES), 1)
    row = lax.broadcasted_iota(jnp.int32, (L, LANES), 0)
    lo_head = lane < HEAD_DIM
    lo_row = lax.broadcasted_iota(jnp.int32, (1, LANES), 1) < HEAD_DIM
    causal = lax.broadcasted_iota(jnp.int32, (L, L), 1) <= lax.broadcasted_iota(jnp.int32, (L, L), 0)
    tri = jnp.where(causal, 1.0, 0.0).astype(F32)
    tri_t = jnp.where(lax.broadcasted_iota(jnp.int32, (L, L), 0)
                      <= lax.broadcasted_iota(jnp.int32, (L, L), 1), 1.0, 0.0).astype(F32)
    blockdiag = (lax.broadcasted_iota(jnp.int32, (LANES, LANES), 0) < HEAD_DIM) == \
                (lax.broadcasted_iota(jnp.int32, (LANES, LANES), 1) < HEAD_DIM)
    row_lo = lax.broadcasted_iota(jnp.int32, (LANES, 1), 0) < HEAD_DIM

    xc = gc_ref[...] + bias_r_ref[...]
    gt_c = jnp.where(lane < N_HEADS_C, xc, _log_sigmoid(xc))
    sub = lax.broadcasted_iota(jnp.int32, (8, L), 0)
    tl = lax.broadcasted_iota(jnp.int32, (8, L), 1)
    xr = gr_ref[...] + bias_c_ref[...]
    gt_r = jnp.where(sub < N_HEADS_C, xr, _log_sigmoid(xr))
    if n_valid < L:
        gt_c = jnp.where(row < n_valid, gt_c, jnp.where(lane < N_HEADS_C, NEG, 0.0))
        gt_r = jnp.where(tl < n_valid, gt_r, jnp.where(sub < N_HEADS_C, NEG, 0.0))
    b_col = _dot_exact(tri, gt_c)
    b_row = _dot_exact(gt_r, tri_t)

    m_row = s_scr[pl.ds(2, 1), :]
    m_row_new = m_row
    lane_row = lax.broadcasted_iota(jnp.int32, (1, LANES), 1)
    refs = ((q0_ref, k0_ref, v0_ref, o0_ref), (q1_ref, k1_ref, v1_ref, o1_ref))
    for p, (q_ref, k_ref, v_ref, o_ref) in enumerate(refs):
        q2 = q_ref[...]
        k2 = k_ref[...] * (HEAD_DIM ** -0.5)
        v2b = v_ref[...].astype(BF16)
        q2b = q2.astype(BF16)
        k2b = k2.astype(BF16)
        c_pair = c_scr[p]
        n_pair = s_scr[pl.ds(p, 1), :]
        qc = _dot(q2b, c_pair.astype(BF16))
        qn = q2 * n_pair
        heads = []
        for h in range(2):
            gh = 2 * p + h
            hm = lo_head if h == 0 else jnp.logical_not(lo_head)
            b_c = b_col[:, N_HEADS_C + gh:N_HEADS_C + gh + 1]
            b_r = b_row[N_HEADS_C + gh:N_HEADS_C + gh + 1, :]
            ig_r = gt_r[gh:gh + 1, :]
            ig_c = gt_c[:, gh:gh + 1]
            m_prev = m_row[:, gh:gh + 1]
            dm = jnp.where(causal, b_c - b_r + ig_r, NEG)
            inter = b_c + m_prev
            mt = jnp.maximum(inter, jnp.max(dm, axis=1, keepdims=True))
            wt = jnp.exp(dm - mt)
            wi = jnp.exp(inter - mt)
            qh = jnp.where(hm, q2, 0.0).astype(BF16)
            a = wt * _dot_nt(qh, k2b)
            num = _dot(a.astype(BF16), v2b) + wi * qc
            den = jnp.sum(a, axis=1, keepdims=True) \
                + wi * jnp.sum(jnp.where(hm, qn, 0.0), axis=1, keepdims=True)
            hout = num / jnp.maximum(jnp.abs(den), jnp.exp(-mt))
            m_new = mt[L - 1:L, :]
            b_last = b_c[L - 1:L, :]
            ws = jnp.exp(b_last - b_c + ig_c - m_new)
            wc = jnp.exp(b_last + m_prev - m_new)
            m_row_new = jnp.where(lane_row == gh, m_new, m_row_new)
            heads.append((hout, ws, wc))
        (h0, ws0, wc0), (h1, ws1, wc1) = heads
        hout2 = jnp.where(lo_head, h0, h1)
        kw = k2 * jnp.where(lo_head, ws0, ws1)
        upd = _dot_tn(kw.astype(BF16), v2b)
        c_scr[p] = jnp.where(blockdiag, jnp.where(row_lo, wc0, wc1) * c_pair + upd, 0.0)
        s_scr[pl.ds(p, 1), :] = jnp.where(lo_row, wc0, wc1) * n_pair \
            + jnp.sum(kw, axis=0, keepdims=True)
        hc = jax.nn.sigmoid(o_ref[...]) * hout2
        sq = hc * hc
        ms0 = jnp.sum(jnp.where(lo_head, sq, 0.0), axis=1, keepdims=True)
        ms1 = jnp.sum(jnp.where(lo_head, 0.0, sq), axis=1, keepdims=True)
        ms = jnp.where(lo_head, ms0, ms1) * (1.0 / HEAD_DIM)
        y_ref[:, p * LANES:(p + 1) * LANES] = \
            hc * lax.rsqrt(ms + EPS) * gain_ref[:, p * LANES:(p + 1) * LANES]
    s_scr[pl.ds(2, 1), :] = m_row_new

    @pl.when(ck == pl.num_programs(1) - 1)
    def _():
        c_out_ref[...] = c_scr[...]
        s_out_ref[...] = s_scr[...]


def _mlstm(z3, col0, g_rows, n_valid, bias_r, bias_c, gain, c0, s0):
    b, t_len, _ = z3.shape
    L = MLSTM_CHUNK
    nc = t_len // L

    def col(cb):
        return pl.BlockSpec((None, L, LANES), lambda i, c: (i, c, cb - col0))

    state_c = pl.BlockSpec((None, 2, LANES, LANES), lambda i, c: (i, 0, 0, 0))
    state_s = pl.BlockSpec((None, 8, LANES), lambda i, c: (i, 0, 0))
    return pl.pallas_call(
        functools.partial(_mlstm_kernel, n_valid),
        grid=(b, nc),
        in_specs=[col(ZC_QC), col(ZC_QC + 1), col(ZC_KC), col(ZC_KC + 1), col(ZC_VC), col(ZC_VC + 1),
                  col(ZC_OC), col(ZC_OC + 1), col(ZC_GATE),
                  pl.BlockSpec((None, 8, L), lambda i, c: (i, 0, c)),
                  _const_spec((1, LANES)), _const_spec((8, 1)), _const_spec((1, D_C)),
                  state_c, state_s],
        out_specs=[pl.BlockSpec((None, L, D_C), lambda i, c: (i, c, 0)), state_c, state_s],
        out_shape=[jax.ShapeDtypeStruct((b, t_len, D_C), F32),
                   jax.ShapeDtypeStruct((b, 2, LANES, LANES), F32),
                   jax.ShapeDtypeStruct((b, 8, LANES), F32)],
        scratch_shapes=[pltpu.VMEM((2, LANES, LANES), F32), pltpu.VMEM((8, LANES), F32)],
        compiler_params=_params(("parallel", "arbitrary")),
        name="mlstm",
    )(*([z3] * 9), g_rows, bias_r, bias_c, gain, c0, s0)


def _pack_mlstm_state(C, n, m):
    b = C.shape[0]
    Cp = C.reshape(b, 2, 2, HEAD_DIM, HEAD_DIM)
    eye = jnp.eye(2, dtype=C.dtype)
    c_pairs = jnp.einsum('bphkv,hg->bphkgv', Cp, eye).reshape(b, 2, LANES, LANES)
    slab = jnp.zeros((b, 8, LANES), C.dtype)
    slab = slab.at[:, 0:2, :].set(n.reshape(b, 2, LANES))
    slab = slab.at[:, 2, 0:N_HEADS_C].set(m)
    return c_pairs, slab


def _unpack_mlstm_state(c_pairs, slab):
    b = c_pairs.shape[0]
    Cp = c_pairs.reshape(b, 2, 2, HEAD_DIM, 2, HEAD_DIM)
    C = jnp.stack([Cp[:, :, 0, :, 0, :], Cp[:, :, 1, :, 1, :]], axis=2)
    C = C.reshape(b, N_HEADS_C, HEAD_DIM, HEAD_DIM)
    n = slab[:, 0:2, :].reshape(b, N_HEADS_C, HEAD_DIM)
    m = slab[:, 2, 0:N_HEADS_C]
    return C, n, m


def _block_diag(w):
    nb, bs, _ = w.shape
    eye = jnp.eye(nb, dtype=w.dtype)
    return jnp.einsum('ncd,nm->ncmd', w, eye).reshape(nb * bs, nb * bs)


def _prep_layer(w):
    d = w['w_in'].shape[0]
    row = lambda v: v.reshape(1, -1)
    w_in = jnp.pad(w['w_in'], ((0, 0), (0, Z_COLS - w['w_in'].shape[1]))).astype(BF16)
    bias_gate = jnp.concatenate([w['b_mlstm_i'], w['b_mlstm_f']])
    return dict(
        f1=(row(w['g_f1_pre']), w['w_f1_gate'].astype(BF16), w['w_f1_up'].astype(BF16),
            w['w_f1_down'].astype(BF16), row(w['g_f1_post'])),
        f2=(row(w['g_f2_pre']), w['w_f2_gate'].astype(BF16), w['w_f2_up'].astype(BF16),
            w['w_f2_down'].astype(BF16), row(w['g_f2_post'])),
        inproj=(row(w['g_mix_pre']), w_in),
        rglru=(w['conv_w'], row(w['conv_b']), _block_diag(w['w_rg_a']).astype(BF16), row(w['b_rg_a']),
               _block_diag(w['w_rg_x']).astype(BF16), row(w['b_rg_x']), row(w['lru_lambda']),
               row(w['g_out_b'])),
        mlstm=(jnp.pad(bias_gate, (0, LANES - 2 * N_HEADS_C)).reshape(1, LANES),
               bias_gate.reshape(2 * N_HEADS_C, 1), row(w['g_out_c'])),
        outproj=(row(w['g_out_a']), w['w_out'][:D_A].astype(BF16),
                 w['w_out'][D_A:D_A + D_B].astype(BF16), w['w_out'][D_A + D_B:].astype(BF16),
                 row(w['g_mix_post'])),
        ple=(row(w['g_ple_pre']), w['w_ple_gate'].astype(BF16), w['w_ple_proj'].astype(BF16),
             row(w['g_ple_post'])),
    )


def _gate_rows(z3, col0):
    lo = (ZC_GATE - col0) * LANES
    return jnp.swapaxes(z3[:, :, lo:lo + 2 * N_HEADS_C], 1, 2)


def _layer(x, pe, lw, cache):
    b, t_len, d = x.shape
    rows = x.reshape(b * t_len, d)
    rows = _ffn(rows, *lw['f1'])
    z3 = _inproj(rows, *lw['inproj']).reshape(b, t_len, Z_COLS)
    ka = z3[:, :, ZC_KA * LANES:ZC_KA * LANES + D_A]
    va = z3[:, :, ZC_VA * LANES:ZC_VA * LANES + D_A]
    if cache is None:
        keep = min(DILATED_CONFIGS[-1][0], t_len)
        new_k = ka[:, t_len - keep:].reshape(b, keep, N_HEADS_A, HEAD_DIM)
        new_v = va[:, t_len - keep:].reshape(b, keep, N_HEADS_A, HEAD_DIM)
        ya = _attn_prompt(z3)
        conv_buf = jnp.zeros((b, CONV_W - 1, D_B), F32)
        h0 = jnp.zeros((b, D_B), F32)
        c0 = jnp.zeros((b, 2, LANES, LANES), F32)
        s0 = jnp.zeros((b, 8, LANES), F32)
        zc, col0, n_valid = z3, 0, MLSTM_CHUNK
    else:
        k_cache, v_cache, conv_buf, h0, C0, n0, m0 = cache
        new_k = ka.reshape(b, t_len, N_HEADS_A, HEAD_DIM)
        new_v = va.reshape(b, t_len, N_HEADS_A, HEAD_DIM)
        ya = _attn_sample(z3, k_cache, v_cache)
        c0, s0 = _pack_mlstm_state(C0, n0, m0)
        col0 = ZC_QC
        zc = jnp.pad(z3[:, :, col0 * LANES:], ((0, 0), (0, MLSTM_CHUNK - t_len), (0, 0)))
        n_valid = t_len
    yb, new_buf, h_last = _rglru(z3, conv_buf, h0, *lw['rglru'])
    yc, c1, s1 = _mlstm(zc, col0, _gate_rows(zc, col0), n_valid, *lw['mlstm'], c0, s0)
    C1, n1, m1 = _unpack_mlstm_state(c1, s1)
    yc = yc[:, :t_len]
    rows = _outproj(rows, ya.reshape(b * t_len, D_A), yb.reshape(b * t_len, D_B),
                    yc.reshape(b * t_len, D_C), *lw['outproj'])
    rows = _ffn(rows, *lw['f2'])
    rows = _ple(rows, pe.reshape(b * t_len, -1), *lw['ple'])
    return rows.reshape(b, t_len, d), (new_k, new_v, new_buf, h_last.reshape(b, D_B), C1, n1, m1)


def kernel(x_prompt, x_sample, cache_k, cache_v, state_conv, state_h, state_C, state_n, state_m, p_prompt, p_sample, g_f1_pre, w_f1_gate, w_f1_up, w_f1_down, g_f1_post, g_mix_pre, w_in, conv_w, conv_b, w_rg_a, b_rg_a, w_rg_x, b_rg_x, lru_lambda, b_mlstm_i, b_mlstm_f, g_out_a, g_out_b, g_out_c, w_out, g_mix_post, g_f2_pre, w_f2_gate, w_f2_up, w_f2_down, g_f2_post, g_ple_pre, w_ple_gate, w_ple_proj, g_ple_post):
    depth = w_in.shape[0]
    xp, xs = x_prompt, x_sample
    sp = [[] for _ in range(7)]
    ss = [[] for _ in range(7)]
    for i in range(depth):
        lw = _prep_layer(dict(
            g_f1_pre=g_f1_pre[i], w_f1_gate=w_f1_gate[i], w_f1_up=w_f1_up[i], w_f1_down=w_f1_down[i],
            g_f1_post=g_f1_post[i], g_mix_pre=g_mix_pre[i], w_in=w_in[i], conv_w=conv_w[i],
            conv_b=conv_b[i], w_rg_a=w_rg_a[i], b_rg_a=b_rg_a[i], w_rg_x=w_rg_x[i], b_rg_x=b_rg_x[i],
            lru_lambda=lru_lambda[i], b_mlstm_i=b_mlstm_i[i], b_mlstm_f=b_mlstm_f[i],
            g_out_a=g_out_a[i], g_out_b=g_out_b[i], g_out_c=g_out_c[i], w_out=w_out[i],
            g_mix_post=g_mix_post[i], g_f2_pre=g_f2_pre[i], w_f2_gate=w_f2_gate[i],
            w_f2_up=w_f2_up[i], w_f2_down=w_f2_down[i], g_f2_post=g_f2_post[i],
            g_ple_pre=g_ple_pre[i], w_ple_gate=w_ple_gate[i], w_ple_proj=w_ple_proj[i],
            g_ple_post=g_ple_post[i]))
        xp, st_p = _layer(xp, p_prompt[i], lw, None)
        cache_i = (cache_k[i], cache_v[i], state_conv[i], state_h[i], state_C[i], state_n[i], state_m[i])
        xs, st_s = _layer(xs, p_sample[i], lw, cache_i)
        for j in range(7):
            sp[j].append(st_p[j])
            ss[j].append(st_s[j])
    k_prompt, v_prompt, conv_prompt, h_prompt, C_prompt, n_prompt, m_prompt = [jnp.stack(a) for a in sp]
    k_sample, v_sample, conv_sample, h_sample, C_sample, n_sample, m_sample = [jnp.stack(a) for a in ss]
    return (xp, xs, k_prompt, v_prompt, k_sample, v_sample, conv_prompt, conv_sample,
            h_prompt, h_sample, C_prompt, C_sample, n_prompt, n_sample, m_prompt, m_sample)
```

```python
import functools

import jax
import jax.numpy as jnp
from jax import lax
from jax.experimental import pallas as pl
from jax.experimental.pallas import tpu as pltpu

F32 = jnp.float32
BF16 = jnp.bfloat16

EPS = 1e-6
NEG = -1e30
HEAD_DIM = 64
N_HEADS_A = 6
D_A = N_HEADS_A * HEAD_DIM
D_B = 384
N_HEADS_C = 4
D_C = N_HEADS_C * HEAD_DIM
DILATED_CONFIGS = ((128, 1), (512, 4), (2048, 16))
LOCAL_BLK = 128
CONV_W = 4
LRU_C = 8.0
MLSTM_CHUNK = 128
LANES = 128
SUBLANES = 8
Z_COLS = 3072
FF_CHUNK = 256
ROW_TILE = 512
VMEM_LIMIT = 56 * 1024 * 1024

ZC_QA, ZC_KA, ZC_VA, ZC_XB, ZC_GB = 0, 3, 6, 9, 12
ZC_QC, ZC_KC, ZC_VC, ZC_OC, ZC_GATE = 15, 17, 19, 21, 23


def _rms(x, g):
    return x * lax.rsqrt(jnp.mean(x * x, axis=-1, keepdims=True) + EPS) * g


def _dot(a, b):
    return jnp.dot(a, b, preferred_element_type=F32)


def _dot_nt(a, b):
    return lax.dot_general(a, b, (((1,), (1,)), ((), ())), preferred_element_type=F32)


def _dot_tn(a, b):
    return lax.dot_general(a, b, (((0,), (0,)), ((), ())), preferred_element_type=F32)


def _dot_exact(a, b):
    return jnp.dot(a, b, preferred_element_type=F32, precision=lax.Precision.HIGHEST)


def _softplus(x):
    return jnp.maximum(x, 0.0) + jnp.log1p(jnp.exp(-jnp.abs(x)))


def _gelu_tanh(x):
    return 0.5 * x * (1.0 + jnp.tanh(0.7978845608028654 * (x + 0.044715 * (x * x * x))))


def _const_spec(shape):
    nd = len(shape)
    return pl.BlockSpec(shape, lambda *_: (0,) * nd, pipeline_mode=pl.Buffered(1))


def _params(sem):
    return pltpu.CompilerParams(dimension_semantics=sem, vmem_limit_bytes=VMEM_LIMIT)


def _ffn_kernel(x_ref, gpre_ref, wg_ref, wu_ref, wd_ref, gpost_ref, o_ref, acc_ref):
    x = x_ref[...]
    h = _rms(x, gpre_ref[...]).astype(BF16)
    n_chunks = wg_ref.shape[1] // FF_CHUNK
    for c in range(n_chunks):
        sl = slice(c * FF_CHUNK, (c + 1) * FF_CHUNK)
        g = _dot(h, wg_ref[:, sl])
        u = _dot(h, wu_ref[:, sl])
        a = (g * jax.nn.sigmoid(g) * u).astype(BF16)
        d = _dot(a, wd_ref[sl, :])
        if c == 0:
            acc_ref[...] = d
        else:
            acc_ref[...] += d
    o_ref[...] = x + 0.5 * _rms(acc_ref[...], gpost_ref[...])


def _ffn(x, gpre, wg, wu, wd, gpost):
    m, d = x.shape
    ff = wg.shape[1]
    tm = min(ROW_TILE, m)
    row = pl.BlockSpec((tm, d), lambda i: (i, 0))
    return pl.pallas_call(
        _ffn_kernel,
        grid=(m // tm,),
        in_specs=[row, _const_spec((1, d)), _const_spec((d, ff)), _const_spec((d, ff)),
                  _const_spec((ff, d)), _const_spec((1, d))],
        out_specs=row,
        out_shape=jax.ShapeDtypeStruct((m, d), F32),
        scratch_shapes=[pltpu.VMEM((tm, d), F32)],
        compiler_params=_params(("parallel",)),
        name="ffn",
    )(x, gpre, wg, wu, wd, gpost)


def _inproj_kernel(x_ref, g_ref, w_ref, z_ref):
    h = _rms(x_ref[...], g_ref[...]).astype(BF16)
    for c in range(w_ref.shape[1] // 256):
        sl = slice(c * 256, (c + 1) * 256)
        z_ref[:, sl] = _dot(h, w_ref[:, sl])


def _inproj(x, g, w):
    m, d = x.shape
    n = w.shape[1]
    tm = min(ROW_TILE, m)
    return pl.pallas_call(
        _inproj_kernel,
        grid=(m // tm,),
        in_specs=[pl.BlockSpec((tm, d), lambda i: (i, 0)), _const_spec((1, d)), _const_spec((d, n))],
        out_specs=pl.BlockSpec((tm, n), lambda i: (i, 0)),
        out_shape=jax.ShapeDtypeStruct((m, n), F32),
        compiler_params=_params(("parallel",)),
        name="inproj",
    )(x, g, w)


def _outproj_kernel(x_ref, ya_ref, yb_ref, yc_ref, ga_ref, wa_ref, wb_ref, wc_ref, gpost_ref, o_ref):
    ya = _rms(ya_ref[...], ga_ref[...]).astype(BF16)
    y = _dot(ya, wa_ref[...])
    y = y + _dot(yb_ref[...].astype(BF16), wb_ref[...])
    y = y + _dot(yc_ref[...].astype(BF16), wc_ref[...])
    o_ref[...] = x_ref[...] + _rms(y, gpost_ref[...])


def _outproj(x, ya, yb, yc, ga, wa, wb, wc, gpost):
    m, d = x.shape
    tm = min(ROW_TILE, m)

    def row(width):
        return pl.BlockSpec((tm, width), lambda i: (i, 0))

    return pl.pallas_call(
        _outproj_kernel,
        grid=(m // tm,),
        in_specs=[row(d), row(D_A), row(D_B), row(D_C), _const_spec((1, D_A)),
                  _const_spec((D_A, d)), _const_spec((D_B, d)), _const_spec((D_C, d)),
                  _const_spec((1, d))],
        out_specs=row(d),
        out_shape=jax.ShapeDtypeStruct((m, d), F32),
        compiler_params=_params(("parallel",)),
        name="outproj",
    )(x, ya, yb, yc, ga, wa, wb, wc, gpost)


def _ple_kernel(x_ref, pe_ref, gpre_ref, wg_ref, wp_ref, gpost_ref, o_ref):
    x = x_ref[...]
    h = _rms(x, gpre_ref[...]).astype(BF16)
    gate = jax.nn.sigmoid(_dot(h, wg_ref[...]))
    proj = _dot(pe_ref[...].astype(BF16), wp_ref[...])
    o_ref[...] = x + _rms(gate * proj, gpost_ref[...])


def _ple(x, pe, gpre, wg, wp, gpost):
    m, d = x.shape
    dp = pe.shape[1]
    tm = min(ROW_TILE, m)
    return pl.pallas_call(
        _ple_kernel,
        grid=(m // tm,),
        in_specs=[pl.BlockSpec((tm, d), lambda i: (i, 0)), pl.BlockSpec((tm, dp), lambda i: (i, 0)),
                  _const_spec((1, d)), _const_spec((d, d)), _const_spec((dp, d)), _const_spec((1, d))],
        out_specs=pl.BlockSpec((tm, d), lambda i: (i, 0)),
        out_shape=jax.ShapeDtypeStruct((m, d), F32),
        compiler_params=_params(("parallel",)),
        name="ple",
    )(x, pe, gpre, wg, wp, gpost)


ATT_QT = 2048
ATT_UNROLL = 4


def _attn_prompt_kernel(q_ref, k_ref, v_ref, o_ref, num_scr, m_scr, den_scr):
    t0 = pl.program_id(2) * ATT_QT
    blk = LOCAL_BLK
    lane = lax.broadcasted_iota(jnp.int32, (blk, LANES), 1)
    row = lax.broadcasted_iota(jnp.int32, (blk, LANES), 0)
    lo_head = lane < HEAD_DIM
    key_minus_query = lane - row
    mask_cur = key_minus_query <= 0
    scale = HEAD_DIM ** -0.5

    for ci, (_, dil) in enumerate(DILATED_CONFIGS):
        span = blk * dil

        def scores(j, dil=dil, span=span):
            r = j % dil
            n = j // dil
            qs = r + span * n
            ks = t0 + qs
            first = ks < span
            ps = jnp.where(first, ks, ks - span)
            prev_floor = jnp.where(first, 2 * blk, 0)
            mask_prev = key_minus_query >= prev_floor
            q2 = q_ref[pl.ds(qs, blk, stride=dil), :] * scale
            kc = k_ref[pl.ds(ks, blk, stride=dil), :].astype(BF16)
            kp = k_ref[pl.ds(ps, blk, stride=dil), :].astype(BF16)
            raw = []
            for h in range(2):
                hm = lo_head if h == 0 else jnp.logical_not(lo_head)
                qh = jnp.where(hm, q2, 0.0).astype(BF16)
                raw.append((_dot_nt(qh, kc), _dot_nt(qh, kp)))
            return qs, ks, ps, mask_prev, raw

        def softmax(mask_prev, raw):
            probs = []
            for sc, sp in raw:
                sc = jnp.where(mask_cur, sc, NEG)
                sp = jnp.where(mask_prev, sp, NEG)
                m = jnp.max(jnp.maximum(sc, sp), axis=-1, keepdims=True)
                pc = jnp.exp(sc - m)
                pp = jnp.exp(sp - m)
                den = jnp.sum(pc + pp, axis=-1, keepdims=True)
                probs.append((pc.astype(BF16), pp.astype(BF16), m, den))
            return probs

        def weighted_values(qs, ks, ps, probs, ci=ci, dil=dil):
            vc = v_ref[pl.ds(ks, blk, stride=dil), :].astype(BF16)
            vp = v_ref[pl.ds(ps, blk, stride=dil), :].astype(BF16)
            (pc0, pp0, m0, d0), (pc1, pp1, m1, d1) = probs
            n0 = _dot(pc0, vc) + _dot(pp0, vp)
            n1 = _dot(pc1, vc) + _dot(pp1, vp)
            dst = pl.ds(qs, blk, stride=dil)
            num_scr[ci, dst, :] = jnp.where(lo_head, n0, n1)
            m_scr[ci, dst, :] = jnp.where(lo_head, m0, m1)
            den_scr[ci, dst, :] = jnp.where(lo_head, d0, d1)

        def body(jj, carry, scores=scores, softmax=softmax, weighted_values=weighted_values):
            staged = [scores(jj * ATT_UNROLL + u) for u in range(ATT_UNROLL)]
            probs = [softmax(st[3], st[4]) for st in staged]
            for st, pr in zip(staged, probs):
                weighted_values(st[0], st[1], st[2], pr)
            return carry

        lax.fori_loop(0, ATT_QT // blk // ATT_UNROLL, body, 0)

    rows = 256

    def combine(i, carry):
        sl = pl.ds(pl.multiple_of(i * rows, rows), rows)
        ms = [m_scr[c, sl, :] for c in range(3)]
        mx = jnp.maximum(jnp.maximum(ms[0], ms[1]), ms[2])
        ws = [jnp.exp(mc - mx) for mc in ms]
        num = num_scr[0, sl, :] * ws[0] + num_scr[1, sl, :] * ws[1] + num_scr[2, sl, :] * ws[2]
        den = den_scr[0, sl, :] * ws[0] + den_scr[1, sl, :] * ws[1] + den_scr[2, sl, :] * ws[2]
        o_ref[sl, :] = num / den
        return carry

    lax.fori_loop(0, ATT_QT // rows, combine, 0)


def _attn_prompt(z3):
    b, s, _ = z3.shape
    n_pairs = D_A // LANES
    q_spec = pl.BlockSpec((None, ATT_QT, LANES), lambda i, p, t: (i, t, ZC_QA + p))
    k_spec = pl.BlockSpec((None, s, LANES), lambda i, p, t: (i, 0, ZC_KA + p))
    v_spec = pl.BlockSpec((None, s, LANES), lambda i, p, t: (i, 0, ZC_VA + p))
    return pl.pallas_call(
        _attn_prompt_kernel,
        grid=(b, n_pairs, s // ATT_QT),
        in_specs=[q_spec, k_spec, v_spec],
        out_specs=pl.BlockSpec((None, ATT_QT, LANES), lambda i, p, t: (i, t, p)),
        out_shape=jax.ShapeDtypeStruct((b, s, D_A), F32),
        scratch_shapes=[pltpu.VMEM((3, ATT_QT, LANES), F32)] * 3,
        compiler_params=_params(("parallel", "parallel", "arbitrary")),
        name="attn_prompt",
    )(z3, z3, z3)


def _attn_sample_kernel(q_ref, kn_ref, vn_ref, kt_ref, vt_ref, o_ref, q_scr, kn_scr, vn_scr):
    t_new = q_ref.shape[0]
    w_buf = kt_ref.shape[1]
    q_scr[...] = jnp.zeros(q_scr.shape, F32)
    kn_scr[...] = jnp.zeros(kn_scr.shape, F32)
    vn_scr[...] = jnp.zeros(vn_scr.shape, F32)
    q_scr[0:t_new, :] = q_ref[...] * (HEAD_DIM ** -0.5)
    kn_scr[0:t_new, :] = kn_ref[...]
    vn_scr[0:t_new, :] = vn_ref[...]

    rows = 2 * SUBLANES
    tq = lax.broadcasted_iota(jnp.int32, (rows, w_buf), 0) & (SUBLANES - 1)
    dist = w_buf + tq - lax.broadcasted_iota(jnp.int32, (rows, w_buf), 1)
    tqn = lax.broadcasted_iota(jnp.int32, (rows, LANES), 0) & (SUBLANES - 1)
    tn = lax.broadcasted_iota(jnp.int32, (rows, LANES), 1)
    dist_new = tqn - tn
    cache_ok, new_ok = [], []
    for window, dil in DILATED_CONFIGS:
        ok = dist <= window
        nk = jnp.logical_and(dist_new >= 0, tn < t_new)
        nk = jnp.logical_and(nk, dist_new <= window)
        if dil > 1:
            ok = jnp.logical_and(ok, (dist & (dil - 1)) == 0)
            nk = jnp.logical_and(nk, (dist_new & (dil - 1)) == 0)
        cache_ok.append(ok)
        new_ok.append(nk)
    lo8 = lax.broadcasted_iota(jnp.int32, (SUBLANES, LANES), 1) < HEAD_DIM

    for p in range(D_A // LANES):
        sl = slice(p * LANES, (p + 1) * LANES)
        q2 = q_scr[:, sl]
        qm = jnp.concatenate([jnp.where(lo8, q2, 0.0), jnp.where(lo8, 0.0, q2)], axis=0).astype(BF16)
        kb = kt_ref[sl, :].astype(BF16)
        vb = vt_ref[sl, :].astype(BF16)
        s = _dot(qm, kb)
        s_new = _dot_nt(qm, kn_scr[:, sl].astype(BF16))
        parts = []
        for c in range(len(DILATED_CONFIGS)):
            sc = jnp.where(cache_ok[c], s, NEG)
            sn = jnp.where(new_ok[c], s_new, NEG)
            m = jnp.maximum(jnp.max(sc, axis=1, keepdims=True), jnp.max(sn, axis=1, keepdims=True))
            pc = jnp.exp(sc - m)
            pn = jnp.exp(sn - m)
            den = jnp.sum(pc, axis=1, keepdims=True) + jnp.sum(pn, axis=1, keepdims=True)
            parts.append((pc, pn, m, den))
        mx = jnp.maximum(jnp.maximum(parts[0][2], parts[1][2]), parts[2][2])
        ws = [jnp.exp(pt[2] - mx) for pt in parts]
        den_all = parts[0][3] * ws[0] + parts[1][3] * ws[1] + parts[2][3] * ws[2]
        coefs = [w / den_all for w in ws]
        p_all = parts[0][0] * coefs[0] + parts[1][0] * coefs[1] + parts[2][0] * coefs[2]
        pn_all = parts[0][1] * coefs[0] + parts[1][1] * coefs[1] + parts[2][1] * coefs[2]
        o16 = _dot_nt(p_all.astype(BF16), vb) + _dot(pn_all.astype(BF16), vn_scr[:, sl].astype(BF16))
        o8 = jnp.where(lo8, o16[0:SUBLANES], o16[SUBLANES:rows])
        o_ref[:, sl] = o8[0:t_new]


def _attn_sample(zs3, kt_all, vt_all, layer):
    b, t_new, _ = zs3.shape
    w_buf = kt_all.shape[-1]

    def new_spec(col):
        return pl.BlockSpec((None, t_new, D_A), lambda i: (i, 0, col))

    cache_spec = pl.BlockSpec((None, None, D_A, w_buf), lambda i: (layer, i, 0, 0))
    return pl.pallas_call(
        _attn_sample_kernel,
        grid=(b,),
        in_specs=[new_spec(0), new_spec(1), new_spec(2), cache_spec, cache_spec],
        out_specs=pl.BlockSpec((None, t_new, D_A), lambda i: (i, 0, 0)),
        out_shape=jax.ShapeDtypeStruct((b, t_new, D_A), F32),
        scratch_shapes=[pltpu.VMEM((SUBLANES, D_A), F32), pltpu.VMEM((LANES, D_A), F32),
                        pltpu.VMEM((LANES, D_A), F32)],
        compiler_params=_params(("parallel",)),
        name="attn_sample",
    )(zs3, zs3, zs3, kt_all, vt_all)


RG_GROUP = SUBLANES
RG_CHUNK = 256
RG_PAD = 8


def _rg_pitch(tcp):
    tiles = tcp // 8 + 1
    return 8 * (tiles if tiles % 2 else tiles + 1)


def _rglru_kernel(xb_ref, gb_ref, conv_ref, h0_ref, cw_ref, cb_ref, wa_ref, ba_ref, wx_ref, bx_ref,
                  lam_ref, g_ref, y_ref, convnew_ref, ht_ref, xp_scr, hist_scr, h_scr, a_scr, b_scr):
    n_seq, tc, _ = xb_ref.shape
    tcp = xp_scr.shape[0] - RG_PAD
    pitch = a_scr.shape[1] // n_seq
    n_hist = CONV_W - 1
    hist = slice(RG_PAD - n_hist, RG_PAD)
    n_groups = D_B // LANES

    @pl.when(pl.program_id(1) == 0)
    def _():
        hist_scr[:, hist, :] = conv_ref[...]
        h_scr[...] = h0_ref[...]

    if tc % 8:
        xp_scr[RG_PAD:, :] = jnp.zeros((tcp, D_B), F32)
    decay = _softplus(-lam_ref[...])

    def gates(g, carry):
        xp_scr[hist, :] = hist_scr[g, hist, :]
        xp_scr[RG_PAD:RG_PAD + tc, :] = xb_ref[g]
        hist_scr[g, hist, :] = xp_scr[RG_PAD + tc - n_hist:RG_PAD + tc, :]
        xc = cb_ref[...]
        for j in range(CONV_W):
            lo = RG_PAD - n_hist + j
            xc = xc + xp_scr[lo:lo + tcp, :] * cw_ref[pl.ds(j, 1), :]
        xcb = xc.astype(BF16)
        r = jax.nn.sigmoid(_dot(xcb, wa_ref[...]) + ba_ref[...])
        gi = jax.nn.sigmoid(_dot(xcb, wx_ref[...]) + bx_ref[...])
        a = jnp.exp(-LRU_C * r * decay)
        bb = jnp.sqrt(1.0 - a * a) * (gi * xc)
        dst = pl.ds(pl.multiple_of(g * pitch, 8), tcp)
        for lg in range(n_groups):
            a_scr[lg, dst, :] = a[:, lg * LANES:(lg + 1) * LANES]
            b_scr[lg, dst, :] = bb[:, lg * LANES:(lg + 1) * LANES]
        return carry

    lax.fori_loop(0, n_seq, gates, 0)

    def step(t, hs):
        rows = pl.ds(t, n_seq, stride=pitch)
        out = []
        for lg in range(n_groups):
            h = a_scr[lg, rows, :] * hs[lg] + b_scr[lg, rows, :]
            b_scr[lg, rows, :] = h
            out.append(h)
        return tuple(out)

    hs = tuple(h_scr[:, lg * LANES:(lg + 1) * LANES] for lg in range(n_groups))
    hs = lax.fori_loop(0, tc, step, hs, unroll=min(8, tc))
    for lg in range(n_groups):
        h_scr[:, lg * LANES:(lg + 1) * LANES] = hs[lg]

    def finish(g, carry):
        src = pl.ds(pl.multiple_of(g * pitch, 8), tcp)
        hseq = jnp.concatenate([b_scr[lg, src, :] for lg in range(n_groups)], axis=1)
        y = _rms(_gelu_tanh(gb_ref[g]) * hseq[0:tc], g_ref[...])
        y_ref[g] = y
        return carry

    lax.fori_loop(0, n_seq, finish, 0)
    convnew_ref[...] = hist_scr[:, hist, :]
    ht_ref[...] = h_scr[...]


def _rglru(z3, conv_buf, h0, cw, cb, wa, ba, wx, bx, lam, g):
    b, t_len, _ = z3.shape
    tc = min(RG_CHUNK, t_len)
    tcp = -(-tc // 8) * 8
    pitch = _rg_pitch(tcp)
    n_hist = CONV_W - 1
    ng = b // RG_GROUP
    vec = _const_spec((1, D_B))
    seq = pl.BlockSpec((RG_GROUP, tc, D_B), lambda i, c: (i, c, 0))
    hist = pl.BlockSpec((RG_GROUP, n_hist, D_B), lambda i, c: (i, 0, 0))
    state = pl.BlockSpec((None, RG_GROUP, D_B), lambda i, c: (i, 0, 0))
    y, conv_new, h_last = pl.pallas_call(
        _rglru_kernel,
        grid=(ng, t_len // tc),
        in_specs=[pl.BlockSpec((RG_GROUP, tc, D_B), lambda i, c: (i, c, ZC_XB // 3)),
                  pl.BlockSpec((RG_GROUP, tc, D_B), lambda i, c: (i, c, ZC_GB // 3)),
                  hist, state,
                  _const_spec((CONV_W, D_B)), vec, _const_spec((D_B, D_B)), vec,
                  _const_spec((D_B, D_B)), vec, vec, vec],
        out_specs=[seq, hist, state],
        out_shape=[jax.ShapeDtypeStruct((b, t_len, D_B), F32),
                   jax.ShapeDtypeStruct((b, n_hist, D_B), F32),
                   jax.ShapeDtypeStruct((ng, RG_GROUP, D_B), F32)],
        scratch_shapes=[pltpu.VMEM((RG_PAD + tcp, D_B), F32),
                        pltpu.VMEM((RG_GROUP, RG_PAD, D_B), F32),
                        pltpu.VMEM((RG_GROUP, D_B), F32),
                        pltpu.VMEM((D_B // LANES, RG_GROUP * pitch, LANES), F32),
                        pltpu.VMEM((D_B // LANES, RG_GROUP * pitch, LANES), F32)],
        compiler_params=_params(("parallel", "arbitrary")),
        name="rglru",
    )(z3, z3, conv_buf, h0.reshape(ng, RG_GROUP, D_B), cw, cb, wa, ba, wx, bx, lam, g)
    return y, conv_new, h_last.reshape(b, D_B)


ML_GROUP = 4


def _log_sigmoid(x):
    return -_softplus(-x)


def _mlstm_kernel(n_valid, q0_ref, q1_ref, k0_ref, k1_ref, v0_ref, v1_ref, o0_ref, o1_ref,
                  gc_ref, gr_ref, bias_r_ref, bias_c_ref, gain_ref, c0_ref, s0_ref,
                  y_ref, c_out_ref, s_out_ref, c_scr, s_scr):
    ck = pl.program_id(1)
    n_seq, L, _ = gc_ref.shape

    @pl.when(ck == 0)
    def _():
        c_scr[...] = c0_ref[...]
        s_scr[...] = s0_ref[...]

    lane = lax.broadcasted_iota(jnp.int32, (L, LANES), 1)
    row = lax.broadcasted_iota(jnp.int32, (L, LANES), 0)
    lo_head = lane < HEAD_DIM
    lo_row = lax.broadcasted_iota(jnp.int32, (1, LANES), 1) < HEAD_DIM
    causal = lax.broadcasted_iota(jnp.int32, (L, L), 1) <= lax.broadcasted_iota(jnp.int32, (L, L), 0)
    tri = jnp.where(causal, 1.0, 0.0).astype(F32)
    tri_t = jnp.where(lax.broadcasted_iota(jnp.int32, (L, L), 0)
                      <= lax.broadcasted_iota(jnp.int32, (L, L), 1), 1.0, 0.0).astype(F32)
    blockdiag = (lax.broadcasted_iota(jnp.int32, (LANES, LANES), 0) < HEAD_DIM) == \
                (lax.broadcasted_iota(jnp.int32, (LANES, LANES), 1) < HEAD_DIM)
    row_lo = lax.broadcasted_iota(jnp.int32, (LANES, 1), 0) < HEAD_DIM
    sub = lax.broadcasted_iota(jnp.int32, (8, L), 0)
    tl = lax.broadcasted_iota(jnp.int32, (8, L), 1)
    lane_row = lax.broadcasted_iota(jnp.int32, (1, LANES), 1)

    refs = ((q0_ref, k0_ref, v0_ref, o0_ref), (q1_ref, k1_ref, v1_ref, o1_ref))

    def gate_sums(sq):
        xc = gc_ref[sq] + bias_r_ref[...]
        gt_c = jnp.where(lane < N_HEADS_C, xc, _log_sigmoid(xc))
        xr = gr_ref[sq] + bias_c_ref[...]
        gt_r = jnp.where(sub < N_HEADS_C, xr, _log_sigmoid(xr))
        if n_valid < L:
            gt_c = jnp.where(row < n_valid, gt_c, jnp.where(lane < N_HEADS_C, NEG, 0.0))
            gt_r = jnp.where(tl < n_valid, gt_r, jnp.where(sub < N_HEADS_C, NEG, 0.0))
        b_col = _dot_exact(tri, gt_c)
        b_row = _dot_exact(gt_r, tri_t)
        return gt_c, gt_r, b_col, b_row

    def score_stage(sq, p):
        q_ref, k_ref, v_ref, _ = refs[p]
        q2 = q_ref[sq]
        k2 = k_ref[sq] * (HEAD_DIM ** -0.5)
        v2b = v_ref[sq].astype(BF16)
        k2b = k2.astype(BF16)
        c_pair = c_scr[sq, p]
        n_pair = s_scr[sq, pl.ds(p, 1), :]
        qc = _dot(q2.astype(BF16), c_pair.astype(BF16))
        s_heads = [_dot_nt(jnp.where(lo_head if h == 0 else jnp.logical_not(lo_head), q2, 0.0)
                           .astype(BF16), k2b) for h in range(2)]
        return q2, k2, v2b, c_pair, n_pair, qc, s_heads

    def weight_stage(sq, p, gates, staged, m_row):
        gt_c, gt_r, b_col, b_row = gates
        q2, _, _, _, n_pair, _, s_heads = staged
        qn = q2 * n_pair
        out = []
        for h in range(2):
            gh = 2 * p + h
            hm = lo_head if h == 0 else jnp.logical_not(lo_head)
            b_c = b_col[:, N_HEADS_C + gh:N_HEADS_C + gh + 1]
            b_r = b_row[N_HEADS_C + gh:N_HEADS_C + gh + 1, :]
            ig_r = gt_r[gh:gh + 1, :]
            ig_c = gt_c[:, gh:gh + 1]
            m_prev = m_row[:, gh:gh + 1]
            dm = jnp.where(causal, b_c - b_r + ig_r, NEG)
            inter = b_c + m_prev
            mt = jnp.maximum(inter, jnp.max(dm, axis=1, keepdims=True))
            wi = jnp.exp(inter - mt)
            a = jnp.exp(dm - mt) * s_heads[h]
            den = jnp.sum(a, axis=1, keepdims=True) \
                + wi * jnp.sum(jnp.where(hm, qn, 0.0), axis=1, keepdims=True)
            scale = 1.0 / jnp.maximum(jnp.abs(den), jnp.exp(-mt))
            m_new = mt[L - 1:L, :]
            b_last = b_c[L - 1:L, :]
            ws = jnp.exp(b_last - b_c + ig_c - m_new)
            wc = jnp.exp(b_last + m_prev - m_new)
            out.append((a.astype(BF16), wi, scale, ws, wc, m_new))
        return out

    def value_stage(sq, p, staged, weights):
        _, _, _, o_ref = refs[p]
        _, k2, v2b, c_pair, n_pair, qc, _ = staged
        (a0, wi0, sc0, ws0, wc0, _), (a1, wi1, sc1, ws1, wc1, _) = weights
        h0 = (_dot(a0, v2b) + wi0 * qc) * sc0
        h1 = (_dot(a1, v2b) + wi1 * qc) * sc1
        hout2 = jnp.where(lo_head, h0, h1)
        kw = k2 * jnp.where(lo_head, ws0, ws1)
        upd = _dot_tn(kw.astype(BF16), v2b)
        c_scr[sq, p] = jnp.where(blockdiag, jnp.where(row_lo, wc0, wc1) * c_pair + upd, 0.0)
        s_scr[sq, pl.ds(p, 1), :] = jnp.where(lo_row, wc0, wc1) * n_pair \
            + jnp.sum(kw, axis=0, keepdims=True)
        hc = jax.nn.sigmoid(o_ref[sq]) * hout2
        hsq = hc * hc
        ms0 = jnp.sum(jnp.where(lo_head, hsq, 0.0), axis=1, keepdims=True)
        ms1 = jnp.sum(jnp.where(lo_head, 0.0, hsq), axis=1, keepdims=True)
        ms = jnp.where(lo_head, ms0, ms1) * (1.0 / HEAD_DIM)
        y_ref[sq, :, p * LANES:(p + 1) * LANES] = \
            hc * lax.rsqrt(ms + EPS) * gain_ref[:, p * LANES:(p + 1) * LANES]

    units = [(sq, p) for sq in range(n_seq) for p in range(2)]
    gates = [gate_sums(sq) for sq in range(n_seq)]
    staged = [score_stage(sq, p) for sq, p in units]
    m_rows = [s_scr[sq, pl.ds(2, 1), :] for sq in range(n_seq)]
    weights = [weight_stage(sq, p, gates[sq], st, m_rows[sq]) for (sq, p), st in zip(units, staged)]
    for (sq, p), st, wt in zip(units, staged, weights):
        value_stage(sq, p, st, wt)
    for sq in range(n_seq):
        m_row_new = m_rows[sq]
        for (usq, p), wt in zip(units, weights):
            if usq == sq:
                for h in range(2):
                    m_row_new = jnp.where(lane_row == 2 * p + h, wt[h][5], m_row_new)
        s_scr[sq, pl.ds(2, 1), :] = m_row_new

    @pl.when(ck == pl.num_programs(1) - 1)
    def _():
        c_out_ref[...] = c_scr[...]
        s_out_ref[...] = s_scr[...]


def _mlstm(z3, col0, g_rows, n_valid, bias_r, bias_c, gain, c0, s0):
    b, t_len, _ = z3.shape
    L = MLSTM_CHUNK
    nc = t_len // L
    G = ML_GROUP

    def col(cb):
        return pl.BlockSpec((G, L, LANES), lambda i, c: (i, c, cb - col0))

    state_c = pl.BlockSpec((G, 2, LANES, LANES), lambda i, c: (i, 0, 0, 0))
    state_s = pl.BlockSpec((G, 8, LANES), lambda i, c: (i, 0, 0))
    return pl.pallas_call(
        functools.partial(_mlstm_kernel, n_valid),
        grid=(b // G, nc),
        in_specs=[col(ZC_QC), col(ZC_QC + 1), col(ZC_KC), col(ZC_KC + 1), col(ZC_VC), col(ZC_VC + 1),
                  col(ZC_OC), col(ZC_OC + 1), col(ZC_GATE),
                  pl.BlockSpec((G, 8, L), lambda i, c: (i, 0, c)),
                  _const_spec((1, LANES)), _const_spec((8, 1)), _const_spec((1, D_C)),
                  state_c, state_s],
        out_specs=[pl.BlockSpec((G, L, D_C), lambda i, c: (i, c, 0)), state_c, state_s],
        out_shape=[jax.ShapeDtypeStruct((b, t_len, D_C), F32),
                   jax.ShapeDtypeStruct((b, 2, LANES, LANES), F32),
                   jax.ShapeDtypeStruct((b, 8, LANES), F32)],
        scratch_shapes=[pltpu.VMEM((G, 2, LANES, LANES), F32), pltpu.VMEM((G, 8, LANES), F32)],
        compiler_params=_params(("parallel", "arbitrary")),
        name="mlstm",
    )(*([z3] * 9), g_rows, bias_r, bias_c, gain, c0, s0)


def _pack_mlstm_state(C, n, m):
    b = C.shape[0]
    Cp = C.reshape(b, 2, 2, HEAD_DIM, HEAD_DIM)
    eye = jnp.eye(2, dtype=C.dtype)
    c_pairs = jnp.einsum('bphkv,hg->bphkgv', Cp, eye).reshape(b, 2, LANES, LANES)
    slab = jnp.zeros((b, 8, LANES), C.dtype)
    slab = slab.at[:, 0:2, :].set(n.reshape(b, 2, LANES))
    slab = slab.at[:, 2, 0:N_HEADS_C].set(m)
    return c_pairs, slab


def _unpack_mlstm_state(c_pairs, slab):
    b = c_pairs.shape[0]
    Cp = c_pairs.reshape(b, 2, 2, HEAD_DIM, 2, HEAD_DIM)
    C = jnp.stack([Cp[:, :, 0, :, 0, :], Cp[:, :, 1, :, 1, :]], axis=2)
    C = C.reshape(b, N_HEADS_C, HEAD_DIM, HEAD_DIM)
    n = slab[:, 0:2, :].reshape(b, N_HEADS_C, HEAD_DIM)
    m = slab[:, 2, 0:N_HEADS_C]
    return C, n, m


def _block_diag(w):
    nb, bs, _ = w.shape
    eye = jnp.eye(nb, dtype=w.dtype)
    return jnp.einsum('ncd,nm->ncmd', w, eye).reshape(nb * bs, nb * bs)


def _prep_layer(w):
    row = lambda v: v.reshape(1, -1)
    w_in = jnp.pad(w['w_in'], ((0, 0), (0, Z_COLS - w['w_in'].shape[1]))).astype(BF16)
    bias_gate = jnp.concatenate([w['b_mlstm_i'], w['b_mlstm_f']])
    return dict(
        f1=(row(w['g_f1_pre']), w['w_f1_gate'].astype(BF16), w['w_f1_up'].astype(BF16),
            w['w_f1_down'].astype(BF16), row(w['g_f1_post'])),
        f2=(row(w['g_f2_pre']), w['w_f2_gate'].astype(BF16), w['w_f2_up'].astype(BF16),
            w['w_f2_down'].astype(BF16), row(w['g_f2_post'])),
        inproj=(row(w['g_mix_pre']), w_in),
        rglru=(w['conv_w'], row(w['conv_b']), _block_diag(w['w_rg_a']).astype(BF16), row(w['b_rg_a']),
               _block_diag(w['w_rg_x']).astype(BF16), row(w['b_rg_x']), row(w['lru_lambda']),
               row(w['g_out_b'])),
        mlstm=(jnp.pad(bias_gate, (0, LANES - 2 * N_HEADS_C)).reshape(1, LANES),
               bias_gate.reshape(2 * N_HEADS_C, 1), row(w['g_out_c'])),
        outproj=(row(w['g_out_a']), w['w_out'][:D_A].astype(BF16),
                 w['w_out'][D_A:D_A + D_B].astype(BF16), w['w_out'][D_A + D_B:].astype(BF16),
                 row(w['g_mix_post'])),
        ple=(row(w['g_ple_pre']), w['w_ple_gate'].astype(BF16), w['w_ple_proj'].astype(BF16),
             row(w['g_ple_post'])),
    )


def _gate_rows(z3, col0):
    lo = (ZC_GATE - col0) * LANES
    return jnp.swapaxes(z3[:, :, lo:lo + 2 * N_HEADS_C], 1, 2)


def _layer(x, pe, lw, cache):
    b, t_len, d = x.shape
    rows = x.reshape(b * t_len, d)
    rows = _ffn(rows, *lw['f1'])
    z3 = _inproj(rows, *lw['inproj']).reshape(b, t_len, Z_COLS)
    ka = z3[:, :, ZC_KA * LANES:ZC_KA * LANES + D_A]
    va = z3[:, :, ZC_VA * LANES:ZC_VA * LANES + D_A]
    if cache is None:
        keep = min(DILATED_CONFIGS[-1][0], t_len)
        new_k = ka[:, t_len - keep:].reshape(b, keep, N_HEADS_A, HEAD_DIM)
        new_v = va[:, t_len - keep:].reshape(b, keep, N_HEADS_A, HEAD_DIM)
        ya = _attn_prompt(z3)
        conv_buf = jnp.zeros((b, CONV_W - 1, D_B), F32)
        h0 = jnp.zeros((b, D_B), F32)
        c0 = jnp.zeros((b, 2, LANES, LANES), F32)
        s0 = jnp.zeros((b, 8, LANES), F32)
        zc, col0, n_valid = z3, 0, MLSTM_CHUNK
    else:
        kt_all, vt_all, layer, conv_buf, h0, C0, n0, m0 = cache
        new_k = ka.reshape(b, t_len, N_HEADS_A, HEAD_DIM)
        new_v = va.reshape(b, t_len, N_HEADS_A, HEAD_DIM)
        ya = _attn_sample(z3, kt_all, vt_all, layer)
        c0, s0 = _pack_mlstm_state(C0, n0, m0)
        col0 = ZC_QC
        zc = jnp.pad(z3[:, :, col0 * LANES:], ((0, 0), (0, MLSTM_CHUNK - t_len), (0, 0)))
        n_valid = t_len
    yb, new_buf, h_last = _rglru(z3, conv_buf, h0, *lw['rglru'])
    yc, c1, s1 = _mlstm(zc, col0, _gate_rows(zc, col0), n_valid, *lw['mlstm'], c0, s0)
    C1, n1, m1 = _unpack_mlstm_state(c1, s1)
    yc = yc[:, :t_len]
    rows = _outproj(rows, ya.reshape(b * t_len, D_A), yb.reshape(b * t_len, D_B),
                    yc.reshape(b * t_len, D_C), *lw['outproj'])
    rows = _ffn(rows, *lw['f2'])
    rows = _ple(rows, pe.reshape(b * t_len, -1), *lw['ple'])
    return rows.reshape(b, t_len, d), (new_k, new_v, new_buf, h_last, C1, n1, m1)


def _feature_major(cache):
    depth, b, w_buf, nh, dh = cache.shape
    return jnp.transpose(cache, (0, 1, 3, 4, 2)).reshape(depth, b, nh * dh, w_buf)


def kernel(x_prompt, x_sample, cache_k, cache_v, state_conv, state_h, state_C, state_n, state_m, p_prompt, p_sample, g_f1_pre, w_f1_gate, w_f1_up, w_f1_down, g_f1_post, g_mix_pre, w_in, conv_w, conv_b, w_rg_a, b_rg_a, w_rg_x, b_rg_x, lru_lambda, b_mlstm_i, b_mlstm_f, g_out_a, g_out_b, g_out_c, w_out, g_mix_post, g_f2_pre, w_f2_gate, w_f2_up, w_f2_down, g_f2_post, g_ple_pre, w_ple_gate, w_ple_proj, g_ple_post):
    depth = w_in.shape[0]
    xp, xs = x_prompt, x_sample
    kt_all = _feature_major(cache_k)
    vt_all = _feature_major(cache_v)
    sp = [[] for _ in range(7)]
    ss = [[] for _ in range(7)]
    for i in range(depth):
        lw = _prep_layer(dict(
            g_f1_pre=g_f1_pre[i], w_f1_gate=w_f1_gate[i], w_f1_up=w_f1_up[i], w_f1_down=w_f1_down[i],
            g_f1_post=g_f1_post[i], g_mix_pre=g_mix_pre[i], w_in=w_in[i], conv_w=conv_w[i],
            conv_b=conv_b[i], w_rg_a=w_rg_a[i], b_rg_a=b_rg_a[i], w_rg_x=w_rg_x[i], b_rg_x=b_rg_x[i],
            lru_lambda=lru_lambda[i], b_mlstm_i=b_mlstm_i[i], b_mlstm_f=b_mlstm_f[i],
            g_out_a=g_out_a[i], g_out_b=g_out_b[i], g_out_c=g_out_c[i], w_out=w_out[i],
            g_mix_post=g_mix_post[i], g_f2_pre=g_f2_pre[i], w_f2_gate=w_f2_gate[i],
            w_f2_up=w_f2_up[i], w_f2_down=w_f2_down[i], g_f2_post=g_f2_post[i],
            g_ple_pre=g_ple_pre[i], w_ple_gate=w_ple_gate[i], w_ple_proj=w_ple_proj[i],
            g_ple_post=g_ple_post[i]))
        xp, st_p = _layer(xp, p_prompt[i], lw, None)
        cache_i = (kt_all, vt_all, i, state_conv[i], state_h[i], state_C[i], state_n[i], state_m[i])
        xs, st_s = _layer(xs, p_sample[i], lw, cache_i)
        for j in range(7):
            sp[j].append(st_p[j])
            ss[j].append(st_s[j])
    k_prompt, v_prompt, conv_prompt, h_prompt, C_prompt, n_prompt, m_prompt = [jnp.stack(a) for a in sp]
    k_sample, v_sample, conv_sample, h_sample, C_sample, n_sample, m_sample = [jnp.stack(a) for a in ss]
    return (xp, xs, k_prompt, v_prompt, k_sample, v_sample, conv_prompt, conv_sample,
            h_prompt, h_sample, C_prompt, C_sample, n_prompt, n_sample, m_prompt, m_sample)
```

```python
import functools

import jax
import jax.numpy as jnp
from jax import lax
from jax.experimental import pallas as pl
from jax.experimental.pallas import tpu as pltpu

F32 = jnp.float32
BF16 = jnp.bfloat16

EPS = 1e-6
NEG = -1e30
HEAD_DIM = 64
N_HEADS_A = 6
D_A = N_HEADS_A * HEAD_DIM
D_B = 384
N_HEADS_C = 4
D_C = N_HEADS_C * HEAD_DIM
DILATED_CONFIGS = ((128, 1), (512, 4), (2048, 16))
LOCAL_BLK = 128
CONV_W = 4
LRU_C = 8.0
MLSTM_CHUNK = 128
LANES = 128
SUBLANES = 8
Z_COLS = 3072
FF_CHUNK = 256
ROW_TILE = 512
VMEM_LIMIT = 56 * 1024 * 1024

ZC_QA, ZC_KA, ZC_VA, ZC_XB, ZC_GB = 0, 3, 6, 9, 12
ZC_QC, ZC_KC, ZC_VC, ZC_OC, ZC_GATE = 15, 17, 19, 21, 23


def _rms(x, g):
    return x * lax.rsqrt(jnp.mean(x * x, axis=-1, keepdims=True) + EPS) * g


def _dot(a, b):
    return jnp.dot(a, b, preferred_element_type=F32)


def _dot_nt(a, b):
    return lax.dot_general(a, b, (((1,), (1,)), ((), ())), preferred_element_type=F32)


def _dot_tn(a, b):
    return lax.dot_general(a, b, (((0,), (0,)), ((), ())), preferred_element_type=F32)


def _dot_exact(a, b):
    return jnp.dot(a, b, preferred_element_type=F32, precision=lax.Precision.HIGHEST)


def _softplus(x):
    return jnp.maximum(x, 0.0) + jnp.log1p(jnp.exp(-jnp.abs(x)))


def _gelu_tanh(x):
    return 0.5 * x * (1.0 + jnp.tanh(0.7978845608028654 * (x + 0.044715 * (x * x * x))))


def _const_spec(shape):
    nd = len(shape)
    return pl.BlockSpec(shape, lambda *_: (0,) * nd, pipeline_mode=pl.Buffered(1))


def _params(sem):
    return pltpu.CompilerParams(dimension_semantics=sem, vmem_limit_bytes=VMEM_LIMIT)


def _ffn_kernel(x_ref, gpre_ref, wg_ref, wu_ref, wd_ref, gpost_ref, o_ref, acc_ref):
    x = x_ref[...]
    h = _rms(x, gpre_ref[...]).astype(BF16)
    n_chunks = wg_ref.shape[1] // FF_CHUNK
    for c in range(n_chunks):
        sl = slice(c * FF_CHUNK, (c + 1) * FF_CHUNK)
        g = _dot(h, wg_ref[:, sl])
        u = _dot(h, wu_ref[:, sl])
        a = (g * jax.nn.sigmoid(g) * u).astype(BF16)
        d = _dot(a, wd_ref[sl, :])
        if c == 0:
            acc_ref[...] = d
        else:
            acc_ref[...] += d
    o_ref[...] = x + 0.5 * _rms(acc_ref[...], gpost_ref[...])


def _ffn(x, gpre, wg, wu, wd, gpost):
    m, d = x.shape
    ff = wg.shape[1]
    tm = min(ROW_TILE, m)
    row = pl.BlockSpec((tm, d), lambda i: (i, 0))
    return pl.pallas_call(
        _ffn_kernel,
        grid=(m // tm,),
        in_specs=[row, _const_spec((1, d)), _const_spec((d, ff)), _const_spec((d, ff)),
                  _const_spec((ff, d)), _const_spec((1, d))],
        out_specs=row,
        out_shape=jax.ShapeDtypeStruct((m, d), F32),
        scratch_shapes=[pltpu.VMEM((tm, d), F32)],
        compiler_params=_params(("parallel",)),
        name="ffn",
    )(x, gpre, wg, wu, wd, gpost)


def _inproj_kernel(x_ref, g_ref, w_ref, z_ref):
    h = _rms(x_ref[...], g_ref[...]).astype(BF16)
    for c in range(w_ref.shape[1] // 256):
        sl = slice(c * 256, (c + 1) * 256)
        z_ref[:, sl] = _dot(h, w_ref[:, sl])


def _inproj(x, g, w):
    m, d = x.shape
    n = w.shape[1]
    tm = min(ROW_TILE, m)
    return pl.pallas_call(
        _inproj_kernel,
        grid=(m // tm,),
        in_specs=[pl.BlockSpec((tm, d), lambda i: (i, 0)), _const_spec((1, d)), _const_spec((d, n))],
        out_specs=pl.BlockSpec((tm, n), lambda i: (i, 0)),
        out_shape=jax.ShapeDtypeStruct((m, n), F32),
        compiler_params=_params(("parallel",)),
        name="inproj",
    )(x, g, w)


def _outproj_kernel(x_ref, ya_ref, yb_ref, yc_ref, ga_ref, wa_ref, wb_ref, wc_ref, gpost_ref, o_ref):
    ya = _rms(ya_ref[...], ga_ref[...]).astype(BF16)
    y = _dot(ya, wa_ref[...])
    y = y + _dot(yb_ref[...].astype(BF16), wb_ref[...])
    y = y + _dot(yc_ref[...].astype(BF16), wc_ref[...])
    o_ref[...] = x_ref[...] + _rms(y, gpost_ref[...])


def _outproj(x, ya, yb, yc, ga, wa, wb, wc, gpost):
    m, d = x.shape
    tm = min(ROW_TILE, m)

    def row(width):
        return pl.BlockSpec((tm, width), lambda i: (i, 0))

    return pl.pallas_call(
        _outproj_kernel,
        grid=(m // tm,),
        in_specs=[row(d), row(D_A), row(D_B), row(D_C), _const_spec((1, D_A)),
                  _const_spec((D_A, d)), _const_spec((D_B, d)), _const_spec((D_C, d)),
                  _const_spec((1, d))],
        out_specs=row(d),
        out_shape=jax.ShapeDtypeStruct((m, d), F32),
        compiler_params=_params(("parallel",)),
        name="outproj",
    )(x, ya, yb, yc, ga, wa, wb, wc, gpost)


def _ple_kernel(x_ref, pe_ref, gpre_ref, wg_ref, wp_ref, gpost_ref, o_ref):
    x = x_ref[...]
    h = _rms(x, gpre_ref[...]).astype(BF16)
    gate = jax.nn.sigmoid(_dot(h, wg_ref[...]))
    proj = _dot(pe_ref[...].astype(BF16), wp_ref[...])
    o_ref[...] = x + _rms(gate * proj, gpost_ref[...])


def _ple(x, pe, gpre, wg, wp, gpost):
    m, d = x.shape
    dp = pe.shape[1]
    tm = min(ROW_TILE, m)
    return pl.pallas_call(
        _ple_kernel,
        grid=(m // tm,),
        in_specs=[pl.BlockSpec((tm, d), lambda i: (i, 0)), pl.BlockSpec((tm, dp), lambda i: (i, 0)),
                  _const_spec((1, d)), _const_spec((d, d)), _const_spec((dp, d)), _const_spec((1, d))],
        out_specs=pl.BlockSpec((tm, d), lambda i: (i, 0)),
        out_shape=jax.ShapeDtypeStruct((m, d), F32),
        compiler_params=_params(("parallel",)),
        name="ple",
    )(x, pe, gpre, wg, wp, gpost)


def _swiglu_into(acc_ref, h, wg_ref, wu_ref, wd_ref):
    for c in range(wg_ref.shape[1] // FF_CHUNK):
        sl = slice(c * FF_CHUNK, (c + 1) * FF_CHUNK)
        g = _dot(h, wg_ref[:, sl])
        u = _dot(h, wu_ref[:, sl])
        a = (g * jax.nn.sigmoid(g) * u).astype(BF16)
        d = _dot(a, wd_ref[sl, :])
        if c == 0:
            acc_ref[...] = d
        else:
            acc_ref[...] += d


def _postmix_kernel(x_ref, ya_ref, yb_ref, yc_ref, pe_ref, ga_ref, wa_ref, wb_ref, wc_ref, gmix_ref,
                    gpre_ref, wg_ref, wu_ref, wd_ref, gpost_ref,
                    gple_ref, wpg_ref, wpp_ref, gple_post_ref, o_ref, acc_ref, x_scr):
    ya = _rms(ya_ref[...], ga_ref[...]).astype(BF16)
    y = _dot(ya, wa_ref[...])
    y = y + _dot(yb_ref[...].astype(BF16), wb_ref[...])
    y = y + _dot(yc_ref[...].astype(BF16), wc_ref[...])
    x_scr[...] = x_ref[...] + _rms(y, gmix_ref[...])
    _swiglu_into(acc_ref, _rms(x_scr[...], gpre_ref[...]).astype(BF16), wg_ref, wu_ref, wd_ref)
    x_scr[...] = x_scr[...] + 0.5 * _rms(acc_ref[...], gpost_ref[...])
    h = _rms(x_scr[...], gple_ref[...]).astype(BF16)
    gate = jax.nn.sigmoid(_dot(h, wpg_ref[...]))
    proj = _dot(pe_ref[...].astype(BF16), wpp_ref[...])
    o_ref[...] = x_scr[...] + _rms(gate * proj, gple_post_ref[...])


def _postmix(x, ya, yb, yc, pe, outproj_w, ffn_w, ple_w):
    m, d = x.shape
    dp = pe.shape[1]
    ff = ffn_w[1].shape[1]
    tm = min(ROW_TILE, m)

    def row(width):
        return pl.BlockSpec((tm, width), lambda i: (i, 0))

    vec = _const_spec((1, d))
    return pl.pallas_call(
        _postmix_kernel,
        grid=(m // tm,),
        in_specs=[row(d), row(D_A), row(D_B), row(D_C), row(dp),
                  _const_spec((1, D_A)), _const_spec((D_A, d)), _const_spec((D_B, d)),
                  _const_spec((D_C, d)), vec,
                  vec, _const_spec((d, ff)), _const_spec((d, ff)), _const_spec((ff, d)), vec,
                  vec, _const_spec((d, d)), _const_spec((dp, d)), vec],
        out_specs=row(d),
        out_shape=jax.ShapeDtypeStruct((m, d), F32),
        scratch_shapes=[pltpu.VMEM((tm, d), F32), pltpu.VMEM((tm, d), F32)],
        compiler_params=_params(("parallel",)),
        name="postmix",
    )(x, ya, yb, yc, pe, *outproj_w, *ffn_w, *ple_w)


ATT_QT = 2048
ATT_UNROLL = 8


def _attn_prompt_kernel(q_ref, k_ref, v_ref, o_ref, num_scr, m_scr, den_scr):
    t0 = pl.program_id(2) * ATT_QT
    blk = LOCAL_BLK
    lo_head = lax.broadcasted_iota(jnp.int32, (blk, LANES), 1) < HEAD_DIM
    key_minus_query = lax.broadcasted_iota(jnp.int32, (2 * blk, LANES), 1) \
        - (lax.broadcasted_iota(jnp.int32, (2 * blk, LANES), 0) & (blk - 1))
    mask_cur2 = key_minus_query <= 0
    scale = HEAD_DIM ** -0.5

    for ci, (_, dil) in enumerate(DILATED_CONFIGS):
        span = blk * dil

        def scores(j, dil=dil, span=span):
            r = j % dil
            n = j // dil
            qs = r + span * n
            ks = t0 + qs
            first = ks < span
            ps = jnp.where(first, ks, ks - span)
            prev_floor = jnp.where(first, 2 * blk, 0)
            mask_prev = key_minus_query >= prev_floor
            q2 = q_ref[pl.ds(qs, blk, stride=dil), :] * scale
            kc = k_ref[pl.ds(ks, blk, stride=dil), :].astype(BF16)
            kp = k_ref[pl.ds(ps, blk, stride=dil), :].astype(BF16)
            q_both = jnp.concatenate([jnp.where(lo_head, q2, 0.0), jnp.where(lo_head, 0.0, q2)], axis=0)
            raw = _dot_nt(q_both.astype(BF16), jnp.concatenate([kp, kc], axis=0))
            return qs, ks, ps, mask_prev, raw

        def softmax(mask_prev, raw):
            sp = jnp.where(mask_prev, raw[:, :blk], NEG)
            sc = jnp.where(mask_cur2, raw[:, blk:], NEG)
            m = jnp.max(jnp.maximum(sc, sp), axis=-1, keepdims=True)
            pc = jnp.exp(sc - m)
            pp = jnp.exp(sp - m)
            den = jnp.sum(pc + pp, axis=-1, keepdims=True)
            return jnp.concatenate([pp.astype(BF16), pc.astype(BF16)], axis=1), m, den

        def weighted_values(qs, ks, ps, probs, ci=ci, dil=dil):
            vc = v_ref[pl.ds(ks, blk, stride=dil), :].astype(BF16)
            vp = v_ref[pl.ds(ps, blk, stride=dil), :].astype(BF16)
            p_both, m, den = probs
            num = _dot(p_both, jnp.concatenate([vp, vc], axis=0))
            dst = pl.ds(qs, blk, stride=dil)
            num_scr[ci, dst, :] = jnp.where(lo_head, num[:blk], num[blk:])
            m_scr[ci, dst, :] = jnp.where(lo_head, m[:blk], m[blk:])
            den_scr[ci, dst, :] = jnp.where(lo_head, den[:blk], den[blk:])

        def body(jj, carry, scores=scores, softmax=softmax, weighted_values=weighted_values):
            staged = [scores(jj * ATT_UNROLL + u) for u in range(ATT_UNROLL)]
            probs = [softmax(st[3], st[4]) for st in staged]
            for st, pr in zip(staged, probs):
                weighted_values(st[0], st[1], st[2], pr)
            return carry

        lax.fori_loop(0, ATT_QT // blk // ATT_UNROLL, body, 0)

    rows = 256

    def combine(i, carry):
        sl = pl.ds(pl.multiple_of(i * rows, rows), rows)
        ms = [m_scr[c, sl, :] for c in range(3)]
        mx = jnp.maximum(jnp.maximum(ms[0], ms[1]), ms[2])
        ws = [jnp.exp(mc - mx) for mc in ms]
        num = num_scr[0, sl, :] * ws[0] + num_scr[1, sl, :] * ws[1] + num_scr[2, sl, :] * ws[2]
        den = den_scr[0, sl, :] * ws[0] + den_scr[1, sl, :] * ws[1] + den_scr[2, sl, :] * ws[2]
        o_ref[sl, :] = num / den
        return carry

    lax.fori_loop(0, ATT_QT // rows, combine, 0)


def _attn_prompt(z3):
    b, s, _ = z3.shape
    n_pairs = D_A // LANES
    q_spec = pl.BlockSpec((None, ATT_QT, LANES), lambda i, p, t: (i, t, ZC_QA + p))
    k_spec = pl.BlockSpec((None, s, LANES), lambda i, p, t: (i, 0, ZC_KA + p))
    v_spec = pl.BlockSpec((None, s, LANES), lambda i, p, t: (i, 0, ZC_VA + p))
    return pl.pallas_call(
        _attn_prompt_kernel,
        grid=(b, n_pairs, s // ATT_QT),
        in_specs=[q_spec, k_spec, v_spec],
        out_specs=pl.BlockSpec((None, ATT_QT, LANES), lambda i, p, t: (i, t, p)),
        out_shape=jax.ShapeDtypeStruct((b, s, D_A), F32),
        scratch_shapes=[pltpu.VMEM((3, ATT_QT, LANES), F32)] * 3,
        compiler_params=_params(("parallel", "parallel", "arbitrary")),
        name="attn_prompt",
    )(z3, z3, z3)


def _attn_sample_kernel(q_ref, kn_ref, vn_ref, kt_ref, vt_ref, o_ref, q_scr, kn_scr, vn_scr):
    t_new = q_ref.shape[0]
    w_buf = kt_ref.shape[1]
    q_scr[...] = jnp.zeros(q_scr.shape, F32)
    kn_scr[...] = jnp.zeros(kn_scr.shape, F32)
    vn_scr[...] = jnp.zeros(vn_scr.shape, F32)
    q_scr[0:t_new, :] = q_ref[...] * (HEAD_DIM ** -0.5)
    kn_scr[0:t_new, :] = kn_ref[...]
    vn_scr[0:t_new, :] = vn_ref[...]

    rows = 2 * SUBLANES
    tq = lax.broadcasted_iota(jnp.int32, (rows, w_buf), 0) & (SUBLANES - 1)
    dist = w_buf + tq - lax.broadcasted_iota(jnp.int32, (rows, w_buf), 1)
    tqn = lax.broadcasted_iota(jnp.int32, (rows, LANES), 0) & (SUBLANES - 1)
    tn = lax.broadcasted_iota(jnp.int32, (rows, LANES), 1)
    dist_new = tqn - tn
    cache_ok, new_ok = [], []
    for window, dil in DILATED_CONFIGS:
        ok = dist <= window
        nk = jnp.logical_and(dist_new >= 0, tn < t_new)
        nk = jnp.logical_and(nk, dist_new <= window)
        if dil > 1:
            ok = jnp.logical_and(ok, (dist & (dil - 1)) == 0)
            nk = jnp.logical_and(nk, (dist_new & (dil - 1)) == 0)
        cache_ok.append(ok)
        new_ok.append(nk)
    lo8 = lax.broadcasted_iota(jnp.int32, (SUBLANES, LANES), 1) < HEAD_DIM

    for p in range(D_A // LANES):
        sl = slice(p * LANES, (p + 1) * LANES)
        q2 = q_scr[:, sl]
        qm = jnp.concatenate([jnp.where(lo8, q2, 0.0), jnp.where(lo8, 0.0, q2)], axis=0).astype(BF16)
        kb = kt_ref[sl, :].astype(BF16)
        vb = vt_ref[sl, :].astype(BF16)
        s = _dot(qm, kb)
        s_new = _dot_nt(qm, kn_scr[:, sl].astype(BF16))
        parts = []
        for c in range(len(DILATED_CONFIGS)):
            sc = jnp.where(cache_ok[c], s, NEG)
            sn = jnp.where(new_ok[c], s_new, NEG)
            m = jnp.maximum(jnp.max(sc, axis=1, keepdims=True), jnp.max(sn, axis=1, keepdims=True))
            pc = jnp.exp(sc - m)
            pn = jnp.exp(sn - m)
            den = jnp.sum(pc, axis=1, keepdims=True) + jnp.sum(pn, axis=1, keepdims=True)
            parts.append((pc, pn, m, den))
        mx = jnp.maximum(jnp.maximum(parts[0][2], parts[1][2]), parts[2][2])
        ws = [jnp.exp(pt[2] - mx) for pt in parts]
        den_all = parts[0][3] * ws[0] + parts[1][3] * ws[1] + parts[2][3] * ws[2]
        coefs = [w / den_all for w in ws]
        p_all = parts[0][0] * coefs[0] + parts[1][0] * coefs[1] + parts[2][0] * coefs[2]
        pn_all = parts[0][1] * coefs[0] + parts[1][1] * coefs[1] + parts[2][1] * coefs[2]
        o16 = _dot_nt(p_all.astype(BF16), vb) + _dot(pn_all.astype(BF16), vn_scr[:, sl].astype(BF16))
        o8 = jnp.where(lo8, o16[0:SUBLANES], o16[SUBLANES:rows])
        o_ref[:, sl] = o8[0:t_new]


def _attn_sample(zs3, kt_all, vt_all, layer):
    b, t_new, _ = zs3.shape
    w_buf = kt_all.shape[-1]

    def new_spec(col):
        return pl.BlockSpec((None, t_new, D_A), lambda i: (i, 0, col))

    cache_spec = pl.BlockSpec((None, None, D_A, w_buf), lambda i: (layer, i, 0, 0))
    return pl.pallas_call(
        _attn_sample_kernel,
        grid=(b,),
        in_specs=[new_spec(0), new_spec(1), new_spec(2), cache_spec, cache_spec],
        out_specs=pl.BlockSpec((None, t_new, D_A), lambda i: (i, 0, 0)),
        out_shape=jax.ShapeDtypeStruct((b, t_new, D_A), F32),
        scratch_shapes=[pltpu.VMEM((SUBLANES, D_A), F32), pltpu.VMEM((LANES, D_A), F32),
                        pltpu.VMEM((LANES, D_A), F32)],
        compiler_params=_params(("parallel",)),
        name="attn_sample",
    )(zs3, zs3, zs3, kt_all, vt_all)


RG_GROUP = SUBLANES
RG_CHUNK = 256
RG_PAD = 8


def _rg_pitch(tcp):
    tiles = tcp // 8 + 1
    return 8 * (tiles if tiles % 2 else tiles + 1)


def _rglru_kernel(xb_ref, gb_ref, conv_ref, h0_ref, cw_ref, cb_ref, wa_ref, ba_ref, wx_ref, bx_ref,
                  lam_ref, g_ref, y_ref, convnew_ref, ht_ref, xp_scr, hist_scr, h_scr, a_scr, b_scr):
    n_seq, tc, _ = xb_ref.shape
    tcp = xp_scr.shape[0] - RG_PAD
    pitch = a_scr.shape[1] // n_seq
    n_hist = CONV_W - 1
    hist = slice(RG_PAD - n_hist, RG_PAD)
    n_groups = D_B // LANES

    @pl.when(pl.program_id(1) == 0)
    def _():
        hist_scr[:, hist, :] = conv_ref[...]
        h_scr[...] = h0_ref[...]

    if tc % 8:
        xp_scr[RG_PAD:, :] = jnp.zeros((tcp, D_B), F32)
    decay = _softplus(-lam_ref[...])

    def gates(g, carry):
        xp_scr[hist, :] = hist_scr[g, hist, :]
        xp_scr[RG_PAD:RG_PAD + tc, :] = xb_ref[g]
        hist_scr[g, hist, :] = xp_scr[RG_PAD + tc - n_hist:RG_PAD + tc, :]
        xc = cb_ref[...]
        for j in range(CONV_W):
            lo = RG_PAD - n_hist + j
            xc = xc + xp_scr[lo:lo + tcp, :] * cw_ref[pl.ds(j, 1), :]
        xcb = xc.astype(BF16)
        r = jax.nn.sigmoid(_dot(xcb, wa_ref[...]) + ba_ref[...])
        gi = jax.nn.sigmoid(_dot(xcb, wx_ref[...]) + bx_ref[...])
        a = jnp.exp(-LRU_C * r * decay)
        bb = jnp.sqrt(1.0 - a * a) * (gi * xc)
        dst = pl.ds(pl.multiple_of(g * pitch, 8), tcp)
        for lg in range(n_groups):
            a_scr[lg, dst, :] = a[:, lg * LANES:(lg + 1) * LANES]
            b_scr[lg, dst, :] = bb[:, lg * LANES:(lg + 1) * LANES]
        return carry

    lax.fori_loop(0, n_seq, gates, 0)

    def step(t, hs):
        rows = pl.ds(t, n_seq, stride=pitch)
        out = []
        for lg in range(n_groups):
            h = a_scr[lg, rows, :] * hs[lg] + b_scr[lg, rows, :]
            b_scr[lg, rows, :] = h
            out.append(h)
        return tuple(out)

    hs = tuple(h_scr[:, lg * LANES:(lg + 1) * LANES] for lg in range(n_groups))
    hs = lax.fori_loop(0, tc, step, hs, unroll=min(8, tc))
    for lg in range(n_groups):
        h_scr[:, lg * LANES:(lg + 1) * LANES] = hs[lg]

    def finish(g, carry):
        src = pl.ds(pl.multiple_of(g * pitch, 8), tcp)
        hseq = jnp.concatenate([b_scr[lg, src, :] for lg in range(n_groups)], axis=1)
        y = _rms(_gelu_tanh(gb_ref[g]) * hseq[0:tc], g_ref[...])
        y_ref[g] = y
        return carry

    lax.fori_loop(0, n_seq, finish, 0)
    convnew_ref[...] = hist_scr[:, hist, :]
    ht_ref[...] = h_scr[...]


def _rglru(z3, conv_buf, h0, cw, cb, wa, ba, wx, bx, lam, g):
    b, t_len, _ = z3.shape
    tc = min(RG_CHUNK, t_len)
    tcp = -(-tc // 8) * 8
    pitch = _rg_pitch(tcp)
    n_hist = CONV_W - 1
    ng = b // RG_GROUP
    vec = _const_spec((1, D_B))
    seq = pl.BlockSpec((RG_GROUP, tc, D_B), lambda i, c: (i, c, 0))
    hist = pl.BlockSpec((RG_GROUP, n_hist, D_B), lambda i, c: (i, 0, 0))
    state = pl.BlockSpec((None, RG_GROUP, D_B), lambda i, c: (i, 0, 0))
    y, conv_new, h_last = pl.pallas_call(
        _rglru_kernel,
        grid=(ng, t_len // tc),
        in_specs=[pl.BlockSpec((RG_GROUP, tc, D_B), lambda i, c: (i, c, ZC_XB // 3)),
                  pl.BlockSpec((RG_GROUP, tc, D_B), lambda i, c: (i, c, ZC_GB // 3)),
                  hist, state,
                  _const_spec((CONV_W, D_B)), vec, _const_spec((D_B, D_B)), vec,
                  _const_spec((D_B, D_B)), vec, vec, vec],
        out_specs=[seq, hist, state],
        out_shape=[jax.ShapeDtypeStruct((b, t_len, D_B), F32),
                   jax.ShapeDtypeStruct((b, n_hist, D_B), F32),
                   jax.ShapeDtypeStruct((ng, RG_GROUP, D_B), F32)],
        scratch_shapes=[pltpu.VMEM((RG_PAD + tcp, D_B), F32),
                        pltpu.VMEM((RG_GROUP, RG_PAD, D_B), F32),
                        pltpu.VMEM((RG_GROUP, D_B), F32),
                        pltpu.VMEM((D_B // LANES, RG_GROUP * pitch, LANES), F32),
                        pltpu.VMEM((D_B // LANES, RG_GROUP * pitch, LANES), F32)],
        compiler_params=_params(("parallel", "arbitrary")),
        name="rglru",
    )(z3, z3, conv_buf, h0.reshape(ng, RG_GROUP, D_B), cw, cb, wa, ba, wx, bx, lam, g)
    return y, conv_new, h_last.reshape(b, D_B)


ML_GROUP = 4


def _log_sigmoid(x):
    return -_softplus(-x)


def _mlstm_kernel(n_valid, q0_ref, q1_ref, k0_ref, k1_ref, v0_ref, v1_ref, o0_ref, o1_ref,
                  gc_ref, gr_ref, bias_r_ref, bias_c_ref, gain_ref, c0_ref, s0_ref,
                  y_ref, c_out_ref, s_out_ref, c_scr, s_scr):
    ck = pl.program_id(1)
    n_seq, L, _ = gc_ref.shape

    @pl.when(ck == 0)
    def _():
        c_scr[...] = c0_ref[...]
        s_scr[...] = s0_ref[...]

    lane = lax.broadcasted_iota(jnp.int32, (L, LANES), 1)
    row = lax.broadcasted_iota(jnp.int32, (L, LANES), 0)
    lo_head = lane < HEAD_DIM
    lo_row = lax.broadcasted_iota(jnp.int32, (1, LANES), 1) < HEAD_DIM
    causal = lax.broadcasted_iota(jnp.int32, (L, L), 1) <= lax.broadcasted_iota(jnp.int32, (L, L), 0)
    tri = jnp.where(causal, 1.0, 0.0).astype(F32)
    tri_t = jnp.where(lax.broadcasted_iota(jnp.int32, (L, L), 0)
                      <= lax.broadcasted_iota(jnp.int32, (L, L), 1), 1.0, 0.0).astype(F32)
    blockdiag = (lax.broadcasted_iota(jnp.int32, (LANES, LANES), 0) < HEAD_DIM) == \
                (lax.broadcasted_iota(jnp.int32, (LANES, LANES), 1) < HEAD_DIM)
    row_lo = lax.broadcasted_iota(jnp.int32, (LANES, 1), 0) < HEAD_DIM
    sub = lax.broadcasted_iota(jnp.int32, (8, L), 0)
    tl = lax.broadcasted_iota(jnp.int32, (8, L), 1)
    lane_row = lax.broadcasted_iota(jnp.int32, (1, LANES), 1)

    refs = ((q0_ref, k0_ref, v0_ref, o0_ref), (q1_ref, k1_ref, v1_ref, o1_ref))

    def gate_sums(sq):
        xc = gc_ref[sq] + bias_r_ref[...]
        gt_c = jnp.where(lane < N_HEADS_C, xc, _log_sigmoid(xc))
        xr = gr_ref[sq] + bias_c_ref[...]
        gt_r = jnp.where(sub < N_HEADS_C, xr, _log_sigmoid(xr))
        if n_valid < L:
            gt_c = jnp.where(row < n_valid, gt_c, jnp.where(lane < N_HEADS_C, NEG, 0.0))
            gt_r = jnp.where(tl < n_valid, gt_r, jnp.where(sub < N_HEADS_C, NEG, 0.0))
        b_col = _dot_exact(tri, gt_c)
        b_row = _dot_exact(gt_r, tri_t)
        return gt_c, gt_r, b_col, b_row

    def score_stage(sq, p):
        q_ref, k_ref, v_ref, _ = refs[p]
        q2 = q_ref[sq]
        k2 = k_ref[sq] * (HEAD_DIM ** -0.5)
        v2b = v_ref[sq].astype(BF16)
        k2b = k2.astype(BF16)
        c_pair = c_scr[sq, p]
        n_pair = s_scr[sq, pl.ds(p, 1), :]
        qc = _dot(q2.astype(BF16), c_pair.astype(BF16))
        s_heads = [_dot_nt(jnp.where(lo_head if h == 0 else jnp.logical_not(lo_head), q2, 0.0)
                           .astype(BF16), k2b) for h in range(2)]
        return q2, k2, v2b, c_pair, n_pair, qc, s_heads

    def weight_stage(sq, p, gates, staged, m_row):
        gt_c, gt_r, b_col, b_row = gates
        q2, _, _, _, n_pair, _, s_heads = staged
        qn = q2 * n_pair
        out = []
        for h in range(2):
            gh = 2 * p + h
            hm = lo_head if h == 0 else jnp.logical_not(lo_head)
            b_c = b_col[:, N_HEADS_C + gh:N_HEADS_C + gh + 1]
            b_r = b_row[N_HEADS_C + gh:N_HEADS_C + gh + 1, :]
            ig_r = gt_r[gh:gh + 1, :]
            ig_c = gt_c[:, gh:gh + 1]
            m_prev = m_row[:, gh:gh + 1]
            dm = jnp.where(causal, b_c - b_r + ig_r, NEG)
            inter = b_c + m_prev
            mt = jnp.maximum(inter, jnp.max(dm, axis=1, keepdims=True))
            wi = jnp.exp(inter - mt)
            a = jnp.exp(dm - mt) * s_heads[h]
            den = jnp.sum(a, axis=1, keepdims=True) \
                + wi * jnp.sum(jnp.where(hm, qn, 0.0), axis=1, keepdims=True)
            scale = 1.0 / jnp.maximum(jnp.abs(den), jnp.exp(-mt))
            m_new = mt[L - 1:L, :]
            b_last = b_c[L - 1:L, :]
            ws = jnp.exp(b_last - b_c + ig_c - m_new)
            wc = jnp.exp(b_last + m_prev - m_new)
            out.append((a.astype(BF16), wi, scale, ws, wc, m_new))
        return out

    def value_stage(sq, p, staged, weights):
        _, _, _, o_ref = refs[p]
        _, k2, v2b, c_pair, n_pair, qc, _ = staged
        (a0, wi0, sc0, ws0, wc0, _), (a1, wi1, sc1, ws1, wc1, _) = weights
        h0 = (_dot(a0, v2b) + wi0 * qc) * sc0
        h1 = (_dot(a1, v2b) + wi1 * qc) * sc1
        hout2 = jnp.where(lo_head, h0, h1)
        kw = k2 * jnp.where(lo_head, ws0, ws1)
        upd = _dot_tn(kw.astype(BF16), v2b)
        c_scr[sq, p] = jnp.where(blockdiag, jnp.where(row_lo, wc0, wc1) * c_pair + upd, 0.0)
        s_scr[sq, pl.ds(p, 1), :] = jnp.where(lo_row, wc0, wc1) * n_pair \
            + jnp.sum(kw, axis=0, keepdims=True)
        hc = jax.nn.sigmoid(o_ref[sq]) * hout2
        hsq = hc * hc
        ms0 = jnp.sum(jnp.where(lo_head, hsq, 0.0), axis=1, keepdims=True)
        ms1 = jnp.sum(jnp.where(lo_head, 0.0, hsq), axis=1, keepdims=True)
        ms = jnp.where(lo_head, ms0, ms1) * (1.0 / HEAD_DIM)
        y_ref[sq, :, p * LANES:(p + 1) * LANES] = \
            hc * lax.rsqrt(ms + EPS) * gain_ref[:, p * LANES:(p + 1) * LANES]

    units = [(sq, p) for sq in range(n_seq) for p in range(2)]
    gates = [gate_sums(sq) for sq in range(n_seq)]
    staged = [score_stage(sq, p) for sq, p in units]
    m_rows = [s_scr[sq, pl.ds(2, 1), :] for sq in range(n_seq)]
    weights = [weight_stage(sq, p, gates[sq], st, m_rows[sq]) for (sq, p), st in zip(units, staged)]
    for (sq, p), st, wt in zip(units, staged, weights):
        value_stage(sq, p, st, wt)
    for sq in range(n_seq):
        m_row_new = m_rows[sq]
        for (usq, p), wt in zip(units, weights):
            if usq == sq:
                for h in range(2):
                    m_row_new = jnp.where(lane_row == 2 * p + h, wt[h][5], m_row_new)
        s_scr[sq, pl.ds(2, 1), :] = m_row_new

    @pl.when(ck == pl.num_programs(1) - 1)
    def _():
        c_out_ref[...] = c_scr[...]
        s_out_ref[...] = s_scr[...]


def _mlstm(z3, col0, g_rows, n_valid, bias_r, bias_c, gain, c0, s0):
    b, t_len, _ = z3.shape
    L = MLSTM_CHUNK
    nc = t_len // L
    G = ML_GROUP

    def col(cb):
        return pl.BlockSpec((G, L, LANES), lambda i, c: (i, c, cb - col0))

    state_c = pl.BlockSpec((G, 2, LANES, LANES), lambda i, c: (i, 0, 0, 0))
    state_s = pl.BlockSpec((G, 8, LANES), lambda i, c: (i, 0, 0))
    return pl.pallas_call(
        functools.partial(_mlstm_kernel, n_valid),
        grid=(b // G, nc),
        in_specs=[col(ZC_QC), col(ZC_QC + 1), col(ZC_KC), col(ZC_KC + 1), col(ZC_VC), col(ZC_VC + 1),
                  col(ZC_OC), col(ZC_OC + 1), col(ZC_GATE),
                  pl.BlockSpec((G, 8, L), lambda i, c: (i, 0, c)),
                  _const_spec((1, LANES)), _const_spec((8, 1)), _const_spec((1, D_C)),
                  state_c, state_s],
        out_specs=[pl.BlockSpec((G, L, D_C), lambda i, c: (i, c, 0)), state_c, state_s],
        out_shape=[jax.ShapeDtypeStruct((b, t_len, D_C), F32),
                   jax.ShapeDtypeStruct((b, 2, LANES, LANES), F32),
                   jax.ShapeDtypeStruct((b, 8, LANES), F32)],
        scratch_shapes=[pltpu.VMEM((G, 2, LANES, LANES), F32), pltpu.VMEM((G, 8, LANES), F32)],
        compiler_params=_params(("parallel", "arbitrary")),
        name="mlstm",
    )(*([z3] * 9), g_rows, bias_r, bias_c, gain, c0, s0)


def _pack_mlstm_state(C, n, m):
    b = C.shape[0]
    Cp = C.reshape(b, 2, 2, HEAD_DIM, HEAD_DIM)
    eye = jnp.eye(2, dtype=C.dtype)
    c_pairs = jnp.einsum('bphkv,hg->bphkgv', Cp, eye).reshape(b, 2, LANES, LANES)
    slab = jnp.zeros((b, 8, LANES), C.dtype)
    slab = slab.at[:, 0:2, :].set(n.reshape(b, 2, LANES))
    slab = slab.at[:, 2, 0:N_HEADS_C].set(m)
    return c_pairs, slab


def _unpack_mlstm_state(c_pairs, slab):
    b = c_pairs.shape[0]
    Cp = c_pairs.reshape(b, 2, 2, HEAD_DIM, 2, HEAD_DIM)
    C = jnp.stack([Cp[:, :, 0, :, 0, :], Cp[:, :, 1, :, 1, :]], axis=2)
    C = C.reshape(b, N_HEADS_C, HEAD_DIM, HEAD_DIM)
    n = slab[:, 0:2, :].reshape(b, N_HEADS_C, HEAD_DIM)
    m = slab[:, 2, 0:N_HEADS_C]
    return C, n, m


N_GATES = 2 * N_HEADS_C


def _split3(x, axis):
    hi = x.astype(BF16).astype(F32)
    r1 = x - hi
    mid = r1.astype(BF16).astype(F32)
    lo = (r1 - mid).astype(BF16).astype(F32)
    return jnp.concatenate([hi, mid, lo], axis=axis).astype(BF16)


def _sum3(x, axis):
    n = x.shape[axis] // 3
    if axis == 0:
        return x[0:n] + x[n:2 * n] + x[2 * n:3 * n]
    return x[:, 0:n] + x[:, n:2 * n] + x[:, 2 * n:3 * n]


def _mlstm_mxu_kernel(n_valid, q0_ref, q1_ref, k0_ref, k1_ref, v0_ref, v1_ref, o0_ref, o1_ref,
                      gc_ref, gr_ref, bias_r_ref, bias_c_ref, gain_ref, tri_ref, trit_ref, sel_ref,
                      c0_ref, n0_ref, m0_ref, y_ref, c_out_ref, n_out_ref, m_out_ref,
                      c_scr, n_scr, m_scr):
    ck = pl.program_id(1)
    n_seq, L, _ = gc_ref.shape

    @pl.when(ck == 0)
    def _():
        c_scr[...] = c0_ref[...]
        n_scr[...] = n0_ref[...]
        m_scr[...] = m0_ref[...]

    lane = lax.broadcasted_iota(jnp.int32, (L, LANES), 1)
    row = lax.broadcasted_iota(jnp.int32, (L, LANES), 0)
    lo_head = lane < HEAD_DIM
    causal = lane <= row
    row_lo = row < HEAD_DIM
    blockdiag = ((row ^ lane) & HEAD_DIM) == 0
    sub = lax.broadcasted_iota(jnp.int32, (SUBLANES, L), 0)
    tl = lax.broadcasted_iota(jnp.int32, (SUBLANES, L), 1)
    ones_b = jnp.ones((L, LANES), BF16)
    refs = ((q0_ref, k0_ref, v0_ref, o0_ref), (q1_ref, k1_ref, v1_ref, o1_ref))

    def gate_stage(sq):
        xc = gc_ref[sq] + bias_r_ref[...]
        gt_c = jnp.where(lane < N_HEADS_C, xc, _log_sigmoid(xc))
        xr = gr_ref[sq] + bias_c_ref[...]
        gt_r = jnp.where(sub < N_HEADS_C, xr, _log_sigmoid(xr))
        if n_valid < L:
            gt_c = jnp.where(row < n_valid, gt_c, jnp.where(lane < N_HEADS_C, NEG, 0.0))
            gt_r = jnp.where(tl < n_valid, gt_r, jnp.where(sub < N_HEADS_C, NEG, 0.0))
        b_col = _sum3(_dot(tri_ref[...], _split3(gt_c, 1)), 1)
        b_row = _sum3(_dot(_split3(gt_r, 0), trit_ref[...]), 0)
        cols = _dot(_split3(jnp.where(lane < N_HEADS_C, gt_c, b_col), 1), sel_ref[...])
        return gt_r, b_row, cols

    def score_stage(sq, p):
        q_ref, k_ref, v_ref, _ = refs[p]
        q2 = q_ref[sq]
        k2 = k_ref[sq] * (HEAD_DIM ** -0.5)
        v2b = v_ref[sq].astype(BF16)
        k2b = k2.astype(BF16)
        c_pair = c_scr[sq, p]
        n_pair = n_scr[sq, p]
        cn_b = jnp.concatenate([c_pair, n_pair], axis=1).astype(BF16)
        heads = []
        for h in range(2):
            qh = jnp.where(lo_head if h == 0 else jnp.logical_not(lo_head), q2, 0.0).astype(BF16)
            heads.append((_dot_nt(qh, k2b), _dot(qh, cn_b)))
        return k2, v2b, c_pair, n_pair, heads

    def weight_stage(sq, p, gates, staged):
        gt_r, b_row, cols = gates
        heads = staged[4]
        out = []
        for h in range(2):
            gh = 2 * p + h
            s_h, qx = heads[h]
            ig_c = cols[:, gh * LANES:(gh + 1) * LANES]
            b_c = cols[:, (N_HEADS_C + gh) * LANES:(N_HEADS_C + gh + 1) * LANES]
            b_r = b_row[N_HEADS_C + gh:N_HEADS_C + gh + 1, :]
            ig_r = gt_r[gh:gh + 1, :]
            m_prev = m_scr[sq, pl.ds(gh, 1), :]
            dm = jnp.where(causal, b_c - b_r + ig_r, NEG)
            inter = b_c + m_prev
            mt = jnp.maximum(inter, jnp.max(dm, axis=1, keepdims=True))
            wi = jnp.exp(inter - mt)
            a = jnp.exp(dm - mt) * s_h
            den = jnp.sum(a, axis=1, keepdims=True) + wi * qx[:, LANES:]
            inv = 1.0 / jnp.maximum(jnp.abs(den), jnp.exp(-mt))
            m_new = mt[L - 1:L, :]
            b_last = b_c[L - 1:L, :]
            ws = jnp.exp(b_last - b_c + ig_c - m_new)
            wc = jnp.exp(b_last + m_prev - m_new)
            out.append((a.astype(BF16), wi * qx[:, :LANES], inv, ws, wc, m_new))
        return out

    def value_stage(sq, p, staged, weights):
        o_ref = refs[p][3]
        k2, v2b, c_pair, n_pair, _ = staged
        (a0, qc0, inv0, ws0, wc0, mn0), (a1, qc1, inv1, ws1, wc1, mn1) = weights
        h0 = (_dot(a0, v2b) + qc0) * inv0
        h1 = (_dot(a1, v2b) + qc1) * inv1
        hout2 = jnp.where(lo_head, h0, h1)
        kw = k2 * jnp.where(lo_head, ws0, ws1)
        upd = _dot_tn(kw.astype(BF16), jnp.concatenate([v2b, ones_b], axis=1))
        keep = jnp.where(row_lo, wc0, wc1)
        c_scr[sq, p] = jnp.where(blockdiag, keep * c_pair + upd[:, :LANES], 0.0)
        n_scr[sq, p] = keep * n_pair + upd[:, LANES:]
        m_scr[sq, pl.ds(2 * p, 1), :] = mn0
        m_scr[sq, pl.ds(2 * p + 1, 1), :] = mn1
        hc = jax.nn.sigmoid(o_ref[sq]) * hout2
        hsq = hc * hc
        ms0 = jnp.sum(jnp.where(lo_head, hsq, 0.0), axis=1, keepdims=True)
        ms1 = jnp.sum(jnp.where(lo_head, 0.0, hsq), axis=1, keepdims=True)
        ms = jnp.where(lo_head, ms0, ms1) * (1.0 / HEAD_DIM)
        y_ref[sq, :, p * LANES:(p + 1) * LANES] = \
            hc * lax.rsqrt(ms + EPS) * gain_ref[:, p * LANES:(p + 1) * LANES]

    units = [(sq, p) for sq in range(n_seq) for p in range(2)]
    gates = [gate_stage(sq) for sq in range(n_seq)]
    staged = [score_stage(sq, p) for sq, p in units]
    weights = [weight_stage(sq, p, gates[sq], st) for (sq, p), st in zip(units, staged)]
    for (sq, p), st, wt in zip(units, staged, weights):
        value_stage(sq, p, st, wt)

    @pl.when(ck == pl.num_programs(1) - 1)
    def _():
        c_out_ref[...] = c_scr[...]
        n_out_ref[...] = n_scr[...]
        m_out_ref[...] = m_scr[...]


def _mlstm_consts():
    L = MLSTM_CHUNK
    t = jnp.arange(L)
    tri = (t[None, :] <= t[:, None]).astype(BF16)
    src = jnp.arange(3 * LANES) % LANES
    dst = jnp.arange(N_GATES * LANES) // LANES
    sel = (src[:, None] == dst[None, :]).astype(BF16)
    return tri, tri.T, sel


def _mlstm_mxu(z3, col0, g_rows, n_valid, bias_r, bias_c, gain, c0, n0, m0):
    b, t_len, _ = z3.shape
    L = MLSTM_CHUNK
    nc = t_len // L
    G = ML_GROUP
    tri, tri_t, sel = _mlstm_consts()

    def col(cb):
        return pl.BlockSpec((G, L, LANES), lambda i, c: (i, c, cb - col0))

    state_c = pl.BlockSpec((G, 2, LANES, LANES), lambda i, c: (i, 0, 0, 0))
    state_m = pl.BlockSpec((G, SUBLANES, LANES), lambda i, c: (i, 0, 0))
    return pl.pallas_call(
        functools.partial(_mlstm_mxu_kernel, n_valid),
        grid=(b // G, nc),
        in_specs=[col(ZC_QC), col(ZC_QC + 1), col(ZC_KC), col(ZC_KC + 1), col(ZC_VC), col(ZC_VC + 1),
                  col(ZC_OC), col(ZC_OC + 1), col(ZC_GATE),
                  pl.BlockSpec((G, SUBLANES, L), lambda i, c: (i, 0, c)),
                  _const_spec((1, LANES)), _const_spec((SUBLANES, 1)), _const_spec((1, D_C)),
                  _const_spec((L, L)), _const_spec((L, L)), _const_spec((3 * LANES, N_GATES * LANES)),
                  state_c, state_c, state_m],
        out_specs=[pl.BlockSpec((G, L, D_C), lambda i, c: (i, c, 0)), state_c, state_c, state_m],
        out_shape=[jax.ShapeDtypeStruct((b, t_len, D_C), F32),
                   jax.ShapeDtypeStruct((b, 2, LANES, LANES), F32),
                   jax.ShapeDtypeStruct((b, 2, LANES, LANES), F32),
                   jax.ShapeDtypeStruct((b, SUBLANES, LANES), F32)],
        scratch_shapes=[pltpu.VMEM((G, 2, LANES, LANES), F32), pltpu.VMEM((G, 2, LANES, LANES), F32),
                        pltpu.VMEM((G, SUBLANES, LANES), F32)],
        compiler_params=_params(("parallel", "arbitrary")),
        name="mlstm",
    )(*([z3] * 9), g_rows, bias_r, bias_c, gain, tri, tri_t, sel, c0, n0, m0)


def _pack_mlstm_state3(C, n, m):
    b = C.shape[0]
    Cp = C.reshape(b, 2, 2, HEAD_DIM, HEAD_DIM)
    eye = jnp.eye(2, dtype=C.dtype)
    c_pairs = jnp.einsum('bphkv,hg->bphkgv', Cp, eye).reshape(b, 2, LANES, LANES)
    n_cols = jnp.broadcast_to(n.reshape(b, 2, LANES, 1), (b, 2, LANES, LANES))
    m_rows = jnp.broadcast_to(jnp.pad(m, ((0, 0), (0, SUBLANES - N_HEADS_C)))[:, :, None],
                              (b, SUBLANES, LANES))
    return c_pairs, n_cols, m_rows


def _unpack_mlstm_state3(c_pairs, n_cols, m_rows):
    b = c_pairs.shape[0]
    Cp = c_pairs.reshape(b, 2, 2, HEAD_DIM, 2, HEAD_DIM)
    C = jnp.stack([Cp[:, :, 0, :, 0, :], Cp[:, :, 1, :, 1, :]], axis=2)
    C = C.reshape(b, N_HEADS_C, HEAD_DIM, HEAD_DIM)
    n = n_cols[:, :, :, 0].reshape(b, N_HEADS_C, HEAD_DIM)
    m = m_rows[:, 0:N_HEADS_C, 0]
    return C, n, m


def _block_diag(w):
    nb, bs, _ = w.shape
    eye = jnp.eye(nb, dtype=w.dtype)
    return jnp.einsum('ncd,nm->ncmd', w, eye).reshape(nb * bs, nb * bs)


def _prep_layer(w):
    row = lambda v: v.reshape(1, -1)
    w_in = jnp.pad(w['w_in'], ((0, 0), (0, Z_COLS - w['w_in'].shape[1]))).astype(BF16)
    bias_gate = jnp.concatenate([w['b_mlstm_i'], w['b_mlstm_f']])
    return dict(
        f1=(row(w['g_f1_pre']), w['w_f1_gate'].astype(BF16), w['w_f1_up'].astype(BF16),
            w['w_f1_down'].astype(BF16), row(w['g_f1_post'])),
        f2=(row(w['g_f2_pre']), w['w_f2_gate'].astype(BF16), w['w_f2_up'].astype(BF16),
            w['w_f2_down'].astype(BF16), row(w['g_f2_post'])),
        inproj=(row(w['g_mix_pre']), w_in),
        rglru=(w['conv_w'], row(w['conv_b']), _block_diag(w['w_rg_a']).astype(BF16), row(w['b_rg_a']),
               _block_diag(w['w_rg_x']).astype(BF16), row(w['b_rg_x']), row(w['lru_lambda']),
               row(w['g_out_b'])),
        mlstm=(jnp.pad(bias_gate, (0, LANES - 2 * N_HEADS_C)).reshape(1, LANES),
               bias_gate.reshape(2 * N_HEADS_C, 1), row(w['g_out_c'])),
        outproj=(row(w['g_out_a']), w['w_out'][:D_A].astype(BF16),
                 w['w_out'][D_A:D_A + D_B].astype(BF16), w['w_out'][D_A + D_B:].astype(BF16),
                 row(w['g_mix_post'])),
        ple=(row(w['g_ple_pre']), w['w_ple_gate'].astype(BF16), w['w_ple_proj'].astype(BF16),
             row(w['g_ple_post'])),
    )


def _gate_rows(z3, col0):
    lo = (ZC_GATE - col0) * LANES
    return jnp.swapaxes(z3[:, :, lo:lo + 2 * N_HEADS_C], 1, 2)


def _layer(x, pe, lw, cache):
    b, t_len, d = x.shape
    rows = x.reshape(b * t_len, d)
    rows = _ffn(rows, *lw['f1'])
    z3 = _inproj(rows, *lw['inproj']).reshape(b, t_len, Z_COLS)
    ka = z3[:, :, ZC_KA * LANES:ZC_KA * LANES + D_A]
    va = z3[:, :, ZC_VA * LANES:ZC_VA * LANES + D_A]
    if cache is None:
        keep = min(DILATED_CONFIGS[-1][0], t_len)
        new_k = ka[:, t_len - keep:].reshape(b, keep, N_HEADS_A, HEAD_DIM)
        new_v = va[:, t_len - keep:].reshape(b, keep, N_HEADS_A, HEAD_DIM)
        ya = _attn_prompt(z3)
        conv_buf = jnp.zeros((b, CONV_W - 1, D_B), F32)
        h0 = jnp.zeros((b, D_B), F32)
        c0 = n0 = jnp.zeros((b, 2, LANES, LANES), F32)
        m0 = jnp.zeros((b, SUBLANES, LANES), F32)
        zc, col0, n_valid = z3, 0, MLSTM_CHUNK
    else:
        kt_all, vt_all, layer, conv_buf, h0, C0, n0, m0 = cache
        new_k = ka.reshape(b, t_len, N_HEADS_A, HEAD_DIM)
        new_v = va.reshape(b, t_len, N_HEADS_A, HEAD_DIM)
        ya = _attn_sample(z3, kt_all, vt_all, layer)
        c0, n0, m0 = _pack_mlstm_state3(C0, n0, m0)
        col0 = ZC_QC
        zc = jnp.pad(z3[:, :, col0 * LANES:], ((0, 0), (0, MLSTM_CHUNK - t_len), (0, 0)))
        n_valid = t_len
    yb, new_buf, h_last = _rglru(z3, conv_buf, h0, *lw['rglru'])
    yc, c1, n1, m1 = _mlstm_mxu(zc, col0, _gate_rows(zc, col0), n_valid, *lw['mlstm'], c0, n0, m0)
    C1, n1, m1 = _unpack_mlstm_state3(c1, n1, m1)
    yc = yc[:, :t_len]
    rows = _postmix(rows, ya.reshape(b * t_len, D_A), yb.reshape(b * t_len, D_B),
                    yc.reshape(b * t_len, D_C), pe.reshape(b * t_len, -1),
                    lw['outproj'], lw['f2'], lw['ple'])
    return rows.reshape(b, t_len, d), (new_k, new_v, new_buf, h_last, C1, n1, m1)


def _feature_major(cache):
    depth, b, w_buf, nh, dh = cache.shape
    return jnp.transpose(cache, (0, 1, 3, 4, 2)).reshape(depth, b, nh * dh, w_buf)


def kernel(x_prompt, x_sample, cache_k, cache_v, state_conv, state_h, state_C, state_n, state_m, p_prompt, p_sample, g_f1_pre, w_f1_gate, w_f1_up, w_f1_down, g_f1_post, g_mix_pre, w_in, conv_w, conv_b, w_rg_a, b_rg_a, w_rg_x, b_rg_x, lru_lambda, b_mlstm_i, b_mlstm_f, g_out_a, g_out_b, g_out_c, w_out, g_mix_post, g_f2_pre, w_f2_gate, w_f2_up, w_f2_down, g_f2_post, g_ple_pre, w_ple_gate, w_ple_proj, g_ple_post):
    depth = w_in.shape[0]
    xp, xs = x_prompt, x_sample
    kt_all = _feature_major(cache_k)
    vt_all = _feature_major(cache_v)
    sp = [[] for _ in range(7)]
    ss = [[] for _ in range(7)]
    for i in range(depth):
        lw = _prep_layer(dict(
            g_f1_pre=g_f1_pre[i], w_f1_gate=w_f1_gate[i], w_f1_up=w_f1_up[i], w_f1_down=w_f1_down[i],
            g_f1_post=g_f1_post[i], g_mix_pre=g_mix_pre[i], w_in=w_in[i], conv_w=conv_w[i],
            conv_b=conv_b[i], w_rg_a=w_rg_a[i], b_rg_a=b_rg_a[i], w_rg_x=w_rg_x[i], b_rg_x=b_rg_x[i],
            lru_lambda=lru_lambda[i], b_mlstm_i=b_mlstm_i[i], b_mlstm_f=b_mlstm_f[i],
            g_out_a=g_out_a[i], g_out_b=g_out_b[i], g_out_c=g_out_c[i], w_out=w_out[i],
            g_mix_post=g_mix_post[i], g_f2_pre=g_f2_pre[i], w_f2_gate=w_f2_gate[i],
            w_f2_up=w_f2_up[i], w_f2_down=w_f2_down[i], g_f2_post=g_f2_post[i],
            g_ple_pre=g_ple_pre[i], w_ple_gate=w_ple_gate[i], w_ple_proj=w_ple_proj[i],
            g_ple_post=g_ple_post[i]))
        xp, st_p = _layer(xp, p_prompt[i], lw, None)
        cache_i = (kt_all, vt_all, i, state_conv[i], state_h[i], state_C[i], state_n[i], state_m[i])
        xs, st_s = _layer(xs, p_sample[i], lw, cache_i)
        for j in range(7):
            sp[j].append(st_p[j])
            ss[j].append(st_s[j])
    k_prompt, v_prompt, conv_prompt, h_prompt, C_prompt, n_prompt, m_prompt = [jnp.stack(a) for a in sp]
    k_sample, v_sample, conv_sample, h_sample, C_sample, n_sample, m_sample = [jnp.stack(a) for a in ss]
    return (xp, xs, k_prompt, v_prompt, k_sample, v_sample, conv_prompt, conv_sample,
            h_prompt, h_sample, C_prompt, C_sample, n_prompt, n_sample, m_prompt, m_sample)
```

```python
import functools

import jax
import jax.numpy as jnp
from jax import lax
from jax.experimental import pallas as pl
from jax.experimental.pallas import tpu as pltpu

F32 = jnp.float32
BF16 = jnp.bfloat16

EPS = 1e-6
NEG = -1e30
HEAD_DIM = 64
N_HEADS_A = 6
D_A = N_HEADS_A * HEAD_DIM
D_B = 384
N_HEADS_C = 4
D_C = N_HEADS_C * HEAD_DIM
N_GATES = 2 * N_HEADS_C
DILATED_CONFIGS = ((128, 1), (512, 4), (2048, 16))
LOCAL_BLK = 128
CONV_W = 4
LRU_C = 8.0
MLSTM_CHUNK = 128
LANES = 128
SUBLANES = 8
Z_COLS = 3072
FF_CHUNK = 256
ROW_TILE = 512
VMEM_LIMIT = 58 * 1024 * 1024

ZC_QA, ZC_KA, ZC_VA, ZC_XB, ZC_GB = 0, 3, 6, 9, 12
ZC_QC, ZC_KC, ZC_VC, ZC_OC, ZC_GATE = 15, 17, 19, 21, 23


def _rms(x, g):
    return x * lax.rsqrt(jnp.mean(x * x, axis=-1, keepdims=True) + EPS) * g


def _dot(a, b):
    return jnp.dot(a, b, preferred_element_type=F32)


def _dot_nt(a, b):
    return lax.dot_general(a, b, (((1,), (1,)), ((), ())), preferred_element_type=F32)


def _dot_tn(a, b):
    return lax.dot_general(a, b, (((0,), (0,)), ((), ())), preferred_element_type=F32)


def _softplus(x):
    return jnp.maximum(x, 0.0) + jnp.log1p(jnp.exp(-jnp.abs(x)))


def _log_sigmoid(x):
    return -_softplus(-x)


def _gelu_tanh(x):
    return 0.5 * x * (1.0 + jnp.tanh(0.7978845608028654 * (x + 0.044715 * (x * x * x))))


def _const_spec(shape):
    nd = len(shape)
    return pl.BlockSpec(shape, lambda *_: (0,) * nd, pipeline_mode=pl.Buffered(1))


def _layer_spec(shape, layer, block=None):
    nd = len(shape)
    idx = (0,) * nd if block is None else block
    return pl.BlockSpec((None,) + tuple(shape), lambda *_: (layer,) + idx,
                        pipeline_mode=pl.Buffered(1))


def _params(sem):
    return pltpu.CompilerParams(dimension_semantics=sem, vmem_limit_bytes=VMEM_LIMIT)


def _swiglu_into(acc_ref, h, wg_ref, wu_ref, wd_ref):
    for c in range(wg_ref.shape[1] // FF_CHUNK):
        sl = slice(c * FF_CHUNK, (c + 1) * FF_CHUNK)
        g = _dot(h, wg_ref[:, sl])
        u = _dot(h, wu_ref[:, sl])
        a = (g * jax.nn.sigmoid(g) * u).astype(BF16)
        d = _dot(a, wd_ref[sl, :])
        if c == 0:
            acc_ref[...] = d
        else:
            acc_ref[...] += d


def _premix_kernel(x_ref, gpre_ref, wg_ref, wu_ref, wd_ref, gpost_ref, gmix_ref, win_ref,
                   xo_ref, z_ref, acc_ref):
    x = x_ref[...]
    _swiglu_into(acc_ref, _rms(x, gpre_ref[...]).astype(BF16), wg_ref, wu_ref, wd_ref)
    x = x + 0.5 * _rms(acc_ref[...], gpost_ref[...])
    xo_ref[...] = x
    h = _rms(x, gmix_ref[...]).astype(BF16)
    for c in range(win_ref.shape[1] // 256):
        sl = slice(c * 256, (c + 1) * 256)
        z_ref[:, sl] = _dot(h, win_ref[:, sl])


def _premix(x, sw, layer):
    m, d = x.shape
    ff = sw['w_f1_gate'].shape[-1]
    tm = min(ROW_TILE, m)
    row = pl.BlockSpec((tm, d), lambda i: (i, 0))
    vec = _layer_spec((1, d), layer)
    return pl.pallas_call(
        _premix_kernel,
        grid=(m // tm,),
        in_specs=[row, vec, _layer_spec((d, ff), layer), _layer_spec((d, ff), layer),
                  _layer_spec((ff, d), layer), vec, vec, _layer_spec((d, Z_COLS), layer)],
        out_specs=[row, pl.BlockSpec((tm, Z_COLS), lambda i: (i, 0))],
        out_shape=[jax.ShapeDtypeStruct((m, d), F32), jax.ShapeDtypeStruct((m, Z_COLS), F32)],
        scratch_shapes=[pltpu.VMEM((tm, d), F32)],
        compiler_params=_params(("parallel",)),
        name="premix",
    )(x, sw['g_f1_pre'], sw['w_f1_gate'], sw['w_f1_up'], sw['w_f1_down'], sw['g_f1_post'],
      sw['g_mix_pre'], sw['w_in'])


def _postmix_kernel(x_ref, ya_ref, yb_ref, yc_ref, pe_ref, ga_ref, wa_ref, wb_ref, wc_ref, gmix_ref,
                    gpre_ref, wg_ref, wu_ref, wd_ref, gpost_ref,
                    gple_ref, wpg_ref, wpp_ref, gple_post_ref, o_ref, acc_ref, x_scr):
    ya = _rms(ya_ref[...], ga_ref[...]).astype(BF16)
    y = _dot(ya, wa_ref[...])
    y = y + _dot(yb_ref[...].astype(BF16), wb_ref[...])
    y = y + _dot(yc_ref[...].astype(BF16), wc_ref[...])
    x_scr[...] = x_ref[...] + _rms(y, gmix_ref[...])
    _swiglu_into(acc_ref, _rms(x_scr[...], gpre_ref[...]).astype(BF16), wg_ref, wu_ref, wd_ref)
    x_scr[...] = x_scr[...] + 0.5 * _rms(acc_ref[...], gpost_ref[...])
    h = _rms(x_scr[...], gple_ref[...]).astype(BF16)
    gate = jax.nn.sigmoid(_dot(h, wpg_ref[...]))
    proj = _dot(pe_ref[...].astype(BF16), wpp_ref[...])
    o_ref[...] = x_scr[...] + _rms(gate * proj, gple_post_ref[...])


def _postmix(x, ya, yb, yc, pe_all, sw, layer):
    m, d = x.shape
    dp = pe_all.shape[-1]
    ff = sw['w_f2_gate'].shape[-1]
    tm = min(ROW_TILE, m)

    def row(width):
        return pl.BlockSpec((tm, width), lambda i: (i, 0))

    vec = _layer_spec((1, d), layer)
    return pl.pallas_call(
        _postmix_kernel,
        grid=(m // tm,),
        in_specs=[row(d), row(D_A), row(D_B), row(D_C),
                  pl.BlockSpec((None, tm, dp), lambda i: (layer, i, 0)),
                  _layer_spec((1, D_A), layer),
                  _layer_spec((D_A, d), layer, (0, 0)),
                  _layer_spec((D_B, d), layer, (D_A // D_B, 0)),
                  _layer_spec((D_C, d), layer, ((D_A + D_B) // D_C, 0)), vec,
                  vec, _layer_spec((d, ff), layer), _layer_spec((d, ff), layer),
                  _layer_spec((ff, d), layer), vec,
                  vec, _layer_spec((d, d), layer), _layer_spec((dp, d), layer), vec],
        out_specs=row(d),
        out_shape=jax.ShapeDtypeStruct((m, d), F32),
        scratch_shapes=[pltpu.VMEM((tm, d), F32), pltpu.VMEM((tm, d), F32)],
        compiler_params=_params(("parallel",)),
        name="postmix",
    )(x, ya, yb, yc, pe_all, sw['g_out_a'], sw['w_out'], sw['w_out'], sw['w_out'], sw['g_mix_post'],
      sw['g_f2_pre'], sw['w_f2_gate'], sw['w_f2_up'], sw['w_f2_down'], sw['g_f2_post'],
      sw['g_ple_pre'], sw['w_ple_gate'], sw['w_ple_proj'], sw['g_ple_post'])


ATT_QT = 2048
ATT_UNROLL = 8


def _attn_prompt_kernel(q_ref, k_ref, v_ref, o_ref, num_scr, m_scr, den_scr):
    t0 = pl.program_id(2) * ATT_QT
    blk = LOCAL_BLK
    lo_head = lax.broadcasted_iota(jnp.int32, (blk, LANES), 1) < HEAD_DIM
    key_minus_query = lax.broadcasted_iota(jnp.int32, (2 * blk, LANES), 1) \
        - (lax.broadcasted_iota(jnp.int32, (2 * blk, LANES), 0) & (blk - 1))
    mask_cur2 = key_minus_query <= 0
    scale = HEAD_DIM ** -0.5

    for ci, (_, dil) in enumerate(DILATED_CONFIGS):
        span = blk * dil

        def scores(j, dil=dil, span=span):
            r = j % dil
            n = j // dil
            qs = r + span * n
            ks = t0 + qs
            first = ks < span
            ps = jnp.where(first, ks, ks - span)
            prev_floor = jnp.where(first, 2 * blk, 0)
            mask_prev = key_minus_query >= prev_floor
            q2 = q_ref[pl.ds(qs, blk, stride=dil), :] * scale
            kc = k_ref[pl.ds(ks, blk, stride=dil), :].astype(BF16)
            kp = k_ref[pl.ds(ps, blk, stride=dil), :].astype(BF16)
            q_both = jnp.concatenate([jnp.where(lo_head, q2, 0.0), jnp.where(lo_head, 0.0, q2)], axis=0)
            raw = _dot_nt(q_both.astype(BF16), jnp.concatenate([kp, kc], axis=0))
            return qs, ks, ps, mask_prev, raw

        def softmax(mask_prev, raw):
            sp = jnp.where(mask_prev, raw[:, :blk], NEG)
            sc = jnp.where(mask_cur2, raw[:, blk:], NEG)
            m = jnp.max(jnp.maximum(sc, sp), axis=-1, keepdims=True)
            pc = jnp.exp(sc - m)
            pp = jnp.exp(sp - m)
            den = jnp.sum(pc + pp, axis=-1, keepdims=True)
            return jnp.concatenate([pp.astype(BF16), pc.astype(BF16)], axis=1), m, den

        def weighted_values(qs, ks, ps, probs, ci=ci, dil=dil):
            vc = v_ref[pl.ds(ks, blk, stride=dil), :].astype(BF16)
            vp = v_ref[pl.ds(ps, blk, stride=dil), :].astype(BF16)
            p_both, m, den = probs
            num = _dot(p_both, jnp.concatenate([vp, vc], axis=0))
            dst = pl.ds(qs, blk, stride=dil)
            num_scr[ci, dst, :] = jnp.where(lo_head, num[:blk], num[blk:])
            m_scr[ci, dst, :] = jnp.where(lo_head, m[:blk], m[blk:])
            den_scr[ci, dst, :] = jnp.where(lo_head, den[:blk], den[blk:])

        def body(jj, carry, scores=scores, softmax=softmax, weighted_values=weighted_values):
            staged = [scores(jj * ATT_UNROLL + u) for u in range(ATT_UNROLL)]
            probs = [softmax(st[3], st[4]) for st in staged]
            for st, pr in zip(staged, probs):
                weighted_values(st[0], st[1], st[2], pr)
            return carry

        lax.fori_loop(0, ATT_QT // blk // ATT_UNROLL, body, 0)

    rows = 256

    def combine(i, carry):
        sl = pl.ds(pl.multiple_of(i * rows, rows), rows)
        ms = [m_scr[c, sl, :] for c in range(3)]
        mx = jnp.maximum(jnp.maximum(ms[0], ms[1]), ms[2])
        ws = [jnp.exp(mc - mx) for mc in ms]
        num = num_scr[0, sl, :] * ws[0] + num_scr[1, sl, :] * ws[1] + num_scr[2, sl, :] * ws[2]
        den = den_scr[0, sl, :] * ws[0] + den_scr[1, sl, :] * ws[1] + den_scr[2, sl, :] * ws[2]
        o_ref[sl, :] = num / den
        return carry

    lax.fori_loop(0, ATT_QT // rows, combine, 0)


def _attn_prompt(z3):
    b, s, _ = z3.shape
    n_pairs = D_A // LANES
    q_spec = pl.BlockSpec((None, ATT_QT, LANES), lambda i, p, t: (i, t, ZC_QA + p))
    k_spec = pl.BlockSpec((None, s, LANES), lambda i, p, t: (i, 0, ZC_KA + p))
    v_spec = pl.BlockSpec((None, s, LANES), lambda i, p, t: (i, 0, ZC_VA + p))
    return pl.pallas_call(
        _attn_prompt_kernel,
        grid=(b, n_pairs, s // ATT_QT),
        in_specs=[q_spec, k_spec, v_spec],
        out_specs=pl.BlockSpec((None, ATT_QT, LANES), lambda i, p, t: (i, t, p)),
        out_shape=jax.ShapeDtypeStruct((b, s, D_A), F32),
        scratch_shapes=[pltpu.VMEM((3, ATT_QT, LANES), F32)] * 3,
        compiler_params=_params(("parallel", "parallel", "arbitrary")),
        name="attn_prompt",
    )(z3, z3, z3)


def _attn_sample_kernel(q_ref, kn_ref, vn_ref, kt_ref, vt_ref, o_ref, q_scr, kn_scr, vn_scr):
    t_new = q_ref.shape[0]
    w_buf = kt_ref.shape[1]
    q_scr[...] = jnp.zeros(q_scr.shape, F32)
    kn_scr[...] = jnp.zeros(kn_scr.shape, F32)
    vn_scr[...] = jnp.zeros(vn_scr.shape, F32)
    q_scr[0:t_new, :] = q_ref[...] * (HEAD_DIM ** -0.5)
    kn_scr[0:t_new, :] = kn_ref[...]
    vn_scr[0:t_new, :] = vn_ref[...]

    rows = 2 * SUBLANES
    tq = lax.broadcasted_iota(jnp.int32, (rows, w_buf), 0) & (SUBLANES - 1)
    dist = w_buf + tq - lax.broadcasted_iota(jnp.int32, (rows, w_buf), 1)
    tqn = lax.broadcasted_iota(jnp.int32, (rows, LANES), 0) & (SUBLANES - 1)
    tn = lax.broadcasted_iota(jnp.int32, (rows, LANES), 1)
    dist_new = tqn - tn
    cache_ok, new_ok = [], []
    for window, dil in DILATED_CONFIGS:
        ok = dist <= window
        nk = jnp.logical_and(dist_new >= 0, tn < t_new)
        nk = jnp.logical_and(nk, dist_new <= window)
        if dil > 1:
            ok = jnp.logical_and(ok, (dist & (dil - 1)) == 0)
            nk = jnp.logical_and(nk, (dist_new & (dil - 1)) == 0)
        cache_ok.append(ok)
        new_ok.append(nk)
    lo8 = lax.broadcasted_iota(jnp.int32, (SUBLANES, LANES), 1) < HEAD_DIM

    for p in range(D_A // LANES):
        sl = slice(p * LANES, (p + 1) * LANES)
        q2 = q_scr[:, sl]
        qm = jnp.concatenate([jnp.where(lo8, q2, 0.0), jnp.where(lo8, 0.0, q2)], axis=0).astype(BF16)
        kb = kt_ref[sl, :].astype(BF16)
        vb = vt_ref[sl, :].astype(BF16)
        s = _dot(qm, kb)
        s_new = _dot_nt(qm, kn_scr[:, sl].astype(BF16))
        parts = []
        for c in range(len(DILATED_CONFIGS)):
            sc = jnp.where(cache_ok[c], s, NEG)
            sn = jnp.where(new_ok[c], s_new, NEG)
            m = jnp.maximum(jnp.max(sc, axis=1, keepdims=True), jnp.max(sn, axis=1, keepdims=True))
            pc = jnp.exp(sc - m)
            pn = jnp.exp(sn - m)
            den = jnp.sum(pc, axis=1, keepdims=True) + jnp.sum(pn, axis=1, keepdims=True)
            parts.append((pc, pn, m, den))
        mx = jnp.maximum(jnp.maximum(parts[0][2], parts[1][2]), parts[2][2])
        ws = [jnp.exp(pt[2] - mx) for pt in parts]
        den_all = parts[0][3] * ws[0] + parts[1][3] * ws[1] + parts[2][3] * ws[2]
        coefs = [w / den_all for w in ws]
        p_all = parts[0][0] * coefs[0] + parts[1][0] * coefs[1] + parts[2][0] * coefs[2]
        pn_all = parts[0][1] * coefs[0] + parts[1][1] * coefs[1] + parts[2][1] * coefs[2]
        o16 = _dot_nt(p_all.astype(BF16), vb) + _dot(pn_all.astype(BF16), vn_scr[:, sl].astype(BF16))
        o8 = jnp.where(lo8, o16[0:SUBLANES], o16[SUBLANES:rows])
        o_ref[:, sl] = o8[0:t_new]


def _attn_sample(zs3, kt_all, vt_all, layer):
    b, t_new, _ = zs3.shape
    w_buf = kt_all.shape[-1]

    def new_spec(col):
        return pl.BlockSpec((None, t_new, D_A), lambda i: (i, 0, col))

    cache_spec = pl.BlockSpec((None, None, D_A, w_buf), lambda i: (layer, i, 0, 0))
    return pl.pallas_call(
        _attn_sample_kernel,
        grid=(b,),
        in_specs=[new_spec(0), new_spec(1), new_spec(2), cache_spec, cache_spec],
        out_specs=pl.BlockSpec((None, t_new, D_A), lambda i: (i, 0, 0)),
        out_shape=jax.ShapeDtypeStruct((b, t_new, D_A), F32),
        scratch_shapes=[pltpu.VMEM((SUBLANES, D_A), F32), pltpu.VMEM((LANES, D_A), F32),
                        pltpu.VMEM((LANES, D_A), F32)],
        compiler_params=_params(("parallel",)),
        name="attn_sample",
    )(zs3, zs3, zs3, kt_all, vt_all)


RG_GROUP = SUBLANES
RG_CHUNK = 256
RG_PAD = 8


def _rg_pitch(tcp):
    tiles = tcp // 8 + 1
    return 8 * (tiles if tiles % 2 else tiles + 1)


def _rglru_kernel(xb_ref, gb_ref, conv_ref, h0_ref, cw_ref, cb_ref, wa_ref, ba_ref, wx_ref, bx_ref,
                  lam_ref, g_ref, y_ref, convnew_ref, ht_ref, xp_scr, hist_scr, h_scr, a_scr, b_scr):
    n_seq, tc, _ = xb_ref.shape
    tcp = xp_scr.shape[0] - RG_PAD
    pitch = a_scr.shape[1] // n_seq
    n_hist = CONV_W - 1
    hist = slice(RG_PAD - n_hist, RG_PAD)
    n_groups = D_B // LANES

    @pl.when(pl.program_id(1) == 0)
    def _():
        hist_scr[:, hist, :] = conv_ref[...]
        h_scr[...] = h0_ref[...]

    if tc % 8:
        xp_scr[RG_PAD:, :] = jnp.zeros((tcp, D_B), F32)
    decay = _softplus(-lam_ref[...])

    def gates(g, carry):
        xp_scr[hist, :] = hist_scr[g, hist, :]
        xp_scr[RG_PAD:RG_PAD + tc, :] = xb_ref[g]
        hist_scr[g, hist, :] = xp_scr[RG_PAD + tc - n_hist:RG_PAD + tc, :]
        xc = cb_ref[...]
        for j in range(CONV_W):
            lo = RG_PAD - n_hist + j
            xc = xc + xp_scr[lo:lo + tcp, :] * cw_ref[pl.ds(j, 1), :]
        xcb = xc.astype(BF16)
        r = jax.nn.sigmoid(_dot(xcb, wa_ref[...]) + ba_ref[...])
        gi = jax.nn.sigmoid(_dot(xcb, wx_ref[...]) + bx_ref[...])
        a = jnp.exp(-LRU_C * r * decay)
        bb = jnp.sqrt(1.0 - a * a) * (gi * xc)
        dst = pl.ds(pl.multiple_of(g * pitch, 8), tcp)
        for lg in range(n_groups):
            a_scr[lg, dst, :] = a[:, lg * LANES:(lg + 1) * LANES]
            b_scr[lg, dst, :] = bb[:, lg * LANES:(lg + 1) * LANES]
        return carry

    lax.fori_loop(0, n_seq, gates, 0)

    def step(t, hs):
        rows = pl.ds(t, n_seq, stride=pitch)
        out = []
        for lg in range(n_groups):
            h = a_scr[lg, rows, :] * hs[lg] + b_scr[lg, rows, :]
            b_scr[lg, rows, :] = h
            out.append(h)
        return tuple(out)

    hs = tuple(h_scr[:, lg * LANES:(lg + 1) * LANES] for lg in range(n_groups))
    hs = lax.fori_loop(0, tc, step, hs, unroll=min(8, tc))
    for lg in range(n_groups):
        h_scr[:, lg * LANES:(lg + 1) * LANES] = hs[lg]

    def finish(g, carry):
        src = pl.ds(pl.multiple_of(g * pitch, 8), tcp)
        hseq = jnp.concatenate([b_scr[lg, src, :] for lg in range(n_groups)], axis=1)
        y = _rms(_gelu_tanh(gb_ref[g]) * hseq[0:tc], g_ref[...])
        y_ref[g] = y
        return carry

    lax.fori_loop(0, n_seq, finish, 0)
    convnew_ref[...] = hist_scr[:, hist, :]
    ht_ref[...] = h_scr[...]


def _rglru(z3, conv_buf, h0, sw, layer):
    b, t_len, _ = z3.shape
    tc = min(RG_CHUNK, t_len)
    tcp = -(-tc // 8) * 8
    pitch = _rg_pitch(tcp)
    n_hist = CONV_W - 1
    ng = b // RG_GROUP
    vec = _layer_spec((1, D_B), layer)
    mat = _layer_spec((D_B, D_B), layer)
    seq = pl.BlockSpec((RG_GROUP, tc, D_B), lambda i, c: (i, c, 0))
    hist = pl.BlockSpec((RG_GROUP, n_hist, D_B), lambda i, c: (i, 0, 0))
    state = pl.BlockSpec((None, RG_GROUP, D_B), lambda i, c: (i, 0, 0))
    y, conv_new, h_last = pl.pallas_call(
        _rglru_kernel,
        grid=(ng, t_len // tc),
        in_specs=[pl.BlockSpec((RG_GROUP, tc, D_B), lambda i, c: (i, c, ZC_XB // 3)),
                  pl.BlockSpec((RG_GROUP, tc, D_B), lambda i, c: (i, c, ZC_GB // 3)),
                  hist, state,
                  _layer_spec((CONV_W, D_B), layer), vec, mat, vec, mat, vec, vec, vec],
        out_specs=[seq, hist, state],
        out_shape=[jax.ShapeDtypeStruct((b, t_len, D_B), F32),
                   jax.ShapeDtypeStruct((b, n_hist, D_B), F32),
                   jax.ShapeDtypeStruct((ng, RG_GROUP, D_B), F32)],
        scratch_shapes=[pltpu.VMEM((RG_PAD + tcp, D_B), F32),
                        pltpu.VMEM((RG_GROUP, RG_PAD, D_B), F32),
                        pltpu.VMEM((RG_GROUP, D_B), F32),
                        pltpu.VMEM((D_B // LANES, RG_GROUP * pitch, LANES), F32),
                        pltpu.VMEM((D_B // LANES, RG_GROUP * pitch, LANES), F32)],
        compiler_params=_params(("parallel", "arbitrary")),
        name="rglru",
    )(z3, z3, conv_buf, h0.reshape(ng, RG_GROUP, D_B), sw['conv_w'], sw['conv_b'], sw['w_rg_a'],
      sw['b_rg_a'], sw['w_rg_x'], sw['b_rg_x'], sw['lru_lambda'], sw['g_out_b'])
    return y, conv_new, h_last.reshape(b, D_B)


ML_GROUP = 4


def _split3(x, axis):
    hi = x.astype(BF16).astype(F32)
    r1 = x - hi
    mid = r1.astype(BF16).astype(F32)
    lo = (r1 - mid).astype(BF16).astype(F32)
    return jnp.concatenate([hi, mid, lo], axis=axis).astype(BF16)


def _sum3(x, axis):
    n = x.shape[axis] // 3
    if axis == 0:
        return x[0:n] + x[n:2 * n] + x[2 * n:3 * n]
    return x[:, 0:n] + x[:, n:2 * n] + x[:, 2 * n:3 * n]


def _mlstm_kernel(n_valid, q0_ref, q1_ref, k0_ref, k1_ref, v0_ref, v1_ref, o0_ref, o1_ref,
                  gr_ref, bias_ref, gain_ref, tri_ref, trit_ref, sel_ref,
                  c0_ref, n0_ref, m0_ref, y_ref, c_out_ref, n_out_ref, m_out_ref,
                  c_scr, n_scr, m_scr):
    ck = pl.program_id(1)
    n_seq, _, L = gr_ref.shape

    lane = lax.broadcasted_iota(jnp.int32, (L, LANES), 1)
    row = lax.broadcasted_iota(jnp.int32, (L, LANES), 0)
    lo_head = lane < HEAD_DIM
    causal = lane <= row
    row_lo = row < HEAD_DIM
    eye = lane == row
    blockdiag = ((row ^ lane) & HEAD_DIM) == 0
    sub = lax.broadcasted_iota(jnp.int32, (SUBLANES, L), 0)
    tl = lax.broadcasted_iota(jnp.int32, (SUBLANES, L), 1)
    ones_b = jnp.ones((L, LANES), BF16)
    refs = ((q0_ref, k0_ref, v0_ref, o0_ref), (q1_ref, k1_ref, v1_ref, o1_ref))
    units = [(sq, p) for sq in range(n_seq) for p in range(2)]

    @pl.when(ck == 0)
    def _():
        zero = jnp.zeros((HEAD_DIM, HEAD_DIM), F32)
        ones3 = jnp.ones((3 * LANES, LANES), BF16)
        for sq, p in units:
            top = jnp.concatenate([c0_ref[sq, 2 * p], zero], axis=1)
            bot = jnp.concatenate([zero, c0_ref[sq, 2 * p + 1]], axis=1)
            c_scr[sq, p] = jnp.concatenate([top, bot], axis=0)
            n_row = jnp.concatenate([n0_ref[sq, pl.ds(2 * p, 1), :],
                                     n0_ref[sq, pl.ds(2 * p + 1, 1), :]], axis=1)
            n_scr[sq, p] = _dot(_split3(jnp.where(eye, n_row, 0.0), 1), ones3)
        m_scr[...] = m0_ref[...]

    def gate_stage(sq):
        xr = gr_ref[sq] + bias_ref[...]
        gt_r = jnp.where(sub < N_HEADS_C, xr, _log_sigmoid(xr))
        if n_valid < L:
            gt_r = jnp.where(tl < n_valid, gt_r, jnp.where(sub < N_HEADS_C, NEG, 0.0))
        gt_c = jnp.transpose(jnp.concatenate([gt_r, jnp.zeros((L - SUBLANES, L), F32)], axis=0))
        b_col = _sum3(_dot(tri_ref[...], _split3(gt_c, 1)), 1)
        b_row = _sum3(_dot(_split3(gt_r, 0), trit_ref[...]), 0)
        cols = _dot(_split3(jnp.where(lane < N_HEADS_C, gt_c, b_col), 1), sel_ref[...])
        return gt_r, b_row, cols

    def score_stage(sq, p):
        q_ref, k_ref, v_ref, _ = refs[p]
        q2 = q_ref[sq]
        k2 = k_ref[sq] * (HEAD_DIM ** -0.5)
        v2b = v_ref[sq].astype(BF16)
        k2b = k2.astype(BF16)
        c_pair = c_scr[sq, p]
        n_pair = n_scr[sq, p]
        cn_b = jnp.concatenate([c_pair, n_pair], axis=1).astype(BF16)
        heads = []
        for h in range(2):
            qh = jnp.where(lo_head if h == 0 else jnp.logical_not(lo_head), q2, 0.0).astype(BF16)
            heads.append((_dot_nt(qh, k2b), _dot(qh, cn_b)))
        return k2, v2b, c_pair, n_pair, heads

    def weight_stage(sq, p, gates, staged):
        gt_r, b_row, cols = gates
        heads = staged[4]
        out = []
        for h in range(2):
            gh = 2 * p + h
            s_h, qx = heads[h]
            ig_c = cols[:, gh * LANES:(gh + 1) * LANES]
            b_c = cols[:, (N_HEADS_C + gh) * LANES:(N_HEADS_C + gh + 1) * LANES]
            b_r = b_row[N_HEADS_C + gh:N_HEADS_C + gh + 1, :]
            ig_r = gt_r[gh:gh + 1, :]
            m_prev = m_scr[sq, pl.ds(gh, 1), :]
            dm = jnp.where(causal, b_c - b_r + ig_r, NEG)
            inter = b_c + m_prev
            mt = jnp.maximum(inter, jnp.max(dm, axis=1, keepdims=True))
            wi = jnp.exp(inter - mt)
            a = jnp.exp(dm - mt) * s_h
            den = jnp.sum(a, axis=1, keepdims=True) + wi * qx[:, LANES:]
            inv = 1.0 / jnp.maximum(jnp.abs(den), jnp.exp(-mt))
            m_new = mt[L - 1:L, :]
            b_last = b_c[L - 1:L, :]
            ws = jnp.exp(b_last - b_c + ig_c - m_new)
            wc = jnp.exp(b_last + m_prev - m_new)
            out.append((a.astype(BF16), wi * qx[:, :LANES], inv, ws, wc, m_new))
        return out

    def value_stage(sq, p, staged, weights):
        o_ref = refs[p][3]
        k2, v2b, c_pair, n_pair, _ = staged
        (a0, qc0, inv0, ws0, wc0, mn0), (a1, qc1, inv1, ws1, wc1, mn1) = weights
        h0 = (_dot(a0, v2b) + qc0) * inv0
        h1 = (_dot(a1, v2b) + qc1) * inv1
        hout2 = jnp.where(lo_head, h0, h1)
        kw = k2 * jnp.where(lo_head, ws0, ws1)
        upd = _dot_tn(kw.astype(BF16), jnp.concatenate([v2b, ones_b], axis=1))
        keep = jnp.where(row_lo, wc0, wc1)
        c_scr[sq, p] = jnp.where(blockdiag, keep * c_pair + upd[:, :LANES], 0.0)
        n_scr[sq, p] = keep * n_pair + upd[:, LANES:]
        m_scr[sq, pl.ds(2 * p, 1), :] = mn0
        m_scr[sq, pl.ds(2 * p + 1, 1), :] = mn1
        hc = jax.nn.sigmoid(o_ref[sq]) * hout2
        hsq = hc * hc
        ms0 = jnp.sum(jnp.where(lo_head, hsq, 0.0), axis=1, keepdims=True)
        ms1 = jnp.sum(jnp.where(lo_head, 0.0, hsq), axis=1, keepdims=True)
        ms = jnp.where(lo_head, ms0, ms1) * (1.0 / HEAD_DIM)
        y_ref[sq, :, p * LANES:(p + 1) * LANES] = \
            hc * lax.rsqrt(ms + EPS) * gain_ref[:, p * LANES:(p + 1) * LANES]

    gates = [gate_stage(sq) for sq in range(n_seq)]
    staged = [score_stage(sq, p) for sq, p in units]
    weights = [weight_stage(sq, p, gates[sq], st) for (sq, p), st in zip(units, staged)]
    for (sq, p), st, wt in zip(units, staged, weights):
        value_stage(sq, p, st, wt)

    @pl.when(ck == pl.num_programs(1) - 1)
    def _():
        for sq, p in units:
            c_pair = c_scr[sq, p]
            c_out_ref[sq, 2 * p] = c_pair[0:HEAD_DIM, 0:HEAD_DIM]
            c_out_ref[sq, 2 * p + 1] = c_pair[HEAD_DIM:, HEAD_DIM:]
            n_out_ref[sq, pl.ds(p, 1), :] = jnp.sum(jnp.where(eye, n_scr[sq, p], 0.0),
                                                    axis=0, keepdims=True)
        m_out_ref[...] = m_scr[...]


def _mlstm_consts():
    L = MLSTM_CHUNK
    t = jnp.arange(L)
    tri = (t[None, :] <= t[:, None]).astype(BF16)
    src = jnp.arange(3 * LANES) % LANES
    dst = jnp.arange(N_GATES * LANES) // LANES
    sel = (src[:, None] == dst[None, :]).astype(BF16)
    return tri, tri.T, sel


def _mlstm(z3, col0, g_rows, n_valid, sw, layer, C0, n0, m0):
    b, t_len, _ = z3.shape
    L = MLSTM_CHUNK
    nc = t_len // L
    G = ML_GROUP
    tri, tri_t, sel = _mlstm_consts()
    m_rows = jnp.broadcast_to(jnp.pad(m0, ((0, 0), (0, SUBLANES - N_HEADS_C)))[:, :, None],
                              (b, SUBLANES, LANES))

    def col(cb):
        return pl.BlockSpec((G, L, LANES), lambda i, c: (i, c, cb - col0))

    state_c = pl.BlockSpec((G, N_HEADS_C, HEAD_DIM, HEAD_DIM), lambda i, c: (i, 0, 0, 0))
    state_m = pl.BlockSpec((G, SUBLANES, LANES), lambda i, c: (i, 0, 0))
    y, c1, n1, m1 = pl.pallas_call(
        functools.partial(_mlstm_kernel, n_valid),
        grid=(b // G, nc),
        in_specs=[col(ZC_QC), col(ZC_QC + 1), col(ZC_KC), col(ZC_KC + 1), col(ZC_VC), col(ZC_VC + 1),
                  col(ZC_OC), col(ZC_OC + 1),
                  pl.BlockSpec((G, SUBLANES, L), lambda i, c: (i, 0, c)),
                  _layer_spec((SUBLANES, 1), layer), _layer_spec((1, D_C), layer),
                  _const_spec((L, L)), _const_spec((L, L)), _const_spec((3 * LANES, N_GATES * LANES)),
                  state_c, pl.BlockSpec((G, N_HEADS_C, HEAD_DIM), lambda i, c: (i, 0, 0)), state_m],
        out_specs=[pl.BlockSpec((G, L, D_C), lambda i, c: (i, c, 0)), state_c,
                   pl.BlockSpec((G, 2, LANES), lambda i, c: (i, 0, 0)), state_m],
        out_shape=[jax.ShapeDtypeStruct((b, t_len, D_C), F32),
                   jax.ShapeDtypeStruct((b, N_HEADS_C, HEAD_DIM, HEAD_DIM), F32),
                   jax.ShapeDtypeStruct((b, 2, LANES), F32),
                   jax.ShapeDtypeStruct((b, SUBLANES, LANES), F32)],
        scratch_shapes=[pltpu.VMEM((G, 2, LANES, LANES), F32), pltpu.VMEM((G, 2, LANES, LANES), F32),
                        pltpu.VMEM((G, SUBLANES, LANES), F32)],
        compiler_params=_params(("parallel", "arbitrary")),
        name="mlstm",
    )(*([z3] * 8), g_rows, sw['b_mlstm_gate'], sw['g_out_c'], tri, tri_t, sel, C0, n0, m_rows)
    return y, c1, n1.reshape(b, N_HEADS_C, HEAD_DIM), m1[:, 0:N_HEADS_C, 0]


def _block_diag(w):
    depth, nb, bs, _ = w.shape
    eye = jnp.eye(nb, dtype=w.dtype)
    return jnp.einsum('lncd,nm->lncmd', w, eye).reshape(depth, nb * bs, nb * bs)


def _stack_weights(p):
    vec = lambda v: v.reshape(v.shape[0], 1, -1)
    bf = lambda w: w.astype(BF16)
    out = {k: vec(p[k]) for k in (
        'g_f1_pre', 'g_f1_post', 'g_mix_pre', 'conv_b', 'b_rg_a', 'b_rg_x', 'lru_lambda', 'g_out_a',
        'g_out_b', 'g_out_c', 'g_mix_post', 'g_f2_pre', 'g_f2_post', 'g_ple_pre', 'g_ple_post')}
    out.update({k: bf(p[k]) for k in (
        'w_f1_gate', 'w_f1_up', 'w_f1_down', 'w_f2_gate', 'w_f2_up', 'w_f2_down', 'w_out',
        'w_ple_gate', 'w_ple_proj')})
    out['w_in'] = bf(jnp.pad(p['w_in'], ((0, 0), (0, 0), (0, Z_COLS - p['w_in'].shape[-1]))))
    out['conv_w'] = p['conv_w']
    out['w_rg_a'] = bf(_block_diag(p['w_rg_a']))
    out['w_rg_x'] = bf(_block_diag(p['w_rg_x']))
    out['b_mlstm_gate'] = jnp.concatenate([p['b_mlstm_i'], p['b_mlstm_f']], axis=1)[:, :, None]
    return out


def _gate_rows(z3, col0):
    lo = (ZC_GATE - col0) * LANES
    return jnp.swapaxes(z3[:, :, lo:lo + N_GATES], 1, 2)


def _layer(x, pe_all, sw, layer, cache):
    b, t_len, d = x.shape
    rows, z = _premix(x.reshape(b * t_len, d), sw, layer)
    z3 = z.reshape(b, t_len, Z_COLS)
    ka = z3[:, :, ZC_KA * LANES:ZC_KA * LANES + D_A]
    va = z3[:, :, ZC_VA * LANES:ZC_VA * LANES + D_A]
    if cache is None:
        keep = min(DILATED_CONFIGS[-1][0], t_len)
        new_k = ka[:, t_len - keep:].reshape(b, keep, N_HEADS_A, HEAD_DIM)
        new_v = va[:, t_len - keep:].reshape(b, keep, N_HEADS_A, HEAD_DIM)
        ya = _attn_prompt(z3)
        conv_buf = jnp.zeros((b, CONV_W - 1, D_B), F32)
        h0 = jnp.zeros((b, D_B), F32)
        C0 = jnp.zeros((b, N_HEADS_C, HEAD_DIM, HEAD_DIM), F32)
        n0 = jnp.zeros((b, N_HEADS_C, HEAD_DIM), F32)
        m0 = jnp.zeros((b, N_HEADS_C), F32)
        zc, col0, n_valid = z3, 0, MLSTM_CHUNK
    else:
        kt_all, vt_all, conv_buf, h0, C0, n0, m0 = cache
        new_k = ka.reshape(b, t_len, N_HEADS_A, HEAD_DIM)
        new_v = va.reshape(b, t_len, N_HEADS_A, HEAD_DIM)
        ya = _attn_sample(z3, kt_all, vt_all, layer)
        col0 = ZC_QC
        zc = jnp.pad(z3[:, :, col0 * LANES:], ((0, 0), (0, MLSTM_CHUNK - t_len), (0, 0)))
        n_valid = t_len
    yb, new_buf, h_last = _rglru(z3, conv_buf, h0, sw, layer)
    yc, C1, n1, m1 = _mlstm(zc, col0, _gate_rows(zc, col0), n_valid, sw, layer, C0, n0, m0)
    yc = yc[:, :t_len]
    rows = _postmix(rows, ya.reshape(b * t_len, D_A), yb.reshape(b * t_len, D_B),
                    yc.reshape(b * t_len, D_C), pe_all, sw, layer)
    return rows.reshape(b, t_len, d), (new_k, new_v, new_buf, h_last, C1, n1, m1)


def _feature_major(cache):
    depth, b, w_buf, nh, dh = cache.shape
    return jnp.transpose(cache, (0, 1, 3, 4, 2)).reshape(depth, b, nh * dh, w_buf)


def kernel(x_prompt, x_sample, cache_k, cache_v, state_conv, state_h, state_C, state_n, state_m, p_prompt, p_sample, g_f1_pre, w_f1_gate, w_f1_up, w_f1_down, g_f1_post, g_mix_pre, w_in, conv_w, conv_b, w_rg_a, b_rg_a, w_rg_x, b_rg_x, lru_lambda, b_mlstm_i, b_mlstm_f, g_out_a, g_out_b, g_out_c, w_out, g_mix_post, g_f2_pre, w_f2_gate, w_f2_up, w_f2_down, g_f2_post, g_ple_pre, w_ple_gate, w_ple_proj, g_ple_post):
    depth = w_in.shape[0]
    sw = _stack_weights(dict(
        g_f1_pre=g_f1_pre, w_f1_gate=w_f1_gate, w_f1_up=w_f1_up, w_f1_down=w_f1_down,
        g_f1_post=g_f1_post, g_mix_pre=g_mix_pre, w_in=w_in, conv_w=conv_w, conv_b=conv_b,
        w_rg_a=w_rg_a, b_rg_a=b_rg_a, w_rg_x=w_rg_x, b_rg_x=b_rg_x, lru_lambda=lru_lambda,
        b_mlstm_i=b_mlstm_i, b_mlstm_f=b_mlstm_f, g_out_a=g_out_a, g_out_b=g_out_b,
        g_out_c=g_out_c, w_out=w_out, g_mix_post=g_mix_post, g_f2_pre=g_f2_pre,
        w_f2_gate=w_f2_gate, w_f2_up=w_f2_up, w_f2_down=w_f2_down, g_f2_post=g_f2_post,
        g_ple_pre=g_ple_pre, w_ple_gate=w_ple_gate, w_ple_proj=w_ple_proj, g_ple_post=g_ple_post))
    xp, xs = x_prompt, x_sample
    pe_prompt = p_prompt.reshape(depth, -1, p_prompt.shape[-1])
    pe_sample = p_sample.reshape(depth, -1, p_sample.shape[-1])
    kt_all = _feature_major(cache_k)
    vt_all = _feature_major(cache_v)
    sp = [[] for _ in range(7)]
    ss = [[] for _ in range(7)]
    for i in range(depth):
        xp, st_p = _layer(xp, pe_prompt, sw, i, None)
        cache_i = (kt_all, vt_all, state_conv[i], state_h[i], state_C[i], state_n[i], state_m[i])
        xs, st_s = _layer(xs, pe_sample, sw, i, cache_i)
        for j in range(7):
            sp[j].append(st_p[j])
            ss[j].append(st_s[j])
    k_prompt, v_prompt, conv_prompt, h_prompt, C_prompt, n_prompt, m_prompt = [jnp.stack(a) for a in sp]
    k_sample, v_sample, conv_sample, h_sample, C_sample, n_sample, m_sample = [jnp.stack(a) for a in ss]
    return (xp, xs, k_prompt, v_prompt, k_sample, v_sample, conv_prompt, conv_sample,
            h_prompt, h_sample, C_prompt, C_sample, n_prompt, n_sample, m_prompt, m_sample)
```

```python
import functools

import jax
import jax.numpy as jnp
from jax import lax
from jax.experimental import pallas as pl
from jax.experimental.pallas import tpu as pltpu

F32 = jnp.float32
BF16 = jnp.bfloat16

EPS = 1e-6
NEG = -1e30
HEAD_DIM = 64
N_HEADS_A = 6
D_A = N_HEADS_A * HEAD_DIM
D_B = 384
N_HEADS_C = 4
D_C = N_HEADS_C * HEAD_DIM
N_GATES = 2 * N_HEADS_C
DILATED_CONFIGS = ((128, 1), (512, 4), (2048, 16))
LOCAL_BLK = 128
CONV_W = 4
LRU_C = 8.0
MLSTM_CHUNK = 128
LANES = 128
SUBLANES = 8
Z_COLS = 3072
FF_CHUNK = 256
ROW_TILE = 512
VMEM_LIMIT = 58 * 1024 * 1024

ZC_QA, ZC_KA, ZC_VA, ZC_XB, ZC_GB = 0, 3, 6, 9, 12
ZC_QC, ZC_KC, ZC_VC, ZC_OC, ZC_GATE = 15, 17, 19, 21, 23


def _rms(x, g):
    return x * lax.rsqrt(jnp.mean(x * x, axis=-1, keepdims=True) + EPS) * g


def _dot(a, b):
    return jnp.dot(a, b, preferred_element_type=F32)


def _dot_nt(a, b):
    return lax.dot_general(a, b, (((1,), (1,)), ((), ())), preferred_element_type=F32)


def _dot_tn(a, b):
    return lax.dot_general(a, b, (((0,), (0,)), ((), ())), preferred_element_type=F32)


def _softplus(x):
    return jnp.maximum(x, 0.0) + jnp.log1p(jnp.exp(-jnp.abs(x)))


def _log_sigmoid(x):
    return -_softplus(-x)


def _gelu_tanh(x):
    return 0.5 * x * (1.0 + jnp.tanh(0.7978845608028654 * (x + 0.044715 * (x * x * x))))


def _const_spec(shape):
    nd = len(shape)
    return pl.BlockSpec(shape, lambda *_: (0,) * nd, pipeline_mode=pl.Buffered(1))


def _layer_spec(shape, layer, block=None):
    nd = len(shape)
    idx = (0,) * nd if block is None else block
    return pl.BlockSpec((None,) + tuple(shape), lambda *_: (layer,) + idx,
                        pipeline_mode=pl.Buffered(1))


def _params(sem):
    return pltpu.CompilerParams(dimension_semantics=sem, vmem_limit_bytes=VMEM_LIMIT)


def _swiglu_into(acc_ref, h, wg_ref, wu_ref, wd_ref):
    for c in range(wg_ref.shape[1] // FF_CHUNK):
        sl = slice(c * FF_CHUNK, (c + 1) * FF_CHUNK)
        g = _dot(h, wg_ref[:, sl])
        u = _dot(h, wu_ref[:, sl])
        a = (g * jax.nn.sigmoid(g) * u).astype(BF16)
        d = _dot(a, wd_ref[sl, :])
        if c == 0:
            acc_ref[...] = d
        else:
            acc_ref[...] += d


def _premix_kernel(emit_kv, x_ref, gpre_ref, wg_ref, wu_ref, wd_ref, gpost_ref, gmix_ref, win_ref,
                   *rest):
    if emit_kv:
        _, _, xo_ref, z_ref, kt_ref, vt_ref, acc_ref = rest
    else:
        xo_ref, z_ref, acc_ref = rest
    x = x_ref[...]
    _swiglu_into(acc_ref, _rms(x, gpre_ref[...]).astype(BF16), wg_ref, wu_ref, wd_ref)
    x = x + 0.5 * _rms(acc_ref[...], gpost_ref[...])
    xo_ref[...] = x
    h = _rms(x, gmix_ref[...]).astype(BF16)
    for c in range(win_ref.shape[1] // 256):
        sl = slice(c * 256, (c + 1) * 256)
        z_ref[:, sl] = _dot(h, win_ref[:, sl])
    if emit_kv:
        kt_ref[...] = z_ref[:, ZC_KA * LANES:ZC_KA * LANES + D_A].T
        vt_ref[...] = z_ref[:, ZC_VA * LANES:ZC_VA * LANES + D_A].T


def _premix(x, sw, layer, kv_bufs=None, seq_len=None):
    m, d = x.shape
    ff = sw['w_f1_gate'].shape[-1]
    tm = min(ROW_TILE, m)
    row = pl.BlockSpec((tm, d), lambda i: (i, 0))
    vec = _layer_spec((1, d), layer)
    in_specs = [row, vec, _layer_spec((d, ff), layer), _layer_spec((d, ff), layer),
                _layer_spec((ff, d), layer), vec, vec, _layer_spec((d, Z_COLS), layer)]
    out_specs = [row, pl.BlockSpec((tm, Z_COLS), lambda i: (i, 0))]
    out_shape = [jax.ShapeDtypeStruct((m, d), F32), jax.ShapeDtypeStruct((m, Z_COLS), F32)]
    args = [x, sw['g_f1_pre'], sw['w_f1_gate'], sw['w_f1_up'], sw['w_f1_down'], sw['g_f1_post'],
            sw['g_mix_pre'], sw['w_in']]
    aliases = {}
    if kv_bufs is not None:
        keep = kv_bufs[0].shape[-1]
        tiles_per_seq = seq_len // tm
        first_kept = (seq_len - keep) // tm
        kv_spec = pl.BlockSpec(
            (None, None, D_A, tm),
            lambda i: (layer, i // tiles_per_seq, 0, jnp.maximum(i % tiles_per_seq - first_kept, 0)))
        for buf in kv_bufs:
            aliases[len(args)] = len(out_shape)
            in_specs.append(pl.BlockSpec(memory_space=pl.ANY))
            args.append(buf)
            out_specs.append(kv_spec)
            out_shape.append(jax.ShapeDtypeStruct(buf.shape, F32))
    return pl.pallas_call(
        functools.partial(_premix_kernel, kv_bufs is not None),
        grid=(m // tm,),
        in_specs=in_specs,
        out_specs=out_specs,
        out_shape=out_shape,
        scratch_shapes=[pltpu.VMEM((tm, d), F32)],
        input_output_aliases=aliases,
        compiler_params=_params(("arbitrary",)),
        name="premix",
    )(*args)


def _postmix_kernel(x_ref, ya_ref, yb_ref, yc_ref, pe_ref, ga_ref, wa_ref, wb_ref, wc_ref, gmix_ref,
                    gpre_ref, wg_ref, wu_ref, wd_ref, gpost_ref,
                    gple_ref, wpg_ref, wpp_ref, gple_post_ref, o_ref, acc_ref, x_scr):
    ya = _rms(ya_ref[...], ga_ref[...]).astype(BF16)
    y = _dot(ya, wa_ref[...])
    y = y + _dot(yb_ref[...].astype(BF16), wb_ref[...])
    y = y + _dot(yc_ref[...].astype(BF16), wc_ref[...])
    x_scr[...] = x_ref[...] + _rms(y, gmix_ref[...])
    _swiglu_into(acc_ref, _rms(x_scr[...], gpre_ref[...]).astype(BF16), wg_ref, wu_ref, wd_ref)
    x_scr[...] = x_scr[...] + 0.5 * _rms(acc_ref[...], gpost_ref[...])
    h = _rms(x_scr[...], gple_ref[...]).astype(BF16)
    gate = jax.nn.sigmoid(_dot(h, wpg_ref[...]))
    proj = _dot(pe_ref[...].astype(BF16), wpp_ref[...])
    o_ref[...] = x_scr[...] + _rms(gate * proj, gple_post_ref[...])


def _postmix(x, ya, yb, yc, pe_all, sw, layer):
    m, d = x.shape
    dp = pe_all.shape[-1]
    ff = sw['w_f2_gate'].shape[-1]
    tm = min(ROW_TILE, m)

    def row(width):
        return pl.BlockSpec((tm, width), lambda i: (i, 0))

    vec = _layer_spec((1, d), layer)
    return pl.pallas_call(
        _postmix_kernel,
        grid=(m // tm,),
        in_specs=[row(d), row(D_A), row(D_B), row(D_C),
                  pl.BlockSpec((None, tm, dp), lambda i: (layer, i, 0)),
                  _layer_spec((1, D_A), layer),
                  _layer_spec((D_A, d), layer, (0, 0)),
                  _layer_spec((D_B, d), layer, (D_A // D_B, 0)),
                  _layer_spec((D_C, d), layer, ((D_A + D_B) // D_C, 0)), vec,
                  vec, _layer_spec((d, ff), layer), _layer_spec((d, ff), layer),
                  _layer_spec((ff, d), layer), vec,
                  vec, _layer_spec((d, d), layer), _layer_spec((dp, d), layer), vec],
        out_specs=row(d),
        out_shape=jax.ShapeDtypeStruct((m, d), F32),
        scratch_shapes=[pltpu.VMEM((tm, d), F32), pltpu.VMEM((tm, d), F32)],
        compiler_params=_params(("parallel",)),
        name="postmix",
    )(x, ya, yb, yc, pe_all, sw['g_out_a'], sw['w_out'], sw['w_out'], sw['w_out'], sw['g_mix_post'],
      sw['g_f2_pre'], sw['w_f2_gate'], sw['w_f2_up'], sw['w_f2_down'], sw['g_f2_post'],
      sw['g_ple_pre'], sw['w_ple_gate'], sw['w_ple_proj'], sw['g_ple_post'])


ATT_QT = 2048
ATT_UNROLL = 8


ATT_PITCH = LOCAL_BLK + 8


def _attn_prompt_kernel(q_ref, k_ref, v_ref, bias_ref, o_ref, num_scr, m_scr, den_scr, cls_scr):
    t0 = pl.program_id(2) * ATT_QT
    blk = LOCAL_BLK
    lo_head = lax.broadcasted_iota(jnp.int32, (blk, LANES), 1) < HEAD_DIM
    scale = HEAD_DIM ** -0.5 * 1.4426950408889634
    n_cfg = len(DILATED_CONFIGS)

    for ci, (_, dil) in enumerate(DILATED_CONFIGS):
        span = blk * dil
        class_major = span == ATT_QT

        def scores(j, dil=dil, span=span):
            r = j % dil
            n = j // dil
            qs = r + span * n
            ks = t0 + qs
            first = ks < span
            ps = jnp.where(first, ks, ks - span)
            q2 = q_ref[pl.ds(qs, blk, stride=dil), :] * scale
            kc = k_ref[pl.ds(ks, blk, stride=dil), :].astype(BF16)
            kp = k_ref[pl.ds(ps, blk, stride=dil), :].astype(BF16)
            q_both = jnp.concatenate([jnp.where(lo_head, q2, 0.0), jnp.where(lo_head, 0.0, q2)], axis=0)
            raw = _dot_nt(q_both.astype(BF16), jnp.concatenate([kp, kc], axis=0))
            return j, qs, ks, ps, first.astype(jnp.int32), raw

        def softmax(first, raw):
            s = raw + bias_ref[first]
            m = jnp.max(s, axis=-1, keepdims=True)
            p = jnp.exp2(s - m)
            return p.astype(BF16), m, jnp.sum(p, axis=-1, keepdims=True)

        def weighted_values(j, qs, ks, ps, probs, ci=ci, dil=dil, class_major=class_major):
            vc = v_ref[pl.ds(ks, blk, stride=dil), :].astype(BF16)
            vp = v_ref[pl.ds(ps, blk, stride=dil), :].astype(BF16)
            p_both, m, den = probs
            num = _dot(p_both, jnp.concatenate([vp, vc], axis=0))
            num = jnp.where(lo_head, num[:blk], num[blk:])
            m = jnp.where(lo_head, m[:blk], m[blk:])
            den = jnp.where(lo_head, den[:blk], den[blk:])
            if class_major:
                dst = pl.ds(pl.multiple_of(j * ATT_PITCH, 8), blk)
                cls_scr[0, dst, :] = num
                cls_scr[1, dst, :] = m
                cls_scr[2, dst, :] = den
            else:
                dst = pl.ds(qs, blk, stride=dil)
                num_scr[ci, dst, :] = num
                m_scr[ci, dst, :] = m
                den_scr[ci, dst, :] = den

        def body(jj, carry, scores=scores, softmax=softmax, weighted_values=weighted_values):
            staged = [scores(jj * ATT_UNROLL + u) for u in range(ATT_UNROLL)]
            probs = [softmax(st[4], st[5]) for st in staged]
            for st, pr in zip(staged, probs):
                weighted_values(st[0], st[1], st[2], st[3], pr)
            return carry

        lax.fori_loop(0, ATT_QT // blk // ATT_UNROLL, body, 0)

    rows = 256
    dil_c = DILATED_CONFIGS[n_cfg - 1][1]

    def combine(i, carry):
        sl = pl.ds(pl.multiple_of(i * rows, rows), rows)

        def class_rows(a):
            base = i * (rows // dil_c)
            return jnp.concatenate(
                [cls_scr[a, pl.ds(base + g, dil_c, stride=ATT_PITCH), :] for g in range(rows // dil_c)],
                axis=0)

        ms = [m_scr[0, sl, :], m_scr[1, sl, :], class_rows(1)]
        nums = [num_scr[0, sl, :], num_scr[1, sl, :], class_rows(0)]
        dens = [den_scr[0, sl, :], den_scr[1, sl, :], class_rows(2)]
        mx = jnp.maximum(jnp.maximum(ms[0], ms[1]), ms[2])
        ws = [jnp.exp2(mc - mx) for mc in ms]
        num = nums[0] * ws[0] + nums[1] * ws[1] + nums[2] * ws[2]
        den = dens[0] * ws[0] + dens[1] * ws[1] + dens[2] * ws[2]
        o_ref[sl, :] = num / den
        return carry

    lax.fori_loop(0, ATT_QT // rows, combine, 0)


def _attn_bias():
    blk = LOCAL_BLK
    qi = jnp.arange(2 * blk)[:, None] % blk
    kj = jnp.arange(2 * blk)[None, :]
    prev_ok = (kj < blk) & (kj >= qi)
    cur_ok = (kj >= blk) & (kj - blk <= qi)
    normal = jnp.where(prev_ok | cur_ok, 0.0, NEG)
    first = jnp.where(cur_ok, 0.0, NEG)
    return jnp.stack([normal, first]).astype(F32)


def _attn_prompt(z3):
    b, s, _ = z3.shape
    n_pairs = D_A // LANES
    assert DILATED_CONFIGS[-1][1] * LOCAL_BLK == ATT_QT and len(DILATED_CONFIGS) == 3
    q_spec = pl.BlockSpec((None, ATT_QT, LANES), lambda i, p, t: (i, t, ZC_QA + p))
    k_spec = pl.BlockSpec((None, s, LANES), lambda i, p, t: (i, 0, ZC_KA + p))
    v_spec = pl.BlockSpec((None, s, LANES), lambda i, p, t: (i, 0, ZC_VA + p))
    return pl.pallas_call(
        _attn_prompt_kernel,
        grid=(b, n_pairs, s // ATT_QT),
        in_specs=[q_spec, k_spec, v_spec, _const_spec((2, 2 * LOCAL_BLK, 2 * LOCAL_BLK))],
        out_specs=pl.BlockSpec((None, ATT_QT, LANES), lambda i, p, t: (i, t, p)),
        out_shape=jax.ShapeDtypeStruct((b, s, D_A), F32),
        scratch_shapes=[pltpu.VMEM((2, ATT_QT, LANES), F32)] * 3
        + [pltpu.VMEM((3, DILATED_CONFIGS[-1][1] * ATT_PITCH, LANES), F32)],
        compiler_params=_params(("parallel", "parallel", "arbitrary")),
        name="attn_prompt",
    )(z3, z3, z3, _attn_bias())


def _attn_sample_kernel(q_ref, kn_ref, vn_ref, kt_ref, vt_ref, o_ref, q_scr, kn_scr, vn_scr):
    t_new = q_ref.shape[0]
    w_buf = kt_ref.shape[1]
    q_scr[...] = jnp.zeros(q_scr.shape, F32)
    kn_scr[...] = jnp.zeros(kn_scr.shape, F32)
    vn_scr[...] = jnp.zeros(vn_scr.shape, F32)
    q_scr[0:t_new, :] = q_ref[...] * (HEAD_DIM ** -0.5)
    kn_scr[0:t_new, :] = kn_ref[...]
    vn_scr[0:t_new, :] = vn_ref[...]

    rows = 2 * SUBLANES
    tq = lax.broadcasted_iota(jnp.int32, (rows, w_buf), 0) & (SUBLANES - 1)
    dist = w_buf + tq - lax.broadcasted_iota(jnp.int32, (rows, w_buf), 1)
    tqn = lax.broadcasted_iota(jnp.int32, (rows, LANES), 0) & (SUBLANES - 1)
    tn = lax.broadcasted_iota(jnp.int32, (rows, LANES), 1)
    dist_new = tqn - tn
    cache_ok, new_ok = [], []
    for window, dil in DILATED_CONFIGS:
        ok = dist <= window
        nk = jnp.logical_and(dist_new >= 0, tn < t_new)
        nk = jnp.logical_and(nk, dist_new <= window)
        if dil > 1:
            ok = jnp.logical_and(ok, (dist & (dil - 1)) == 0)
            nk = jnp.logical_and(nk, (dist_new & (dil - 1)) == 0)
        cache_ok.append(ok)
        new_ok.append(nk)
    lo8 = lax.broadcasted_iota(jnp.int32, (SUBLANES, LANES), 1) < HEAD_DIM

    for p in range(D_A // LANES):
        sl = slice(p * LANES, (p + 1) * LANES)
        q2 = q_scr[:, sl]
        qm = jnp.concatenate([jnp.where(lo8, q2, 0.0), jnp.where(lo8, 0.0, q2)], axis=0).astype(BF16)
        kb = kt_ref[sl, :].astype(BF16)
        vb = vt_ref[sl, :].astype(BF16)
        s = _dot(qm, kb)
        s_new = _dot_nt(qm, kn_scr[:, sl].astype(BF16))
        parts = []
        for c in range(len(DILATED_CONFIGS)):
            sc = jnp.where(cache_ok[c], s, NEG)
            sn = jnp.where(new_ok[c], s_new, NEG)
            m = jnp.maximum(jnp.max(sc, axis=1, keepdims=True), jnp.max(sn, axis=1, keepdims=True))
            pc = jnp.exp(sc - m)
            pn = jnp.exp(sn - m)
            den = jnp.sum(pc, axis=1, keepdims=True) + jnp.sum(pn, axis=1, keepdims=True)
            parts.append((pc, pn, m, den))
        mx = jnp.maximum(jnp.maximum(parts[0][2], parts[1][2]), parts[2][2])
        ws = [jnp.exp(pt[2] - mx) for pt in parts]
        den_all = parts[0][3] * ws[0] + parts[1][3] * ws[1] + parts[2][3] * ws[2]
        coefs = [w / den_all for w in ws]
        p_all = parts[0][0] * coefs[0] + parts[1][0] * coefs[1] + parts[2][0] * coefs[2]
        pn_all = parts[0][1] * coefs[0] + parts[1][1] * coefs[1] + parts[2][1] * coefs[2]
        o16 = _dot_nt(p_all.astype(BF16), vb) + _dot(pn_all.astype(BF16), vn_scr[:, sl].astype(BF16))
        o8 = jnp.where(lo8, o16[0:SUBLANES], o16[SUBLANES:rows])
        o_ref[:, sl] = o8[0:t_new]


def _attn_sample(zs3, kt_all, vt_all, layer):
    b, t_new, _ = zs3.shape
    w_buf = kt_all.shape[-1]

    def new_spec(col):
        return pl.BlockSpec((None, t_new, D_A), lambda i: (i, 0, col))

    cache_spec = pl.BlockSpec((None, None, D_A, w_buf), lambda i: (layer, i, 0, 0))
    return pl.pallas_call(
        _attn_sample_kernel,
        grid=(b,),
        in_specs=[new_spec(0), new_spec(1), new_spec(2), cache_spec, cache_spec],
        out_specs=pl.BlockSpec((None, t_new, D_A), lambda i: (i, 0, 0)),
        out_shape=jax.ShapeDtypeStruct((b, t_new, D_A), F32),
        scratch_shapes=[pltpu.VMEM((SUBLANES, D_A), F32), pltpu.VMEM((LANES, D_A), F32),
                        pltpu.VMEM((LANES, D_A), F32)],
        compiler_params=_params(("parallel",)),
        name="attn_sample",
    )(zs3, zs3, zs3, kt_all, vt_all)


RG_GROUP = SUBLANES
RG_CHUNK = 256
RG_PAD = 8


def _rg_pitch(tcp):
    tiles = tcp // 8 + 1
    return 8 * (tiles if tiles % 2 else tiles + 1)


def _rglru_kernel(xb_ref, gb_ref, conv_ref, h0_ref, cw_ref, cb_ref, wa_ref, ba_ref, wx_ref, bx_ref,
                  lam_ref, g_ref, y_ref, convnew_ref, ht_ref, xp_scr, hist_scr, h_scr, a_scr, b_scr):
    n_seq, tc, _ = xb_ref.shape
    tcp = xp_scr.shape[0] - RG_PAD
    pitch = a_scr.shape[1] // n_seq
    n_hist = CONV_W - 1
    hist = slice(RG_PAD - n_hist, RG_PAD)
    n_groups = D_B // LANES

    @pl.when(pl.program_id(1) == 0)
    def _():
        hist_scr[:, hist, :] = conv_ref[...]
        h_scr[...] = h0_ref[...]

    if tc % 8:
        xp_scr[RG_PAD:, :] = jnp.zeros((tcp, D_B), F32)
    decay = _softplus(-lam_ref[...])

    def gates(g, carry):
        xp_scr[hist, :] = hist_scr[g, hist, :]
        xp_scr[RG_PAD:RG_PAD + tc, :] = xb_ref[g]
        hist_scr[g, hist, :] = xp_scr[RG_PAD + tc - n_hist:RG_PAD + tc, :]
        xc = cb_ref[...]
        for j in range(CONV_W):
            lo = RG_PAD - n_hist + j
            xc = xc + xp_scr[lo:lo + tcp, :] * cw_ref[pl.ds(j, 1), :]
        xcb = xc.astype(BF16)
        r = jax.nn.sigmoid(_dot(xcb, wa_ref[...]) + ba_ref[...])
        gi = jax.nn.sigmoid(_dot(xcb, wx_ref[...]) + bx_ref[...])
        a = jnp.exp(-LRU_C * r * decay)
        bb = jnp.sqrt(1.0 - a * a) * (gi * xc)
        dst = pl.ds(pl.multiple_of(g * pitch, 8), tcp)
        for lg in range(n_groups):
            a_scr[lg, dst, :] = a[:, lg * LANES:(lg + 1) * LANES]
            b_scr[lg, dst, :] = bb[:, lg * LANES:(lg + 1) * LANES]
        return carry

    lax.fori_loop(0, n_seq, gates, 0)

    def step(t, hs):
        rows = pl.ds(t, n_seq, stride=pitch)
        out = []
        for lg in range(n_groups):
            h = a_scr[lg, rows, :] * hs[lg] + b_scr[lg, rows, :]
            b_scr[lg, rows, :] = h
            out.append(h)
        return tuple(out)

    hs = tuple(h_scr[:, lg * LANES:(lg + 1) * LANES] for lg in range(n_groups))
    hs = lax.fori_loop(0, tc, step, hs, unroll=min(8, tc))
    for lg in range(n_groups):
        h_scr[:, lg * LANES:(lg + 1) * LANES] = hs[lg]

    def finish(g, carry):
        src = pl.ds(pl.multiple_of(g * pitch, 8), tcp)
        hseq = jnp.concatenate([b_scr[lg, src, :] for lg in range(n_groups)], axis=1)
        y = _rms(_gelu_tanh(gb_ref[g]) * hseq[0:tc], g_ref[...])
        y_ref[g] = y
        return carry

    lax.fori_loop(0, n_seq, finish, 0)
    convnew_ref[...] = hist_scr[:, hist, :]
    ht_ref[...] = h_scr[...]


def _rglru(z3, conv_buf, h0, sw, layer):
    b, t_len, _ = z3.shape
    tc = min(RG_CHUNK, t_len)
    tcp = -(-tc // 8) * 8
    pitch = _rg_pitch(tcp)
    n_hist = CONV_W - 1
    ng = b // RG_GROUP
    vec = _layer_spec((1, D_B), layer)
    mat = _layer_spec((D_B, D_B), layer)
    seq = pl.BlockSpec((RG_GROUP, tc, D_B), lambda i, c: (i, c, 0))
    hist = pl.BlockSpec((RG_GROUP, n_hist, D_B), lambda i, c: (i, 0, 0))
    state = pl.BlockSpec((None, RG_GROUP, D_B), lambda i, c: (i, 0, 0))
    y, conv_new, h_last = pl.pallas_call(
        _rglru_kernel,
        grid=(ng, t_len // tc),
        in_specs=[pl.BlockSpec((RG_GROUP, tc, D_B), lambda i, c: (i, c, ZC_XB // 3)),
                  pl.BlockSpec((RG_GROUP, tc, D_B), lambda i, c: (i, c, ZC_GB // 3)),
                  hist, state,
                  _layer_spec((CONV_W, D_B), layer), vec, mat, vec, mat, vec, vec, vec],
        out_specs=[seq, hist, state],
        out_shape=[jax.ShapeDtypeStruct((b, t_len, D_B), F32),
                   jax.ShapeDtypeStruct((b, n_hist, D_B), F32),
                   jax.ShapeDtypeStruct((ng, RG_GROUP, D_B), F32)],
        scratch_shapes=[pltpu.VMEM((RG_PAD + tcp, D_B), F32),
                        pltpu.VMEM((RG_GROUP, RG_PAD, D_B), F32),
                        pltpu.VMEM((RG_GROUP, D_B), F32),
                        pltpu.VMEM((D_B // LANES, RG_GROUP * pitch, LANES), F32),
                        pltpu.VMEM((D_B // LANES, RG_GROUP * pitch, LANES), F32)],
        compiler_params=_params(("parallel", "arbitrary")),
        name="rglru",
    )(z3, z3, conv_buf, h0.reshape(ng, RG_GROUP, D_B), sw['conv_w'], sw['conv_b'], sw['w_rg_a'],
      sw['b_rg_a'], sw['w_rg_x'], sw['b_rg_x'], sw['lru_lambda'], sw['g_out_b'])
    return y, conv_new, h_last.reshape(b, D_B)


ML_GROUP = 4


def _split3(x, axis):
    hi = x.astype(BF16).astype(F32)
    r1 = x - hi
    mid = r1.astype(BF16).astype(F32)
    lo = (r1 - mid).astype(BF16).astype(F32)
    return jnp.concatenate([hi, mid, lo], axis=axis).astype(BF16)


def _sum3(x, axis):
    n = x.shape[axis] // 3
    if axis == 0:
        return x[0:n] + x[n:2 * n] + x[2 * n:3 * n]
    return x[:, 0:n] + x[:, n:2 * n] + x[:, 2 * n:3 * n]


def _mlstm_kernel(n_valid, q0_ref, q1_ref, k0_ref, k1_ref, v0_ref, v1_ref, o0_ref, o1_ref,
                  gr_ref, bias_ref, gain_ref, tri_ref, trit_ref, sel_ref,
                  c0_ref, n0_ref, m0_ref, y_ref, c_out_ref, n_out_ref, m_out_ref,
                  c_scr, n_scr, m_scr):
    ck = pl.program_id(1)
    n_seq, _, L = gr_ref.shape

    lane = lax.broadcasted_iota(jnp.int32, (L, LANES), 1)
    row = lax.broadcasted_iota(jnp.int32, (L, LANES), 0)
    lo_head = lane < HEAD_DIM
    causal = lane <= row
    row_lo = row < HEAD_DIM
    eye = lane == row
    blockdiag = ((row ^ lane) & HEAD_DIM) == 0
    sub = lax.broadcasted_iota(jnp.int32, (SUBLANES, L), 0)
    tl = lax.broadcasted_iota(jnp.int32, (SUBLANES, L), 1)
    ones_b = jnp.ones((L, LANES), BF16)
    refs = ((q0_ref, k0_ref, v0_ref, o0_ref), (q1_ref, k1_ref, v1_ref, o1_ref))
    units = [(sq, p) for sq in range(n_seq) for p in range(2)]

    @pl.when(ck == 0)
    def _():
        zero = jnp.zeros((HEAD_DIM, HEAD_DIM), F32)
        ones3 = jnp.ones((3 * LANES, LANES), BF16)
        for sq, p in units:
            top = jnp.concatenate([c0_ref[sq, 2 * p], zero], axis=1)
            bot = jnp.concatenate([zero, c0_ref[sq, 2 * p + 1]], axis=1)
            c_scr[sq, p] = jnp.concatenate([top, bot], axis=0)
            n_row = jnp.concatenate([n0_ref[sq, pl.ds(2 * p, 1), :],
                                     n0_ref[sq, pl.ds(2 * p + 1, 1), :]], axis=1)
            n_scr[sq, p] = _dot(_split3(jnp.where(eye, n_row, 0.0), 1), ones3)
        m_scr[...] = m0_ref[...]

    def gate_stage(sq):
        xr = gr_ref[sq] + bias_ref[...]
        gt_r = jnp.where(sub < N_HEADS_C, xr, _log_sigmoid(xr))
        if n_valid < L:
            gt_r = jnp.where(tl < n_valid, gt_r, jnp.where(sub < N_HEADS_C, NEG, 0.0))
        gt_c = jnp.transpose(jnp.concatenate([gt_r, jnp.zeros((L - SUBLANES, L), F32)], axis=0))
        b_col = _sum3(_dot(tri_ref[...], _split3(gt_c, 1)), 1)
        b_row = _sum3(_dot(_split3(gt_r, 0), trit_ref[...]), 0)
        cols = _dot(_split3(jnp.where(lane < N_HEADS_C, gt_c, b_col), 1), sel_ref[...])
        return gt_r, b_row, cols

    def score_stage(sq, p):
        q_ref, k_ref, v_ref, _ = refs[p]
        q2 = q_ref[sq]
        k2 = k_ref[sq] * (HEAD_DIM ** -0.5)
        v2b = v_ref[sq].astype(BF16)
        k2b = k2.astype(BF16)
        c_pair = c_scr[sq, p]
        n_pair = n_scr[sq, p]
        cn_b = jnp.concatenate([c_pair, n_pair], axis=1).astype(BF16)
        heads = []
        for h in range(2):
            qh = jnp.where(lo_head if h == 0 else jnp.logical_not(lo_head), q2, 0.0).astype(BF16)
            heads.append((_dot_nt(qh, k2b), _dot(qh, cn_b)))
        return k2, v2b, c_pair, n_pair, heads

    def weight_stage(sq, p, gates, staged):
        gt_r, b_row, cols = gates
        heads = staged[4]
        out = []
        for h in range(2):
            gh = 2 * p + h
            s_h, qx = heads[h]
            ig_c = cols[:, gh * LANES:(gh + 1) * LANES]
            b_c = cols[:, (N_HEADS_C + gh) * LANES:(N_HEADS_C + gh + 1) * LANES]
            b_r = b_row[N_HEADS_C + gh:N_HEADS_C + gh + 1, :]
            ig_r = gt_r[gh:gh + 1, :]
            m_prev = m_scr[sq, pl.ds(gh, 1), :]
            dm = jnp.where(causal, b_c - b_r + ig_r, NEG)
            inter = b_c + m_prev
            mt = jnp.maximum(inter, jnp.max(dm, axis=1, keepdims=True))
            wi = jnp.exp(inter - mt)
            a = jnp.exp(dm - mt) * s_h
            den = jnp.sum(a, axis=1, keepdims=True) + wi * qx[:, LANES:]
            inv = 1.0 / jnp.maximum(jnp.abs(den), jnp.exp(-mt))
            m_new = mt[L - 1:L, :]
            b_last = b_c[L - 1:L, :]
            ws = jnp.exp(b_last - b_c + ig_c - m_new)
            wc = jnp.exp(b_last + m_prev - m_new)
            out.append((a.astype(BF16), wi * qx[:, :LANES], inv, ws, wc, m_new))
        return out

    def value_stage(sq, p, staged, weights):
        o_ref = refs[p][3]
        k2, v2b, c_pair, n_pair, _ = staged
        (a0, qc0, inv0, ws0, wc0, mn0), (a1, qc1, inv1, ws1, wc1, mn1) = weights
        h0 = (_dot(a0, v2b) + qc0) * inv0
        h1 = (_dot(a1, v2b) + qc1) * inv1
        hout2 = jnp.where(lo_head, h0, h1)
        kw = k2 * jnp.where(lo_head, ws0, ws1)
        upd = _dot_tn(kw.astype(BF16), jnp.concatenate([v2b, ones_b], axis=1))
        keep = jnp.where(row_lo, wc0, wc1)
        c_scr[sq, p] = jnp.where(blockdiag, keep * c_pair + upd[:, :LANES], 0.0)
        n_scr[sq, p] = keep * n_pair + upd[:, LANES:]
        m_scr[sq, pl.ds(2 * p, 1), :] = mn0
        m_scr[sq, pl.ds(2 * p + 1, 1), :] = mn1
        hc = jax.nn.sigmoid(o_ref[sq]) * hout2
        hsq = hc * hc
        ms0 = jnp.sum(jnp.where(lo_head, hsq, 0.0), axis=1, keepdims=True)
        ms1 = jnp.sum(jnp.where(lo_head, 0.0, hsq), axis=1, keepdims=True)
        ms = jnp.where(lo_head, ms0, ms1) * (1.0 / HEAD_DIM)
        y_ref[sq, :, p * LANES:(p + 1) * LANES] = \
            hc * lax.rsqrt(ms + EPS) * gain_ref[:, p * LANES:(p + 1) * LANES]

    gates = [gate_stage(sq) for sq in range(n_seq)]
    staged = [score_stage(sq, p) for sq, p in units]
    weights = [weight_stage(sq, p, gates[sq], st) for (sq, p), st in zip(units, staged)]
    for (sq, p), st, wt in zip(units, staged, weights):
        value_stage(sq, p, st, wt)

    @pl.when(ck == pl.num_programs(1) - 1)
    def _():
        for sq, p in units:
            c_pair = c_scr[sq, p]
            c_out_ref[sq, 2 * p] = c_pair[0:HEAD_DIM, 0:HEAD_DIM]
            c_out_ref[sq, 2 * p + 1] = c_pair[HEAD_DIM:, HEAD_DIM:]
            n_out_ref[sq, pl.ds(p, 1), :] = jnp.sum(jnp.where(eye, n_scr[sq, p], 0.0),
                                                    axis=0, keepdims=True)
        m_out_ref[...] = m_scr[...]


def _mlstm_consts():
    L = MLSTM_CHUNK
    t = jnp.arange(L)
    tri = (t[None, :] <= t[:, None]).astype(BF16)
    src = jnp.arange(3 * LANES) % LANES
    dst = jnp.arange(N_GATES * LANES) // LANES
    sel = (src[:, None] == dst[None, :]).astype(BF16)
    return tri, tri.T, sel


def _mlstm(z3, col0, g_rows, n_valid, sw, layer, C0, n0, m0):
    b, t_len, _ = z3.shape
    L = MLSTM_CHUNK
    nc = t_len // L
    G = ML_GROUP
    tri, tri_t, sel = _mlstm_consts()
    m_rows = jnp.broadcast_to(jnp.pad(m0, ((0, 0), (0, SUBLANES - N_HEADS_C)))[:, :, None],
                              (b, SUBLANES, LANES))

    def col(cb):
        return pl.BlockSpec((G, L, LANES), lambda i, c: (i, c, cb - col0))

    state_c = pl.BlockSpec((G, N_HEADS_C, HEAD_DIM, HEAD_DIM), lambda i, c: (i, 0, 0, 0))
    state_m = pl.BlockSpec((G, SUBLANES, LANES), lambda i, c: (i, 0, 0))
    y, c1, n1, m1 = pl.pallas_call(
        functools.partial(_mlstm_kernel, n_valid),
        grid=(b // G, nc),
        in_specs=[col(ZC_QC), col(ZC_QC + 1), col(ZC_KC), col(ZC_KC + 1), col(ZC_VC), col(ZC_VC + 1),
                  col(ZC_OC), col(ZC_OC + 1),
                  pl.BlockSpec((G, SUBLANES, L), lambda i, c: (i, 0, c)),
                  _layer_spec((SUBLANES, 1), layer), _layer_spec((1, D_C), layer),
                  _const_spec((L, L)), _const_spec((L, L)), _const_spec((3 * LANES, N_GATES * LANES)),
                  state_c, pl.BlockSpec((G, N_HEADS_C, HEAD_DIM), lambda i, c: (i, 0, 0)), state_m],
        out_specs=[pl.BlockSpec((G, L, D_C), lambda i, c: (i, c, 0)), state_c,
                   pl.BlockSpec((G, 2, LANES), lambda i, c: (i, 0, 0)), state_m],
        out_shape=[jax.ShapeDtypeStruct((b, t_len, D_C), F32),
                   jax.ShapeDtypeStruct((b, N_HEADS_C, HEAD_DIM, HEAD_DIM), F32),
                   jax.ShapeDtypeStruct((b, 2, LANES), F32),
                   jax.ShapeDtypeStruct((b, SUBLANES, LANES), F32)],
        scratch_shapes=[pltpu.VMEM((G, 2, LANES, LANES), F32), pltpu.VMEM((G, 2, LANES, LANES), F32),
                        pltpu.VMEM((G, SUBLANES, LANES), F32)],
        compiler_params=_params(("parallel", "arbitrary")),
        name="mlstm",
    )(*([z3] * 8), g_rows, sw['b_mlstm_gate'], sw['g_out_c'], tri, tri_t, sel, C0, n0, m_rows)
    return y, c1, n1.reshape(b, N_HEADS_C, HEAD_DIM), m1[:, 0:N_HEADS_C, 0]


def _block_diag(w):
    depth, nb, bs, _ = w.shape
    eye = jnp.eye(nb, dtype=w.dtype)
    return jnp.einsum('lncd,nm->lncmd', w, eye).reshape(depth, nb * bs, nb * bs)


def _stack_weights(p):
    vec = lambda v: v.reshape(v.shape[0], 1, -1)
    bf = lambda w: w.astype(BF16)
    out = {k: vec(p[k]) for k in (
        'g_f1_pre', 'g_f1_post', 'g_mix_pre', 'conv_b', 'b_rg_a', 'b_rg_x', 'lru_lambda', 'g_out_a',
        'g_out_b', 'g_out_c', 'g_mix_post', 'g_f2_pre', 'g_f2_post', 'g_ple_pre', 'g_ple_post')}
    out.update({k: bf(p[k]) for k in (
        'w_f1_gate', 'w_f1_up', 'w_f1_down', 'w_f2_gate', 'w_f2_up', 'w_f2_down', 'w_out',
        'w_ple_gate', 'w_ple_proj')})
    out['w_in'] = bf(jnp.pad(p['w_in'], ((0, 0), (0, 0), (0, Z_COLS - p['w_in'].shape[-1]))))
    out['conv_w'] = p['conv_w']
    out['w_rg_a'] = bf(_block_diag(p['w_rg_a']))
    out['w_rg_x'] = bf(_block_diag(p['w_rg_x']))
    out['b_mlstm_gate'] = jnp.concatenate([p['b_mlstm_i'], p['b_mlstm_f']], axis=1)[:, :, None]
    return out


def _gate_rows(z3, col0):
    lo = (ZC_GATE - col0) * LANES
    return jnp.swapaxes(z3[:, :, lo:lo + N_GATES], 1, 2)


def _layer(x, pe_all, sw, layer, cache, kv_bufs=None):
    b, t_len, d = x.shape
    if cache is None:
        rows, z, new_k, new_v = _premix(x.reshape(b * t_len, d), sw, layer, kv_bufs, t_len)
        z3 = z.reshape(b, t_len, Z_COLS)
        ya = _attn_prompt(z3)
        conv_buf = jnp.zeros((b, CONV_W - 1, D_B), F32)
        h0 = jnp.zeros((b, D_B), F32)
        C0 = jnp.zeros((b, N_HEADS_C, HEAD_DIM, HEAD_DIM), F32)
        n0 = jnp.zeros((b, N_HEADS_C, HEAD_DIM), F32)
        m0 = jnp.zeros((b, N_HEADS_C), F32)
        zc, col0, n_valid = z3, 0, MLSTM_CHUNK
    else:
        kt_all, vt_all, conv_buf, h0, C0, n0, m0 = cache
        rows, z = _premix(x.reshape(b * t_len, d), sw, layer)
        z3 = z.reshape(b, t_len, Z_COLS)
        new_k = z3[:, :, ZC_KA * LANES:ZC_KA * LANES + D_A].reshape(b, t_len, N_HEADS_A, HEAD_DIM)
        new_v = z3[:, :, ZC_VA * LANES:ZC_VA * LANES + D_A].reshape(b, t_len, N_HEADS_A, HEAD_DIM)
        ya = _attn_sample(z3, kt_all, vt_all, layer)
        col0 = ZC_QC
        zc = jnp.pad(z3[:, :, col0 * LANES:], ((0, 0), (0, MLSTM_CHUNK - t_len), (0, 0)))
        n_valid = t_len
    yb, new_buf, h_last = _rglru(z3, conv_buf, h0, sw, layer)
    yc, C1, n1, m1 = _mlstm(zc, col0, _gate_rows(zc, col0), n_valid, sw, layer, C0, n0, m0)
    yc = yc[:, :t_len]
    rows = _postmix(rows, ya.reshape(b * t_len, D_A), yb.reshape(b * t_len, D_B),
                    yc.reshape(b * t_len, D_C), pe_all, sw, layer)
    return rows.reshape(b, t_len, d), (new_k, new_v, new_buf, h_last, C1, n1, m1)


def _feature_major(cache):
    depth, b, w_buf, nh, dh = cache.shape
    return jnp.transpose(cache, (0, 1, 3, 4, 2)).reshape(depth, b, nh * dh, w_buf)


def kernel(x_prompt, x_sample, cache_k, cache_v, state_conv, state_h, state_C, state_n, state_m, p_prompt, p_sample, g_f1_pre, w_f1_gate, w_f1_up, w_f1_down, g_f1_post, g_mix_pre, w_in, conv_w, conv_b, w_rg_a, b_rg_a, w_rg_x, b_rg_x, lru_lambda, b_mlstm_i, b_mlstm_f, g_out_a, g_out_b, g_out_c, w_out, g_mix_post, g_f2_pre, w_f2_gate, w_f2_up, w_f2_down, g_f2_post, g_ple_pre, w_ple_gate, w_ple_proj, g_ple_post):
    depth = w_in.shape[0]
    sw = _stack_weights(dict(
        g_f1_pre=g_f1_pre, w_f1_gate=w_f1_gate, w_f1_up=w_f1_up, w_f1_down=w_f1_down,
        g_f1_post=g_f1_post, g_mix_pre=g_mix_pre, w_in=w_in, conv_w=conv_w, conv_b=conv_b,
        w_rg_a=w_rg_a, b_rg_a=b_rg_a, w_rg_x=w_rg_x, b_rg_x=b_rg_x, lru_lambda=lru_lambda,
        b_mlstm_i=b_mlstm_i, b_mlstm_f=b_mlstm_f, g_out_a=g_out_a, g_out_b=g_out_b,
        g_out_c=g_out_c, w_out=w_out, g_mix_post=g_mix_post, g_f2_pre=g_f2_pre,
        w_f2_gate=w_f2_gate, w_f2_up=w_f2_up, w_f2_down=w_f2_down, g_f2_post=g_f2_post,
        g_ple_pre=g_ple_pre, w_ple_gate=w_ple_gate, w_ple_proj=w_ple_proj, g_ple_post=g_ple_post))
    xp, xs = x_prompt, x_sample
    pe_prompt = p_prompt.reshape(depth, -1, p_prompt.shape[-1])
    pe_sample = p_sample.reshape(depth, -1, p_sample.shape[-1])
    kt_all = _feature_major(cache_k)
    vt_all = _feature_major(cache_v)
    b_p, s_p, _ = x_prompt.shape
    keep = min(DILATED_CONFIGS[-1][0], s_p)
    kv_bufs = (jnp.zeros((depth, b_p, D_A, keep), F32), jnp.zeros((depth, b_p, D_A, keep), F32))
    sp = [[] for _ in range(5)]
    ss = [[] for _ in range(7)]
    for i in range(depth):
        xp, st_p = _layer(xp, pe_prompt, sw, i, None, kv_bufs)
        kv_bufs = st_p[:2]
        cache_i = (kt_all, vt_all, state_conv[i], state_h[i], state_C[i], state_n[i], state_m[i])
        xs, st_s = _layer(xs, pe_sample, sw, i, cache_i)
        for j in range(5):
            sp[j].append(st_p[2 + j])
        for j in range(7):
            ss[j].append(st_s[j])
    k_prompt, v_prompt = [
        jnp.transpose(buf.reshape(depth, b_p, N_HEADS_A, HEAD_DIM, keep), (0, 1, 4, 2, 3))
        for buf in kv_bufs]
    conv_prompt, h_prompt, C_prompt, n_prompt, m_prompt = [jnp.stack(a) for a in sp]
    k_sample, v_sample, conv_sample, h_sample, C_sample, n_sample, m_sample = [jnp.stack(a) for a in ss]
    return (xp, xs, k_prompt, v_prompt, k_sample, v_sample, conv_prompt, conv_sample,
            h_prompt, h_sample, C_prompt, C_sample, n_prompt, n_sample, m_prompt, m_sample)
```

```python
import functools

import jax
import jax.numpy as jnp
from jax import lax
from jax.experimental import pallas as pl
from jax.experimental.pallas import tpu as pltpu

F32 = jnp.float32
BF16 = jnp.bfloat16

EPS = 1e-6
NEG = -1e30
HEAD_DIM = 64
N_HEADS_A = 6
D_A = N_HEADS_A * HEAD_DIM
D_B = 384
N_HEADS_C = 4
D_C = N_HEADS_C * HEAD_DIM
N_GATES = 2 * N_HEADS_C
DILATED_CONFIGS = ((128, 1), (512, 4), (2048, 16))
LOCAL_BLK = 128
CONV_W = 4
LRU_C = 8.0
MLSTM_CHUNK = 128
LANES = 128
SUBLANES = 8
Z_COLS = 3072
FF_CHUNK = 256
ROW_TILE = 512
VMEM_LIMIT = 58 * 1024 * 1024

ZC_QA, ZC_KA, ZC_VA, ZC_XB, ZC_GB = 0, 3, 6, 9, 12
ZC_QC, ZC_KC, ZC_VC, ZC_OC, ZC_GATE = 15, 17, 19, 21, 23


def _rms(x, g):
    return x * lax.rsqrt(jnp.mean(x * x, axis=-1, keepdims=True) + EPS) * g


def _dot(a, b):
    return jnp.dot(a, b, preferred_element_type=F32)


def _dot_nt(a, b):
    return lax.dot_general(a, b, (((1,), (1,)), ((), ())), preferred_element_type=F32)


def _dot_tn(a, b):
    return lax.dot_general(a, b, (((0,), (0,)), ((), ())), preferred_element_type=F32)


def _softplus(x):
    return jnp.maximum(x, 0.0) + jnp.log1p(jnp.exp(-jnp.abs(x)))


def _log_sigmoid(x):
    return -_softplus(-x)


def _gelu_tanh(x):
    return 0.5 * x * (1.0 + jnp.tanh(0.7978845608028654 * (x + 0.044715 * (x * x * x))))


def _const_spec(shape):
    nd = len(shape)
    return pl.BlockSpec(shape, lambda *_: (0,) * nd, pipeline_mode=pl.Buffered(1))


def _layer_spec(shape, layer, block=None):
    nd = len(shape)
    idx = (0,) * nd if block is None else block
    return pl.BlockSpec((None,) + tuple(shape), lambda *_: (layer,) + idx,
                        pipeline_mode=pl.Buffered(1))


def _params(sem):
    return pltpu.CompilerParams(dimension_semantics=sem, vmem_limit_bytes=VMEM_LIMIT)


def _swiglu_into(acc_ref, h, wg_ref, wu_ref, wd_ref):
    for c in range(wg_ref.shape[1] // FF_CHUNK):
        sl = slice(c * FF_CHUNK, (c + 1) * FF_CHUNK)
        g = _dot(h, wg_ref[:, sl])
        u = _dot(h, wu_ref[:, sl])
        a = (g * jax.nn.sigmoid(g) * u).astype(BF16)
        d = _dot(a, wd_ref[sl, :])
        if c == 0:
            acc_ref[...] = d
        else:
            acc_ref[...] += d


def _premix_kernel(emit_kv, x_ref, gpre_ref, wg_ref, wu_ref, wd_ref, gpost_ref, gmix_ref, win_ref,
                   *rest):
    if emit_kv:
        _, _, xo_ref, z_ref, kt_ref, vt_ref, zc_ref, acc_ref, stage_ref = rest
    else:
        xo_ref, z_ref, acc_ref = rest
    x = x_ref[...]
    _swiglu_into(acc_ref, _rms(x, gpre_ref[...]).astype(BF16), wg_ref, wu_ref, wd_ref)
    x = x + 0.5 * _rms(acc_ref[...], gpost_ref[...])
    xo_ref[...] = x
    h = _rms(x, gmix_ref[...]).astype(BF16)
    for c in range(win_ref.shape[1] // 256):
        sl = slice(c * 256, (c + 1) * 256)
        z_ref[:, sl] = _dot(h, win_ref[:, sl])
    if emit_kv:
        kt_ref[...] = z_ref[:, ZC_KA * LANES:ZC_KA * LANES + D_A].T
        vt_ref[...] = z_ref[:, ZC_VA * LANES:ZC_VA * LANES + D_A].T
        n_cls, rows_per_cls, width = zc_ref.shape
        for lg in range(width // LANES):
            sl = slice(lg * LANES, (lg + 1) * LANES)
            stage_ref[...] = z_ref[:, sl]
            for r in range(n_cls):
                zc_ref[r, :, sl] = stage_ref[pl.ds(r, rows_per_cls, stride=n_cls), :]


def _premix(x, sw, layer, kv_bufs=None, seq_len=None):
    m, d = x.shape
    ff = sw['w_f1_gate'].shape[-1]
    tm = min(ROW_TILE, m)
    row = pl.BlockSpec((tm, d), lambda i: (i, 0))
    vec = _layer_spec((1, d), layer)
    in_specs = [row, vec, _layer_spec((d, ff), layer), _layer_spec((d, ff), layer),
                _layer_spec((ff, d), layer), vec, vec, _layer_spec((d, Z_COLS), layer)]
    out_specs = [row, pl.BlockSpec((tm, Z_COLS), lambda i: (i, 0))]
    out_shape = [jax.ShapeDtypeStruct((m, d), F32), jax.ShapeDtypeStruct((m, Z_COLS), F32)]
    args = [x, sw['g_f1_pre'], sw['w_f1_gate'], sw['w_f1_up'], sw['w_f1_down'], sw['g_f1_post'],
            sw['g_mix_pre'], sw['w_in']]
    aliases = {}
    scratch = [pltpu.VMEM((tm, d), F32)]
    if kv_bufs is not None:
        keep = kv_bufs[0].shape[-1]
        tiles_per_seq = seq_len // tm
        first_kept = (seq_len - keep) // tm
        kv_spec = pl.BlockSpec(
            (None, None, D_A, tm),
            lambda i: (layer, i // tiles_per_seq, 0, jnp.maximum(i % tiles_per_seq - first_kept, 0)))
        for buf in kv_bufs:
            aliases[len(args)] = len(out_shape)
            in_specs.append(pl.BlockSpec(memory_space=pl.ANY))
            args.append(buf)
            out_specs.append(kv_spec)
            out_shape.append(jax.ShapeDtypeStruct(buf.shape, F32))
        n_cls = DILATED_CONFIGS[-1][1]
        out_specs.append(pl.BlockSpec((None, n_cls, tm // n_cls, 3 * D_A),
                                      lambda i: (i // tiles_per_seq, 0, i % tiles_per_seq, 0)))
        out_shape.append(jax.ShapeDtypeStruct((m // seq_len, n_cls, seq_len // n_cls, 3 * D_A), F32))
        scratch.append(pltpu.VMEM((tm, LANES), F32))
    return pl.pallas_call(
        functools.partial(_premix_kernel, kv_bufs is not None),
        grid=(m // tm,),
        in_specs=in_specs,
        out_specs=out_specs,
        out_shape=out_shape,
        scratch_shapes=scratch,
        input_output_aliases=aliases,
        compiler_params=_params(("arbitrary",)),
        name="premix",
    )(*args)


def _postmix_kernel(x_ref, ya_ref, yb_ref, yc_ref, pe_ref, ga_ref, wa_ref, wb_ref, wc_ref, gmix_ref,
                    gpre_ref, wg_ref, wu_ref, wd_ref, gpost_ref,
                    gple_ref, wpg_ref, wpp_ref, gple_post_ref, o_ref, acc_ref, x_scr):
    ya = _rms(ya_ref[...], ga_ref[...]).astype(BF16)
    y = _dot(ya, wa_ref[...])
    y = y + _dot(yb_ref[...].astype(BF16), wb_ref[...])
    y = y + _dot(yc_ref[...].astype(BF16), wc_ref[...])
    x_scr[...] = x_ref[...] + _rms(y, gmix_ref[...])
    _swiglu_into(acc_ref, _rms(x_scr[...], gpre_ref[...]).astype(BF16), wg_ref, wu_ref, wd_ref)
    x_scr[...] = x_scr[...] + 0.5 * _rms(acc_ref[...], gpost_ref[...])
    h = _rms(x_scr[...], gple_ref[...]).astype(BF16)
    gate = jax.nn.sigmoid(_dot(h, wpg_ref[...]))
    proj = _dot(pe_ref[...].astype(BF16), wpp_ref[...])
    o_ref[...] = x_scr[...] + _rms(gate * proj, gple_post_ref[...])


def _postmix(x, ya, yb, yc, pe_all, sw, layer):
    m, d = x.shape
    dp = pe_all.shape[-1]
    ff = sw['w_f2_gate'].shape[-1]
    tm = min(ROW_TILE, m)

    def row(width):
        return pl.BlockSpec((tm, width), lambda i: (i, 0))

    vec = _layer_spec((1, d), layer)
    return pl.pallas_call(
        _postmix_kernel,
        grid=(m // tm,),
        in_specs=[row(d), row(D_A), row(D_B), row(D_C),
                  pl.BlockSpec((None, tm, dp), lambda i: (layer, i, 0)),
                  _layer_spec((1, D_A), layer),
                  _layer_spec((D_A, d), layer, (0, 0)),
                  _layer_spec((D_B, d), layer, (D_A // D_B, 0)),
                  _layer_spec((D_C, d), layer, ((D_A + D_B) // D_C, 0)), vec,
                  vec, _layer_spec((d, ff), layer), _layer_spec((d, ff), layer),
                  _layer_spec((ff, d), layer), vec,
                  vec, _layer_spec((d, d), layer), _layer_spec((dp, d), layer), vec],
        out_specs=row(d),
        out_shape=jax.ShapeDtypeStruct((m, d), F32),
        scratch_shapes=[pltpu.VMEM((tm, d), F32), pltpu.VMEM((tm, d), F32)],
        compiler_params=_params(("parallel",)),
        name="postmix",
    )(x, ya, yb, yc, pe_all, sw['g_out_a'], sw['w_out'], sw['w_out'], sw['w_out'], sw['g_mix_post'],
      sw['g_f2_pre'], sw['w_f2_gate'], sw['w_f2_up'], sw['w_f2_down'], sw['g_f2_post'],
      sw['g_ple_pre'], sw['w_ple_gate'], sw['w_ple_proj'], sw['g_ple_post'])


ATT_QT = 2048
ATT_UNROLL = 8


ATT_PITCH = LOCAL_BLK + 8


def _attn_prompt_kernel(q_ref, k_ref, v_ref, qc_ref, kc_ref, vc_ref, bias_ref, o_ref,
                        num_scr, m_scr, den_scr, cls_scr):
    t0 = pl.program_id(2) * ATT_QT
    blk = LOCAL_BLK
    lo_head = lax.broadcasted_iota(jnp.int32, (blk, LANES), 1) < HEAD_DIM
    scale = HEAD_DIM ** -0.5 * 1.4426950408889634
    n_cfg = len(DILATED_CONFIGS)
    dil_c = DILATED_CONFIGS[n_cfg - 1][1]

    for ci, (_, dil) in enumerate(DILATED_CONFIGS):
        span = blk * dil
        class_major = span == ATT_QT

        def rows_of(j, dil=dil, span=span, class_major=class_major):
            if class_major:
                tile = pl.program_id(2)
                first = tile == 0
                cur = pl.multiple_of(tile * blk, 8)
                prev = jnp.where(first, cur, cur - blk)
                return ((j,), (j, pl.ds(cur, blk)), (j, pl.ds(pl.multiple_of(prev, 8), blk)), first)
            qs = j % dil + span * (j // dil)
            ks = t0 + qs
            first = ks < span
            ps = jnp.where(first, ks, ks - span)
            return ((pl.ds(qs, blk, stride=dil),), (pl.ds(ks, blk, stride=dil),),
                    (pl.ds(ps, blk, stride=dil),), first)

        q_src, k_src, v_src = (qc_ref, kc_ref, vc_ref) if class_major else (q_ref, k_ref, v_ref)

        def scores(j, rows_of=rows_of, q_src=q_src, k_src=k_src):
            q_rows, cur, prev, first = rows_of(j)
            q2 = q_src[q_rows] * scale
            kc = k_src[cur].astype(BF16)
            kp = k_src[prev].astype(BF16)
            q_both = jnp.concatenate([jnp.where(lo_head, q2, 0.0), jnp.where(lo_head, 0.0, q2)], axis=0)
            raw = _dot_nt(q_both.astype(BF16), jnp.concatenate([kp, kc], axis=0))
            return j, q_rows, cur, prev, first.astype(jnp.int32), raw

        def softmax(first, raw):
            s = raw + bias_ref[first]
            m = jnp.max(s, axis=-1, keepdims=True)
            p = jnp.exp2(s - m)
            return p.astype(BF16), m, jnp.sum(p, axis=-1, keepdims=True)

        def weighted_values(j, q_rows, cur, prev, probs, ci=ci, class_major=class_major, v_src=v_src):
            vc = v_src[cur].astype(BF16)
            vp = v_src[prev].astype(BF16)
            p_both, m, den = probs
            num = _dot(p_both, jnp.concatenate([vp, vc], axis=0))
            num = jnp.where(lo_head, num[:blk], num[blk:])
            m = jnp.where(lo_head, m[:blk], m[blk:])
            den = jnp.where(lo_head, den[:blk], den[blk:])
            if class_major:
                dst = pl.ds(pl.multiple_of(j * ATT_PITCH, 8), blk)
                cls_scr[0, dst, :] = num
                cls_scr[1, dst, :] = m
                cls_scr[2, dst, :] = den
            else:
                num_scr[(ci,) + q_rows] = num
                m_scr[(ci,) + q_rows] = m
                den_scr[(ci,) + q_rows] = den

        def body(jj, carry, scores=scores, softmax=softmax, weighted_values=weighted_values):
            staged = [scores(jj * ATT_UNROLL + u) for u in range(ATT_UNROLL)]
            probs = [softmax(st[4], st[5]) for st in staged]
            for st, pr in zip(staged, probs):
                weighted_values(st[0], st[1], st[2], st[3], pr)
            return carry

        lax.fori_loop(0, ATT_QT // blk // ATT_UNROLL, body, 0)

    rows = 256

    def combine(i, carry):
        sl = pl.ds(pl.multiple_of(i * rows, rows), rows)

        def class_rows(a):
            base = i * (rows // dil_c)
            return jnp.concatenate(
                [cls_scr[a, pl.ds(base + g, dil_c, stride=ATT_PITCH), :] for g in range(rows // dil_c)],
                axis=0)

        ms = [m_scr[0, sl, :], m_scr[1, sl, :], class_rows(1)]
        nums = [num_scr[0, sl, :], num_scr[1, sl, :], class_rows(0)]
        dens = [den_scr[0, sl, :], den_scr[1, sl, :], class_rows(2)]
        mx = jnp.maximum(jnp.maximum(ms[0], ms[1]), ms[2])
        ws = [jnp.exp2(mc - mx) for mc in ms]
        num = nums[0] * ws[0] + nums[1] * ws[1] + nums[2] * ws[2]
        den = dens[0] * ws[0] + dens[1] * ws[1] + dens[2] * ws[2]
        o_ref[sl, :] = num / den
        return carry

    lax.fori_loop(0, ATT_QT // rows, combine, 0)


def _attn_bias():
    blk = LOCAL_BLK
    qi = jnp.arange(2 * blk)[:, None] % blk
    kj = jnp.arange(2 * blk)[None, :]
    prev_ok = (kj < blk) & (kj >= qi)
    cur_ok = (kj >= blk) & (kj - blk <= qi)
    normal = jnp.where(prev_ok | cur_ok, 0.0, NEG)
    first = jnp.where(cur_ok, 0.0, NEG)
    return jnp.stack([normal, first]).astype(F32)


def _attn_prompt(z3, zc):
    b, s, _ = z3.shape
    n_pairs = D_A // LANES
    dil_c = DILATED_CONFIGS[-1][1]
    assert dil_c * LOCAL_BLK == ATT_QT and len(DILATED_CONFIGS) == 3
    q_spec = pl.BlockSpec((None, ATT_QT, LANES), lambda i, p, t: (i, t, ZC_QA + p))
    k_spec = pl.BlockSpec((None, s, LANES), lambda i, p, t: (i, 0, ZC_KA + p))
    v_spec = pl.BlockSpec((None, s, LANES), lambda i, p, t: (i, 0, ZC_VA + p))
    qc_spec = pl.BlockSpec((None, dil_c, LOCAL_BLK, LANES), lambda i, p, t: (i, 0, t, ZC_QA + p))
    kc_spec = pl.BlockSpec((None, dil_c, s // dil_c, LANES), lambda i, p, t: (i, 0, 0, ZC_KA + p))
    vc_spec = pl.BlockSpec((None, dil_c, s // dil_c, LANES), lambda i, p, t: (i, 0, 0, ZC_VA + p))
    return pl.pallas_call(
        _attn_prompt_kernel,
        grid=(b, n_pairs, s // ATT_QT),
        in_specs=[q_spec, k_spec, v_spec, qc_spec, kc_spec, vc_spec,
                  _const_spec((2, 2 * LOCAL_BLK, 2 * LOCAL_BLK))],
        out_specs=pl.BlockSpec((None, ATT_QT, LANES), lambda i, p, t: (i, t, p)),
        out_shape=jax.ShapeDtypeStruct((b, s, D_A), F32),
        scratch_shapes=[pltpu.VMEM((2, ATT_QT, LANES), F32)] * 3
        + [pltpu.VMEM((3, dil_c * ATT_PITCH, LANES), F32)],
        compiler_params=_params(("parallel", "parallel", "arbitrary")),
        name="attn_prompt",
    )(z3, z3, z3, zc, zc, zc, _attn_bias())


def _attn_sample_kernel(q_ref, kn_ref, vn_ref, kt_ref, vt_ref, o_ref, q_scr, kn_scr, vn_scr):
    t_new = q_ref.shape[0]
    w_buf = kt_ref.shape[1]
    q_scr[...] = jnp.zeros(q_scr.shape, F32)
    kn_scr[...] = jnp.zeros(kn_scr.shape, F32)
    vn_scr[...] = jnp.zeros(vn_scr.shape, F32)
    q_scr[0:t_new, :] = q_ref[...] * (HEAD_DIM ** -0.5)
    kn_scr[0:t_new, :] = kn_ref[...]
    vn_scr[0:t_new, :] = vn_ref[...]

    rows = 2 * SUBLANES
    tq = lax.broadcasted_iota(jnp.int32, (rows, w_buf), 0) & (SUBLANES - 1)
    dist = w_buf + tq - lax.broadcasted_iota(jnp.int32, (rows, w_buf), 1)
    tqn = lax.broadcasted_iota(jnp.int32, (rows, LANES), 0) & (SUBLANES - 1)
    tn = lax.broadcasted_iota(jnp.int32, (rows, LANES), 1)
    dist_new = tqn - tn
    cache_ok, new_ok = [], []
    for window, dil in DILATED_CONFIGS:
        ok = dist <= window
        nk = jnp.logical_and(dist_new >= 0, tn < t_new)
        nk = jnp.logical_and(nk, dist_new <= window)
        if dil > 1:
            ok = jnp.logical_and(ok, (dist & (dil - 1)) == 0)
            nk = jnp.logical_and(nk, (dist_new & (dil - 1)) == 0)
        cache_ok.append(ok)
        new_ok.append(nk)
    lo8 = lax.broadcasted_iota(jnp.int32, (SUBLANES, LANES), 1) < HEAD_DIM

    for p in range(D_A // LANES):
        sl = slice(p * LANES, (p + 1) * LANES)
        q2 = q_scr[:, sl]
        qm = jnp.concatenate([jnp.where(lo8, q2, 0.0), jnp.where(lo8, 0.0, q2)], axis=0).astype(BF16)
        kb = kt_ref[sl, :].astype(BF16)
        vb = vt_ref[sl, :].astype(BF16)
        s = _dot(qm, kb)
        s_new = _dot_nt(qm, kn_scr[:, sl].astype(BF16))
        parts = []
        for c in range(len(DILATED_CONFIGS)):
            sc = jnp.where(cache_ok[c], s, NEG)
            sn = jnp.where(new_ok[c], s_new, NEG)
            m = jnp.maximum(jnp.max(sc, axis=1, keepdims=True), jnp.max(sn, axis=1, keepdims=True))
            pc = jnp.exp(sc - m)
            pn = jnp.exp(sn - m)
            den = jnp.sum(pc, axis=1, keepdims=True) + jnp.sum(pn, axis=1, keepdims=True)
            parts.append((pc, pn, m, den))
        mx = jnp.maximum(jnp.maximum(parts[0][2], parts[1][2]), parts[2][2])
        ws = [jnp.exp(pt[2] - mx) for pt in parts]
        den_all = parts[0][3] * ws[0] + parts[1][3] * ws[1] + parts[2][3] * ws[2]
        coefs = [w / den_all for w in ws]
        p_all = parts[0][0] * coefs[0] + parts[1][0] * coefs[1] + parts[2][0] * coefs[2]
        pn_all = parts[0][1] * coefs[0] + parts[1][1] * coefs[1] + parts[2][1] * coefs[2]
        o16 = _dot_nt(p_all.astype(BF16), vb) + _dot(pn_all.astype(BF16), vn_scr[:, sl].astype(BF16))
        o8 = jnp.where(lo8, o16[0:SUBLANES], o16[SUBLANES:rows])
        o_ref[:, sl] = o8[0:t_new]


def _attn_sample(zs3, kt_all, vt_all, layer):
    b, t_new, _ = zs3.shape
    w_buf = kt_all.shape[-1]

    def new_spec(col):
        return pl.BlockSpec((None, t_new, D_A), lambda i: (i, 0, col))

    cache_spec = pl.BlockSpec((None, None, D_A, w_buf), lambda i: (layer, i, 0, 0))
    return pl.pallas_call(
        _attn_sample_kernel,
        grid=(b,),
        in_specs=[new_spec(0), new_spec(1), new_spec(2), cache_spec, cache_spec],
        out_specs=pl.BlockSpec((None, t_new, D_A), lambda i: (i, 0, 0)),
        out_shape=jax.ShapeDtypeStruct((b, t_new, D_A), F32),
        scratch_shapes=[pltpu.VMEM((SUBLANES, D_A), F32), pltpu.VMEM((LANES, D_A), F32),
                        pltpu.VMEM((LANES, D_A), F32)],
        compiler_params=_params(("parallel",)),
        name="attn_sample",
    )(zs3, zs3, zs3, kt_all, vt_all)


RG_GROUP = SUBLANES
RG_CHUNK = 256
RG_PAD = 8


def _rg_pitch(tcp):
    tiles = tcp // 8 + 1
    return 8 * (tiles if tiles % 2 else tiles + 1)


def _rglru_kernel(xb_ref, gb_ref, conv_ref, h0_ref, cw_ref, cb_ref, wa_ref, ba_ref, wx_ref, bx_ref,
                  lam_ref, g_ref, y_ref, convnew_ref, ht_ref, xp_scr, hist_scr, h_scr, a_scr, b_scr):
    n_seq, tc, _ = xb_ref.shape
    tcp = xp_scr.shape[0] - RG_PAD
    pitch = a_scr.shape[1] // n_seq
    n_hist = CONV_W - 1
    hist = slice(RG_PAD - n_hist, RG_PAD)
    n_groups = D_B // LANES

    @pl.when(pl.program_id(1) == 0)
    def _():
        hist_scr[:, hist, :] = conv_ref[...]
        h_scr[...] = h0_ref[...]

    if tc % 8:
        xp_scr[RG_PAD:, :] = jnp.zeros((tcp, D_B), F32)
    decay = _softplus(-lam_ref[...])

    def gates(g, carry):
        xp_scr[hist, :] = hist_scr[g, hist, :]
        xp_scr[RG_PAD:RG_PAD + tc, :] = xb_ref[g]
        hist_scr[g, hist, :] = xp_scr[RG_PAD + tc - n_hist:RG_PAD + tc, :]
        xc = cb_ref[...]
        for j in range(CONV_W):
            lo = RG_PAD - n_hist + j
            xc = xc + xp_scr[lo:lo + tcp, :] * cw_ref[pl.ds(j, 1), :]
        xcb = xc.astype(BF16)
        r = jax.nn.sigmoid(_dot(xcb, wa_ref[...]) + ba_ref[...])
        gi = jax.nn.sigmoid(_dot(xcb, wx_ref[...]) + bx_ref[...])
        a = jnp.exp(-LRU_C * r * decay)
        bb = jnp.sqrt(1.0 - a * a) * (gi * xc)
        dst = pl.ds(pl.multiple_of(g * pitch, 8), tcp)
        for lg in range(n_groups):
            a_scr[lg, dst, :] = a[:, lg * LANES:(lg + 1) * LANES]
            b_scr[lg, dst, :] = bb[:, lg * LANES:(lg + 1) * LANES]
        return carry

    lax.fori_loop(0, n_seq, gates, 0)

    def step(t, hs):
        rows = pl.ds(t, n_seq, stride=pitch)
        out = []
        for lg in range(n_groups):
            h = a_scr[lg, rows, :] * hs[lg] + b_scr[lg, rows, :]
            b_scr[lg, rows, :] = h
            out.append(h)
        return tuple(out)

    hs = tuple(h_scr[:, lg * LANES:(lg + 1) * LANES] for lg in range(n_groups))
    hs = lax.fori_loop(0, tc, step, hs, unroll=min(8, tc))
    for lg in range(n_groups):
        h_scr[:, lg * LANES:(lg + 1) * LANES] = hs[lg]

    def finish(g, carry):
        src = pl.ds(pl.multiple_of(g * pitch, 8), tcp)
        hseq = jnp.concatenate([b_scr[lg, src, :] for lg in range(n_groups)], axis=1)
        y = _rms(_gelu_tanh(gb_ref[g]) * hseq[0:tc], g_ref[...])
        y_ref[g] = y
        return carry

    lax.fori_loop(0, n_seq, finish, 0)
    convnew_ref[...] = hist_scr[:, hist, :]
    ht_ref[...] = h_scr[...]


def _rglru(z3, conv_buf, h0, sw, layer):
    b, t_len, _ = z3.shape
    tc = min(RG_CHUNK, t_len)
    tcp = -(-tc // 8) * 8
    pitch = _rg_pitch(tcp)
    n_hist = CONV_W - 1
    ng = b // RG_GROUP
    vec = _layer_spec((1, D_B), layer)
    mat = _layer_spec((D_B, D_B), layer)
    seq = pl.BlockSpec((RG_GROUP, tc, D_B), lambda i, c: (i, c, 0))
    hist = pl.BlockSpec((RG_GROUP, n_hist, D_B), lambda i, c: (i, 0, 0))
    state = pl.BlockSpec((None, RG_GROUP, D_B), lambda i, c: (i, 0, 0))
    y, conv_new, h_last = pl.pallas_call(
        _rglru_kernel,
        grid=(ng, t_len // tc),
        in_specs=[pl.BlockSpec((RG_GROUP, tc, D_B), lambda i, c: (i, c, ZC_XB // 3)),
                  pl.BlockSpec((RG_GROUP, tc, D_B), lambda i, c: (i, c, ZC_GB // 3)),
                  hist, state,
                  _layer_spec((CONV_W, D_B), layer), vec, mat, vec, mat, vec, vec, vec],
        out_specs=[seq, hist, state],
        out_shape=[jax.ShapeDtypeStruct((b, t_len, D_B), F32),
                   jax.ShapeDtypeStruct((b, n_hist, D_B), F32),
                   jax.ShapeDtypeStruct((ng, RG_GROUP, D_B), F32)],
        scratch_shapes=[pltpu.VMEM((RG_PAD + tcp, D_B), F32),
                        pltpu.VMEM((RG_GROUP, RG_PAD, D_B), F32),
                        pltpu.VMEM((RG_GROUP, D_B), F32),
                        pltpu.VMEM((D_B // LANES, RG_GROUP * pitch, LANES), F32),
                        pltpu.VMEM((D_B // LANES, RG_GROUP * pitch, LANES), F32)],
        compiler_params=_params(("parallel", "arbitrary")),
        name="rglru",
    )(z3, z3, conv_buf, h0.reshape(ng, RG_GROUP, D_B), sw['conv_w'], sw['conv_b'], sw['w_rg_a'],
      sw['b_rg_a'], sw['w_rg_x'], sw['b_rg_x'], sw['lru_lambda'], sw['g_out_b'])
    return y, conv_new, h_last.reshape(b, D_B)


ML_GROUP = 8


def _split3(x, axis):
    hi = x.astype(BF16).astype(F32)
    r1 = x - hi
    mid = r1.astype(BF16).astype(F32)
    lo = (r1 - mid).astype(BF16).astype(F32)
    return jnp.concatenate([hi, mid, lo], axis=axis).astype(BF16)


def _sum3(x, axis):
    n = x.shape[axis] // 3
    if axis == 0:
        return x[0:n] + x[n:2 * n] + x[2 * n:3 * n]
    return x[:, 0:n] + x[:, n:2 * n] + x[:, 2 * n:3 * n]


def _mlstm_kernel(n_valid, q0_ref, q1_ref, k0_ref, k1_ref, v0_ref, v1_ref, o0_ref, o1_ref,
                  gr_ref, bias_ref, gain_ref, tri_ref, trit_ref, sel_ref,
                  c0_ref, n0_ref, m0_ref, y_ref, c_out_ref, n_out_ref, m_out_ref,
                  c_scr, n_scr, m_scr):
    ck = pl.program_id(1)
    n_seq, _, L = gr_ref.shape

    lane = lax.broadcasted_iota(jnp.int32, (L, LANES), 1)
    row = lax.broadcasted_iota(jnp.int32, (L, LANES), 0)
    lo_head = lane < HEAD_DIM
    causal = lane <= row
    row_lo = row < HEAD_DIM
    eye = lane == row
    blockdiag = ((row ^ lane) & HEAD_DIM) == 0
    sub = lax.broadcasted_iota(jnp.int32, (SUBLANES, L), 0)
    tl = lax.broadcasted_iota(jnp.int32, (SUBLANES, L), 1)
    ones_b = jnp.ones((L, LANES), BF16)
    refs = ((q0_ref, k0_ref, v0_ref, o0_ref), (q1_ref, k1_ref, v1_ref, o1_ref))
    units = [(sq, p) for sq in range(n_seq) for p in range(2)]

    @pl.when(ck == 0)
    def _():
        zero = jnp.zeros((HEAD_DIM, HEAD_DIM), F32)
        ones3 = jnp.ones((3 * LANES, LANES), BF16)
        for sq, p in units:
            top = jnp.concatenate([c0_ref[sq, 2 * p], zero], axis=1)
            bot = jnp.concatenate([zero, c0_ref[sq, 2 * p + 1]], axis=1)
            c_scr[sq, p] = jnp.concatenate([top, bot], axis=0)
            n_row = jnp.concatenate([n0_ref[sq, pl.ds(2 * p, 1), :],
                                     n0_ref[sq, pl.ds(2 * p + 1, 1), :]], axis=1)
            n_scr[sq, p] = _dot(_split3(jnp.where(eye, n_row, 0.0), 1), ones3)
        m_scr[...] = m0_ref[...]

    def gate_stage(sq):
        xr = gr_ref[sq] + bias_ref[...]
        gt_r = jnp.where(sub < N_HEADS_C, xr, _log_sigmoid(xr))
        if n_valid < L:
            gt_r = jnp.where(tl < n_valid, gt_r, jnp.where(sub < N_HEADS_C, NEG, 0.0))
        gt_c = jnp.transpose(jnp.concatenate([gt_r, jnp.zeros((L - SUBLANES, L), F32)], axis=0))
        b_col = _sum3(_dot(tri_ref[...], _split3(gt_c, 1)), 1)
        b_row = _sum3(_dot(_split3(gt_r, 0), trit_ref[...]), 0)
        cols = _dot(_split3(jnp.where(lane < N_HEADS_C, gt_c, b_col), 1), sel_ref[...])
        return gt_r, b_row, cols

    def score_stage(sq, p):
        q_ref, k_ref, v_ref, _ = refs[p]
        q2 = q_ref[sq]
        k2 = k_ref[sq] * (HEAD_DIM ** -0.5)
        v2b = v_ref[sq].astype(BF16)
        k2b = k2.astype(BF16)
        c_pair = c_scr[sq, p]
        n_pair = n_scr[sq, p]
        cn_b = jnp.concatenate([c_pair, n_pair], axis=1).astype(BF16)
        heads = []
        for h in range(2):
            qh = jnp.where(lo_head if h == 0 else jnp.logical_not(lo_head), q2, 0.0).astype(BF16)
            heads.append((_dot_nt(qh, k2b), _dot(qh, cn_b)))
        return k2, v2b, c_pair, n_pair, heads

    def weight_stage(sq, p, gates, staged):
        gt_r, b_row, cols = gates
        heads = staged[4]
        out = []
        for h in range(2):
            gh = 2 * p + h
            s_h, qx = heads[h]
            ig_c = cols[:, gh * LANES:(gh + 1) * LANES]
            b_c = cols[:, (N_HEADS_C + gh) * LANES:(N_HEADS_C + gh + 1) * LANES]
            b_r = b_row[N_HEADS_C + gh:N_HEADS_C + gh + 1, :]
            ig_r = gt_r[gh:gh + 1, :]
            m_prev = m_scr[sq, pl.ds(gh, 1), :]
            dm = jnp.where(causal, b_c - b_r + ig_r, NEG)
            inter = b_c + m_prev
            mt = jnp.maximum(inter, jnp.max(dm, axis=1, keepdims=True))
            wi = jnp.exp(inter - mt)
            a = jnp.exp(dm - mt) * s_h
            den = jnp.sum(a, axis=1, keepdims=True) + wi * qx[:, LANES:]
            inv = 1.0 / jnp.maximum(jnp.abs(den), jnp.exp(-mt))
            m_new = mt[L - 1:L, :]
            b_last = b_c[L - 1:L, :]
            ws = jnp.exp(b_last - b_c + ig_c - m_new)
            wc = jnp.exp(b_last + m_prev - m_new)
            out.append((a.astype(BF16), wi * qx[:, :LANES], inv, ws, wc, m_new))
        return out

    def value_stage(sq, p, staged, weights):
        o_ref = refs[p][3]
        k2, v2b, c_pair, n_pair, _ = staged
        (a0, qc0, inv0, ws0, wc0, mn0), (a1, qc1, inv1, ws1, wc1, mn1) = weights
        h0 = (_dot(a0, v2b) + qc0) * inv0
        h1 = (_dot(a1, v2b) + qc1) * inv1
        hout2 = jnp.where(lo_head, h0, h1)
        kw = k2 * jnp.where(lo_head, ws0, ws1)
        upd = _dot_tn(kw.astype(BF16), jnp.concatenate([v2b, ones_b], axis=1))
        keep = jnp.where(row_lo, wc0, wc1)
        c_scr[sq, p] = jnp.where(blockdiag, keep * c_pair + upd[:, :LANES], 0.0)
        n_scr[sq, p] = keep * n_pair + upd[:, LANES:]
        m_scr[sq, pl.ds(2 * p, 1), :] = mn0
        m_scr[sq, pl.ds(2 * p + 1, 1), :] = mn1
        hc = jax.nn.sigmoid(o_ref[sq]) * hout2
        hsq = hc * hc
        ms0 = jnp.sum(jnp.where(lo_head, hsq, 0.0), axis=1, keepdims=True)
        ms1 = jnp.sum(jnp.where(lo_head, 0.0, hsq), axis=1, keepdims=True)
        ms = jnp.where(lo_head, ms0, ms1) * (1.0 / HEAD_DIM)
        y_ref[sq, :, p * LANES:(p + 1) * LANES] = \
            hc * lax.rsqrt(ms + EPS) * gain_ref[:, p * LANES:(p + 1) * LANES]

    gates = [gate_stage(sq) for sq in range(n_seq)]
    staged = [score_stage(sq, p) for sq, p in units]
    weights = [weight_stage(sq, p, gates[sq], st) for (sq, p), st in zip(units, staged)]
    for (sq, p), st, wt in zip(units, staged, weights):
        value_stage(sq, p, st, wt)

    @pl.when(ck == pl.num_programs(1) - 1)
    def _():
        for sq, p in units:
            c_pair = c_scr[sq, p]
            c_out_ref[sq, 2 * p] = c_pair[0:HEAD_DIM, 0:HEAD_DIM]
            c_out_ref[sq, 2 * p + 1] = c_pair[HEAD_DIM:, HEAD_DIM:]
            n_out_ref[sq, pl.ds(p, 1), :] = jnp.sum(jnp.where(eye, n_scr[sq, p], 0.0),
                                                    axis=0, keepdims=True)
        m_out_ref[...] = m_scr[...]


def _mlstm_consts():
    L = MLSTM_CHUNK
    t = jnp.arange(L)
    tri = (t[None, :] <= t[:, None]).astype(BF16)
    src = jnp.arange(3 * LANES) % LANES
    dst = jnp.arange(N_GATES * LANES) // LANES
    sel = (src[:, None] == dst[None, :]).astype(BF16)
    return tri, tri.T, sel


def _mlstm(z3, col0, g_rows, n_valid, sw, layer, C0, n0, m0):
    b, t_len, _ = z3.shape
    L = MLSTM_CHUNK
    nc = t_len // L
    G = ML_GROUP
    tri, tri_t, sel = _mlstm_consts()
    m_rows = jnp.broadcast_to(jnp.pad(m0, ((0, 0), (0, SUBLANES - N_HEADS_C)))[:, :, None],
                              (b, SUBLANES, LANES))

    def col(cb):
        return pl.BlockSpec((G, L, LANES), lambda i, c: (i, c, cb - col0))

    state_c = pl.BlockSpec((G, N_HEADS_C, HEAD_DIM, HEAD_DIM), lambda i, c: (i, 0, 0, 0))
    state_m = pl.BlockSpec((G, SUBLANES, LANES), lambda i, c: (i, 0, 0))
    y, c1, n1, m1 = pl.pallas_call(
        functools.partial(_mlstm_kernel, n_valid),
        grid=(b // G, nc),
        in_specs=[col(ZC_QC), col(ZC_QC + 1), col(ZC_KC), col(ZC_KC + 1), col(ZC_VC), col(ZC_VC + 1),
                  col(ZC_OC), col(ZC_OC + 1),
                  pl.BlockSpec((G, SUBLANES, L), lambda i, c: (i, 0, c)),
                  _layer_spec((SUBLANES, 1), layer), _layer_spec((1, D_C), layer),
                  _const_spec((L, L)), _const_spec((L, L)), _const_spec((3 * LANES, N_GATES * LANES)),
                  state_c, pl.BlockSpec((G, N_HEADS_C, HEAD_DIM), lambda i, c: (i, 0, 0)), state_m],
        out_specs=[pl.BlockSpec((G, L, D_C), lambda i, c: (i, c, 0)), state_c,
                   pl.BlockSpec((G, 2, LANES), lambda i, c: (i, 0, 0)), state_m],
        out_shape=[jax.ShapeDtypeStruct((b, t_len, D_C), F32),
                   jax.ShapeDtypeStruct((b, N_HEADS_C, HEAD_DIM, HEAD_DIM), F32),
                   jax.ShapeDtypeStruct((b, 2, LANES), F32),
                   jax.ShapeDtypeStruct((b, SUBLANES, LANES), F32)],
        scratch_shapes=[pltpu.VMEM((G, 2, LANES, LANES), F32), pltpu.VMEM((G, 2, LANES, LANES), F32),
                        pltpu.VMEM((G, SUBLANES, LANES), F32)],
        compiler_params=_params(("parallel", "arbitrary")),
        name="mlstm",
    )(*([z3] * 8), g_rows, sw['b_mlstm_gate'], sw['g_out_c'], tri, tri_t, sel, C0, n0, m_rows)
    return y, c1, n1.reshape(b, N_HEADS_C, HEAD_DIM), m1[:, 0:N_HEADS_C, 0]


def _mlstm_t_kernel(n_valid, q0_ref, q1_ref, k0_ref, k1_ref, v0_ref, v1_ref, o0_ref, o1_ref,
                    gr_ref, bias_ref, gain_ref, trit_ref, sel_ref,
                    c0_ref, n0_ref, m0_ref, y_ref, c_out_ref, n_out_ref, m_out_ref,
                    ct_scr, n_scr, m_scr):
    ck = pl.program_id(1)
    n_seq, _, L = gr_ref.shape

    lane = lax.broadcasted_iota(jnp.int32, (L, LANES), 1)
    row = lax.broadcasted_iota(jnp.int32, (L, LANES), 0)
    lo_head = lane < HEAD_DIM
    src_le_dst = row <= lane
    row_lo = row < HEAD_DIM
    blockdiag = ((row ^ lane) & HEAD_DIM) == 0
    sub = lax.broadcasted_iota(jnp.int32, (SUBLANES, L), 0)
    tl = lax.broadcasted_iota(jnp.int32, (SUBLANES, L), 1)
    lo_row = tl[0:1, :] < HEAD_DIM
    is_head_row = sub < N_HEADS_C
    refs = ((q0_ref, k0_ref, v0_ref, o0_ref), (q1_ref, k1_ref, v1_ref, o1_ref))
    units = [(sq, p) for sq in range(n_seq) for p in range(2)]
    pad_rows = jnp.zeros((L - SUBLANES, L), F32)

    @pl.when(ck == 0)
    def _():
        zero = jnp.zeros((HEAD_DIM, HEAD_DIM), F32)
        n_scr[...] = jnp.zeros(n_scr.shape, F32)
        for sq, p in units:
            top = jnp.concatenate([c0_ref[sq, 2 * p], zero], axis=1)
            bot = jnp.concatenate([zero, c0_ref[sq, 2 * p + 1]], axis=1)
            ct_scr[sq, p] = jnp.transpose(jnp.concatenate([top, bot], axis=0))
            n_scr[sq, pl.ds(p, 1), :] = jnp.concatenate(
                [n0_ref[sq, pl.ds(2 * p, 1), :], n0_ref[sq, pl.ds(2 * p + 1, 1), :]], axis=1)
        m_scr[...] = m0_ref[...]

    def cumsum_stage(sq):
        xr = gr_ref[sq] + bias_ref[...]
        gt = jnp.where(is_head_row, xr, _log_sigmoid(xr))
        if n_valid < L:
            gt = jnp.where(tl < n_valid, gt, jnp.where(is_head_row, NEG, 0.0))
        return gt, _sum3(_dot(_split3(gt, 0), trit_ref[...]), 0)

    def gate_stage(sq, gt, csum):
        b = pltpu.roll(csum, N_HEADS_C, axis=0)
        e = gt - b
        cm = e
        shift = 1
        while shift < L:
            cm = jnp.maximum(cm, jnp.where(tl >= shift, pltpu.roll(cm, shift, axis=1), NEG))
            shift *= 2
        m_prev = m_scr[sq]
        g = jnp.maximum(m_prev, cm)
        mt = g + b
        wi = jnp.exp(m_prev - g)
        emt = jnp.exp(-mt)
        m_new = jnp.broadcast_to(mt[:, L - 1:L], (SUBLANES, L))
        b_last = jnp.broadcast_to(b[:, L - 1:L], (SUBLANES, L))
        ws = jnp.exp(e + b_last - m_new)
        wc = jnp.exp(b_last + m_prev - m_new)
        both = jnp.where(is_head_row, e, pltpu.roll(ws, N_HEADS_C, axis=0))
        return g, wi, emt, wc, m_new, _split3(jnp.transpose(jnp.concatenate([both, pad_rows], axis=0)), 1)

    def score_stage(sq, p):
        q_ref, k_ref, v_ref, _ = refs[p]
        q2 = q_ref[sq]
        k2 = k_ref[sq] * (HEAD_DIM ** -0.5)
        q2b = q2.astype(BF16)
        k2b = k2.astype(BF16)
        vt_b = jnp.transpose(v_ref[sq]).astype(BF16)
        ct = ct_scr[sq, p]
        n_pair = n_scr[sq, pl.ds(p, 1), :]
        n2 = jnp.where(jnp.logical_or(jnp.logical_and(sub == 0, tl < HEAD_DIM),
                                      jnp.logical_and(sub == 1, tl >= HEAD_DIM)), n_pair, 0.0)
        qct = _dot_nt(ct.astype(BF16), q2b)
        qn = _dot_nt(n2.astype(BF16), q2b)
        st = [_dot_nt(k2b, jnp.where(lo_head if h == 0 else jnp.logical_not(lo_head), q2, 0.0)
                      .astype(BF16)) for h in range(2)]
        return k2, vt_b, ct, n_pair, qct, qn, st

    def weight_stage(p, gates, staged):
        g, wi, emt, _, _, cols = gates
        qn, st = staged[5], staged[6]
        out = []
        for h in range(2):
            gh = 2 * p + h
            e_c = cols[:, gh * LANES:(gh + 1) * LANES]
            at = jnp.exp(jnp.where(src_le_dst, e_c - g[gh:gh + 1, :], NEG)) * st[h]
            den = jnp.sum(at, axis=0, keepdims=True) + wi[gh:gh + 1, :] * qn[h:h + 1, :]
            inv = 1.0 / jnp.maximum(jnp.abs(den), emt[gh:gh + 1, :])
            out.append((at.astype(BF16), inv))
        return out

    def value_stage(sq, p, gates, staged, weights):
        o_ref = refs[p][3]
        _, wi, _, wc, m_new, cols = gates
        k2, vt_b, ct, n_pair, qct, _, _ = staged
        (at0, inv0), (at1, inv1) = weights
        g0, g1 = 2 * p, 2 * p + 1
        num_t = jnp.where(row_lo, _dot(vt_b, at0), _dot(vt_b, at1))
        wi2 = jnp.where(row_lo, wi[g0:g0 + 1, :], wi[g1:g1 + 1, :])
        inv2 = jnp.where(row_lo, inv0, inv1)
        hout = jnp.transpose((num_t + wi2 * qct) * inv2)
        ws2 = jnp.where(lo_head, cols[:, (N_HEADS_C + g0) * LANES:(N_HEADS_C + g0 + 1) * LANES],
                        cols[:, (N_HEADS_C + g1) * LANES:(N_HEADS_C + g1 + 1) * LANES])
        kw = k2 * ws2
        keep_t = jnp.where(row_lo, wc[g0:g0 + 1, :], wc[g1:g1 + 1, :])
        ct_scr[sq, p] = jnp.where(blockdiag, keep_t * ct + _dot(vt_b, kw.astype(BF16)), 0.0)
        n_scr[sq, pl.ds(p, 1), :] = jnp.where(lo_row, wc[g0:g0 + 1, :], wc[g1:g1 + 1, :]) * n_pair \
            + jnp.sum(kw, axis=0, keepdims=True)
        hc = jax.nn.sigmoid(o_ref[sq]) * hout
        hsq = hc * hc
        ms0 = jnp.sum(jnp.where(lo_head, hsq, 0.0), axis=1, keepdims=True)
        ms1 = jnp.sum(jnp.where(lo_head, 0.0, hsq), axis=1, keepdims=True)
        ms = jnp.where(lo_head, ms0, ms1) * (1.0 / HEAD_DIM)
        y_ref[sq, :, p * LANES:(p + 1) * LANES] = \
            hc * lax.rsqrt(ms + EPS) * gain_ref[:, p * LANES:(p + 1) * LANES]

    sums = [cumsum_stage(sq) for sq in range(n_seq)]
    staged = [score_stage(sq, p) for sq, p in units]
    gates = [gate_stage(sq, *sums[sq]) for sq in range(n_seq)]
    gates = [gt[:5] + (_dot(gt[5], sel_ref[...]),) for gt in gates]
    weights = [weight_stage(p, gates[sq], st) for (sq, p), st in zip(units, staged)]
    for (sq, p), st, wt in zip(units, staged, weights):
        value_stage(sq, p, gates[sq], st, wt)
    for sq in range(n_seq):
        m_scr[sq] = jnp.where(is_head_row, gates[sq][4], 0.0)

    @pl.when(ck == pl.num_programs(1) - 1)
    def _():
        for sq, p in units:
            c_pair = jnp.transpose(ct_scr[sq, p])
            c_out_ref[sq, 2 * p] = c_pair[0:HEAD_DIM, 0:HEAD_DIM]
            c_out_ref[sq, 2 * p + 1] = c_pair[HEAD_DIM:, HEAD_DIM:]
        n_out_ref[...] = n_scr[:, 0:2, :]
        m_out_ref[...] = m_scr[...]


def _mlstm_t(z3, col0, g_rows, n_valid, sw, layer, C0, n0, m0):
    b, t_len, _ = z3.shape
    L = MLSTM_CHUNK
    nc = t_len // L
    G = ML_GROUP
    _, tri_t, sel = _mlstm_consts()
    m_rows = jnp.broadcast_to(jnp.pad(m0, ((0, 0), (0, SUBLANES - N_HEADS_C)))[:, :, None],
                              (b, SUBLANES, LANES))

    def col(cb):
        return pl.BlockSpec((G, L, LANES), lambda i, c: (i, c, cb - col0))

    state_c = pl.BlockSpec((G, N_HEADS_C, HEAD_DIM, HEAD_DIM), lambda i, c: (i, 0, 0, 0))
    state_m = pl.BlockSpec((G, SUBLANES, LANES), lambda i, c: (i, 0, 0))
    y, c1, n1, m1 = pl.pallas_call(
        functools.partial(_mlstm_t_kernel, n_valid),
        grid=(b // G, nc),
        in_specs=[col(ZC_QC), col(ZC_QC + 1), col(ZC_KC), col(ZC_KC + 1), col(ZC_VC), col(ZC_VC + 1),
                  col(ZC_OC), col(ZC_OC + 1),
                  pl.BlockSpec((G, SUBLANES, L), lambda i, c: (i, 0, c)),
                  _layer_spec((SUBLANES, 1), layer), _layer_spec((1, D_C), layer),
                  _const_spec((L, L)), _const_spec((3 * LANES, N_GATES * LANES)),
                  state_c, pl.BlockSpec((G, N_HEADS_C, HEAD_DIM), lambda i, c: (i, 0, 0)), state_m],
        out_specs=[pl.BlockSpec((G, L, D_C), lambda i, c: (i, c, 0)), state_c,
                   pl.BlockSpec((G, 2, LANES), lambda i, c: (i, 0, 0)), state_m],
        out_shape=[jax.ShapeDtypeStruct((b, t_len, D_C), F32),
                   jax.ShapeDtypeStruct((b, N_HEADS_C, HEAD_DIM, HEAD_DIM), F32),
                   jax.ShapeDtypeStruct((b, 2, LANES), F32),
                   jax.ShapeDtypeStruct((b, SUBLANES, LANES), F32)],
        scratch_shapes=[pltpu.VMEM((G, 2, LANES, LANES), F32), pltpu.VMEM((G, SUBLANES, LANES), F32),
                        pltpu.VMEM((G, SUBLANES, LANES), F32)],
        compiler_params=_params(("parallel", "arbitrary")),
        name="mlstm",
    )(*([z3] * 8), g_rows, sw['b_mlstm_gate'], sw['g_out_c'], tri_t, sel, C0, n0, m_rows)
    return y, c1, n1.reshape(b, N_HEADS_C, HEAD_DIM), m1[:, 0:N_HEADS_C, 0]


def _block_diag(w):
    depth, nb, bs, _ = w.shape
    eye = jnp.eye(nb, dtype=w.dtype)
    return jnp.einsum('lncd,nm->lncmd', w, eye).reshape(depth, nb * bs, nb * bs)


def _stack_weights(p):
    vec = lambda v: v.reshape(v.shape[0], 1, -1)
    bf = lambda w: w.astype(BF16)
    out = {k: vec(p[k]) for k in (
        'g_f1_pre', 'g_f1_post', 'g_mix_pre', 'conv_b', 'b_rg_a', 'b_rg_x', 'lru_lambda', 'g_out_a',
        'g_out_b', 'g_out_c', 'g_mix_post', 'g_f2_pre', 'g_f2_post', 'g_ple_pre', 'g_ple_post')}
    out.update({k: bf(p[k]) for k in (
        'w_f1_gate', 'w_f1_up', 'w_f1_down', 'w_f2_gate', 'w_f2_up', 'w_f2_down', 'w_out',
        'w_ple_gate', 'w_ple_proj')})
    out['w_in'] = bf(jnp.pad(p['w_in'], ((0, 0), (0, 0), (0, Z_COLS - p['w_in'].shape[-1]))))
    out['conv_w'] = p['conv_w']
    out['w_rg_a'] = bf(_block_diag(p['w_rg_a']))
    out['w_rg_x'] = bf(_block_diag(p['w_rg_x']))
    out['b_mlstm_gate'] = jnp.concatenate([p['b_mlstm_i'], p['b_mlstm_f']], axis=1)[:, :, None]
    return out


def _gate_rows(z3, col0):
    lo = (ZC_GATE - col0) * LANES
    return jnp.swapaxes(z3[:, :, lo:lo + N_GATES], 1, 2)


def _layer(x, pe_all, sw, layer, cache, kv_bufs=None):
    b, t_len, d = x.shape
    if cache is None:
        rows, z, new_k, new_v, zc = _premix(x.reshape(b * t_len, d), sw, layer, kv_bufs, t_len)
        z3 = z.reshape(b, t_len, Z_COLS)
        ya = _attn_prompt(z3, zc)
        conv_buf = jnp.zeros((b, CONV_W - 1, D_B), F32)
        h0 = jnp.zeros((b, D_B), F32)
        C0 = jnp.zeros((b, N_HEADS_C, HEAD_DIM, HEAD_DIM), F32)
        n0 = jnp.zeros((b, N_HEADS_C, HEAD_DIM), F32)
        m0 = jnp.zeros((b, N_HEADS_C), F32)
        zc, col0, n_valid = z3, 0, MLSTM_CHUNK
    else:
        kt_all, vt_all, conv_buf, h0, C0, n0, m0 = cache
        rows, z = _premix(x.reshape(b * t_len, d), sw, layer)
        z3 = z.reshape(b, t_len, Z_COLS)
        new_k = z3[:, :, ZC_KA * LANES:ZC_KA * LANES + D_A].reshape(b, t_len, N_HEADS_A, HEAD_DIM)
        new_v = z3[:, :, ZC_VA * LANES:ZC_VA * LANES + D_A].reshape(b, t_len, N_HEADS_A, HEAD_DIM)
        ya = _attn_sample(z3, kt_all, vt_all, layer)
        col0 = ZC_QC
        zc = jnp.pad(z3[:, :, col0 * LANES:], ((0, 0), (0, MLSTM_CHUNK - t_len), (0, 0)))
        n_valid = t_len
    yb, new_buf, h_last = _rglru(z3, conv_buf, h0, sw, layer)
    yc, C1, n1, m1 = _mlstm_t(zc, col0, _gate_rows(zc, col0), n_valid, sw, layer, C0, n0, m0)
    yc = yc[:, :t_len]
    rows = _postmix(rows, ya.reshape(b * t_len, D_A), yb.reshape(b * t_len, D_B),
                    yc.reshape(b * t_len, D_C), pe_all, sw, layer)
    return rows.reshape(b, t_len, d), (new_k, new_v, new_buf, h_last, C1, n1, m1)


def _feature_major(cache):
    depth, b, w_buf, nh, dh = cache.shape
    return jnp.transpose(cache, (0, 1, 3, 4, 2)).reshape(depth, b, nh * dh, w_buf)


def kernel(x_prompt, x_sample, cache_k, cache_v, state_conv, state_h, state_C, state_n, state_m, p_prompt, p_sample, g_f1_pre, w_f1_gate, w_f1_up, w_f1_down, g_f1_post, g_mix_pre, w_in, conv_w, conv_b, w_rg_a, b_rg_a, w_rg_x, b_rg_x, lru_lambda, b_mlstm_i, b_mlstm_f, g_out_a, g_out_b, g_out_c, w_out, g_mix_post, g_f2_pre, w_f2_gate, w_f2_up, w_f2_down, g_f2_post, g_ple_pre, w_ple_gate, w_ple_proj, g_ple_post):
    depth = w_in.shape[0]
    sw = _stack_weights(dict(
        g_f1_pre=g_f1_pre, w_f1_gate=w_f1_gate, w_f1_up=w_f1_up, w_f1_down=w_f1_down,
        g_f1_post=g_f1_post, g_mix_pre=g_mix_pre, w_in=w_in, conv_w=conv_w, conv_b=conv_b,
        w_rg_a=w_rg_a, b_rg_a=b_rg_a, w_rg_x=w_rg_x, b_rg_x=b_rg_x, lru_lambda=lru_lambda,
        b_mlstm_i=b_mlstm_i, b_mlstm_f=b_mlstm_f, g_out_a=g_out_a, g_out_b=g_out_b,
        g_out_c=g_out_c, w_out=w_out, g_mix_post=g_mix_post, g_f2_pre=g_f2_pre,
        w_f2_gate=w_f2_gate, w_f2_up=w_f2_up, w_f2_down=w_f2_down, g_f2_post=g_f2_post,
        g_ple_pre=g_ple_pre, w_ple_gate=w_ple_gate, w_ple_proj=w_ple_proj, g_ple_post=g_ple_post))
    xp, xs = x_prompt, x_sample
    pe_prompt = p_prompt.reshape(depth, -1, p_prompt.shape[-1])
    pe_sample = p_sample.reshape(depth, -1, p_sample.shape[-1])
    kt_all = _feature_major(cache_k)
    vt_all = _feature_major(cache_v)
    b_p, s_p, _ = x_prompt.shape
    keep = min(DILATED_CONFIGS[-1][0], s_p)
    kv_bufs = (jnp.zeros((depth, b_p, D_A, keep), F32), jnp.zeros((depth, b_p, D_A, keep), F32))
    sp = [[] for _ in range(5)]
    ss = [[] for _ in range(7)]
    for i in range(depth):
        xp, st_p = _layer(xp, pe_prompt, sw, i, None, kv_bufs)
        kv_bufs = st_p[:2]
        cache_i = (kt_all, vt_all, state_conv[i], state_h[i], state_C[i], state_n[i], state_m[i])
        xs, st_s = _layer(xs, pe_sample, sw, i, cache_i)
        for j in range(5):
            sp[j].append(st_p[2 + j])
        for j in range(7):
            ss[j].append(st_s[j])
    k_prompt, v_prompt = [
        jnp.transpose(buf.reshape(depth, b_p, N_HEADS_A, HEAD_DIM, keep), (0, 1, 4, 2, 3))
        for buf in kv_bufs]
    conv_prompt, h_prompt, C_prompt, n_prompt, m_prompt = [jnp.stack(a) for a in sp]
    k_sample, v_sample, conv_sample, h_sample, C_sample, n_sample, m_sample = [jnp.stack(a) for a in ss]
    return (xp, xs, k_prompt, v_prompt, k_sample, v_sample, conv_prompt, conv_sample,
            h_prompt, h_sample, C_prompt, C_sample, n_prompt, n_sample, m_prompt, m_sample)
```

```python
import functools

import jax
import jax.numpy as jnp
from jax import lax
from jax.experimental import pallas as pl
from jax.experimental.pallas import tpu as pltpu

F32 = jnp.float32
BF16 = jnp.bfloat16

EPS = 1e-6
NEG = -1e30
HEAD_DIM = 64
N_HEADS_A = 6
D_A = N_HEADS_A * HEAD_DIM
D_B = 384
N_HEADS_C = 4
D_C = N_HEADS_C * HEAD_DIM
N_GATES = 2 * N_HEADS_C
DILATED_CONFIGS = ((128, 1), (512, 4), (2048, 16))
LOCAL_BLK = 128
CONV_W = 4
LRU_C = 8.0
MLSTM_CHUNK = 128
LANES = 128
SUBLANES = 8
Z_COLS = 3072
FF_CHUNK = 256
ROW_TILE = 512
VMEM_LIMIT = 58 * 1024 * 1024

ZC_QA, ZC_KA, ZC_VA, ZC_XB, ZC_GB = 0, 3, 6, 9, 12
ZC_QC, ZC_KC, ZC_VC, ZC_OC, ZC_GATE = 15, 17, 19, 21, 23


def _rms(x, g):
    return x * lax.rsqrt(jnp.mean(x * x, axis=-1, keepdims=True) + EPS) * g


def _dot(a, b):
    return jnp.dot(a, b, preferred_element_type=F32)


def _dot_nt(a, b):
    return lax.dot_general(a, b, (((1,), (1,)), ((), ())), preferred_element_type=F32)


def _softplus(x):
    return jnp.maximum(x, 0.0) + jnp.log1p(jnp.exp(-jnp.abs(x)))


def _log_sigmoid(x):
    return -_softplus(-x)


def _gelu_tanh(x):
    return 0.5 * x * (1.0 + jnp.tanh(0.7978845608028654 * (x + 0.044715 * (x * x * x))))


def _const_spec(shape):
    nd = len(shape)
    return pl.BlockSpec(shape, lambda *_: (0,) * nd, pipeline_mode=pl.Buffered(1))


def _layer_spec(shape, layer, block=None):
    nd = len(shape)
    idx = (0,) * nd if block is None else block
    return pl.BlockSpec((None,) + tuple(shape), lambda *_: (layer,) + idx,
                        pipeline_mode=pl.Buffered(1))


def _params(sem):
    return pltpu.CompilerParams(dimension_semantics=sem, vmem_limit_bytes=VMEM_LIMIT)


def _swiglu_into(acc_ref, h, wg_ref, wu_ref, wd_ref):
    for c in range(wg_ref.shape[1] // FF_CHUNK):
        sl = slice(c * FF_CHUNK, (c + 1) * FF_CHUNK)
        g = _dot(h, wg_ref[:, sl])
        u = _dot(h, wu_ref[:, sl])
        a = (g * jax.nn.sigmoid(g) * u).astype(BF16)
        d = _dot(a, wd_ref[sl, :])
        if c == 0:
            acc_ref[...] = d
        else:
            acc_ref[...] += d


def _premix_kernel(emit_kv, x_ref, gpre_ref, wg_ref, wu_ref, wd_ref, gpost_ref, gmix_ref, win_ref,
                   *rest):
    if emit_kv:
        _, _, xo_ref, z_ref, kt_ref, vt_ref, zc_ref, gt_ref, acc_ref, stage_ref = rest
    else:
        xo_ref, z_ref, acc_ref = rest
    x = x_ref[...]
    _swiglu_into(acc_ref, _rms(x, gpre_ref[...]).astype(BF16), wg_ref, wu_ref, wd_ref)
    x = x + 0.5 * _rms(acc_ref[...], gpost_ref[...])
    xo_ref[...] = x
    h = _rms(x, gmix_ref[...]).astype(BF16)
    for c in range(win_ref.shape[1] // 256):
        sl = slice(c * 256, (c + 1) * 256)
        zc = _dot(h, win_ref[:, sl])
        z_ref[:, sl] = zc
        if not emit_kv:
            continue
        n_cls, rows_per_cls, width = zc_ref.shape
        for half in range(2):
            lg = 2 * c + half
            if lg == ZC_GATE:
                gt_ref[...] = zc[:, half * LANES:(half + 1) * LANES].T[0:N_GATES, :]
            if lg >= width // LANES:
                continue
            cols = zc[:, half * LANES:(half + 1) * LANES]
            lanes = slice(lg * LANES, (lg + 1) * LANES)
            if ZC_KA <= lg < ZC_KA + D_A // LANES:
                kt_ref[(lg - ZC_KA) * LANES:(lg - ZC_KA + 1) * LANES, :] = cols.T
            if ZC_VA <= lg < ZC_VA + D_A // LANES:
                vt_ref[(lg - ZC_VA) * LANES:(lg - ZC_VA + 1) * LANES, :] = cols.T
            stage_ref[lg % 2] = cols
            for r in range(n_cls):
                zc_ref[r, :, lanes] = stage_ref[lg % 2, pl.ds(r, rows_per_cls, stride=n_cls), :]


def _premix(x, sw, layer, kv_bufs=None, seq_len=None):
    m, d = x.shape
    ff = sw['w_f1_gate'].shape[-1]
    tm = min(ROW_TILE, m)
    row = pl.BlockSpec((tm, d), lambda i: (i, 0))
    vec = _layer_spec((1, d), layer)
    in_specs = [row, vec, _layer_spec((d, ff), layer), _layer_spec((d, ff), layer),
                _layer_spec((ff, d), layer), vec, vec, _layer_spec((d, Z_COLS), layer)]
    out_specs = [row, pl.BlockSpec((tm, Z_COLS), lambda i: (i, 0))]
    out_shape = [jax.ShapeDtypeStruct((m, d), F32), jax.ShapeDtypeStruct((m, Z_COLS), F32)]
    args = [x, sw['g_f1_pre'], sw['w_f1_gate'], sw['w_f1_up'], sw['w_f1_down'], sw['g_f1_post'],
            sw['g_mix_pre'], sw['w_in']]
    aliases = {}
    scratch = [pltpu.VMEM((tm, d), F32)]
    if kv_bufs is not None:
        keep = kv_bufs[0].shape[-1]
        tiles_per_seq = seq_len // tm
        first_kept = (seq_len - keep) // tm
        kv_spec = pl.BlockSpec(
            (None, None, D_A, tm),
            lambda i: (layer, i // tiles_per_seq, 0, jnp.maximum(i % tiles_per_seq - first_kept, 0)))
        for buf in kv_bufs:
            aliases[len(args)] = len(out_shape)
            in_specs.append(pl.BlockSpec(memory_space=pl.ANY))
            args.append(buf)
            out_specs.append(kv_spec)
            out_shape.append(jax.ShapeDtypeStruct(buf.shape, F32))
        n_cls = DILATED_CONFIGS[-1][1]
        out_specs.append(pl.BlockSpec((None, n_cls, tm // n_cls, 3 * D_A),
                                      lambda i: (i // tiles_per_seq, 0, i % tiles_per_seq, 0)))
        out_shape.append(jax.ShapeDtypeStruct((m // seq_len, n_cls, seq_len // n_cls, 3 * D_A), F32))
        out_specs.append(pl.BlockSpec((None, N_GATES, tm),
                                      lambda i: (i // tiles_per_seq, 0, i % tiles_per_seq)))
        out_shape.append(jax.ShapeDtypeStruct((m // seq_len, N_GATES, seq_len), F32))
        scratch.append(pltpu.VMEM((2, tm, LANES), F32))
    return pl.pallas_call(
        functools.partial(_premix_kernel, kv_bufs is not None),
        grid=(m // tm,),
        in_specs=in_specs,
        out_specs=out_specs,
        out_shape=out_shape,
        scratch_shapes=scratch,
        input_output_aliases=aliases,
        compiler_params=_params(("arbitrary",)),
        name="premix",
    )(*args)


def _postmix_kernel(x_ref, ya_ref, yb_ref, yc_ref, pe_ref, ga_ref, wa_ref, wb_ref, wc_ref, gmix_ref,
                    gpre_ref, wg_ref, wu_ref, wd_ref, gpost_ref,
                    gple_ref, wpg_ref, wpp_ref, gple_post_ref, o_ref, acc_ref, x_scr):
    ya = _rms(ya_ref[...], ga_ref[...]).astype(BF16)
    y = _dot(ya, wa_ref[...])
    y = y + _dot(yb_ref[...].astype(BF16), wb_ref[...])
    y = y + _dot(yc_ref[...].astype(BF16), wc_ref[...])
    x_scr[...] = x_ref[...] + _rms(y, gmix_ref[...])
    _swiglu_into(acc_ref, _rms(x_scr[...], gpre_ref[...]).astype(BF16), wg_ref, wu_ref, wd_ref)
    x_scr[...] = x_scr[...] + 0.5 * _rms(acc_ref[...], gpost_ref[...])
    h = _rms(x_scr[...], gple_ref[...]).astype(BF16)
    gate = jax.nn.sigmoid(_dot(h, wpg_ref[...]))
    proj = _dot(pe_ref[...].astype(BF16), wpp_ref[...])
    o_ref[...] = x_scr[...] + _rms(gate * proj, gple_post_ref[...])


def _postmix(x, ya, yb, yc, pe_all, sw, layer):
    m, d = x.shape
    dp = pe_all.shape[-1]
    ff = sw['w_f2_gate'].shape[-1]
    tm = min(ROW_TILE, m)

    def row(width):
        return pl.BlockSpec((tm, width), lambda i: (i, 0))

    vec = _layer_spec((1, d), layer)
    return pl.pallas_call(
        _postmix_kernel,
        grid=(m // tm,),
        in_specs=[row(d), row(D_A), row(D_B), row(D_C),
                  pl.BlockSpec((None, tm, dp), lambda i: (layer, i, 0)),
                  _layer_spec((1, D_A), layer),
                  _layer_spec((D_A, d), layer, (0, 0)),
                  _layer_spec((D_B, d), layer, (D_A // D_B, 0)),
                  _layer_spec((D_C, d), layer, ((D_A + D_B) // D_C, 0)), vec,
                  vec, _layer_spec((d, ff), layer), _layer_spec((d, ff), layer),
                  _layer_spec((ff, d), layer), vec,
                  vec, _layer_spec((d, d), layer), _layer_spec((dp, d), layer), vec],
        out_specs=row(d),
        out_shape=jax.ShapeDtypeStruct((m, d), F32),
        scratch_shapes=[pltpu.VMEM((tm, d), F32), pltpu.VMEM((tm, d), F32)],
        compiler_params=_params(("parallel",)),
        name="postmix",
    )(x, ya, yb, yc, pe_all, sw['g_out_a'], sw['w_out'], sw['w_out'], sw['w_out'], sw['g_mix_post'],
      sw['g_f2_pre'], sw['w_f2_gate'], sw['w_f2_up'], sw['w_f2_down'], sw['g_f2_post'],
      sw['g_ple_pre'], sw['w_ple_gate'], sw['w_ple_proj'], sw['g_ple_post'])


ATT_QT = 2048
ATT_UNROLL = 8


ATT_PITCH = LOCAL_BLK + 8


def _attn_prompt_kernel(q_ref, k_ref, v_ref, qc_ref, kc_ref, vc_ref, bias_ref, o_ref,
                        num_scr, m_scr, den_scr, cls_scr):
    t0 = pl.program_id(2) * ATT_QT
    blk = LOCAL_BLK
    lo_head = lax.broadcasted_iota(jnp.int32, (blk, LANES), 1) < HEAD_DIM
    scale = HEAD_DIM ** -0.5 * 1.4426950408889634
    n_cfg = len(DILATED_CONFIGS)
    dil_c = DILATED_CONFIGS[n_cfg - 1][1]

    for ci, (_, dil) in enumerate(DILATED_CONFIGS):
        span = blk * dil
        class_major = span == ATT_QT

        def rows_of(j, dil=dil, span=span, class_major=class_major):
            if class_major:
                tile = pl.program_id(2)
                first = tile == 0
                cur = pl.multiple_of(tile * blk, 8)
                prev = jnp.where(first, cur, cur - blk)
                return ((j,), (j, pl.ds(cur, blk)), (j, pl.ds(pl.multiple_of(prev, 8), blk)), first)
            qs = j % dil + span * (j // dil)
            ks = t0 + qs
            first = ks < span
            ps = jnp.where(first, ks, ks - span)
            return ((pl.ds(qs, blk, stride=dil),), (pl.ds(ks, blk, stride=dil),),
                    (pl.ds(ps, blk, stride=dil),), first)

        q_src, k_src, v_src = (qc_ref, kc_ref, vc_ref) if class_major else (q_ref, k_ref, v_ref)

        def scores(j, rows_of=rows_of, q_src=q_src, k_src=k_src):
            q_rows, cur, prev, first = rows_of(j)
            q2 = q_src[q_rows] * scale
            kc = k_src[cur].astype(BF16)
            kp = k_src[prev].astype(BF16)
            q_both = jnp.concatenate([jnp.where(lo_head, q2, 0.0), jnp.where(lo_head, 0.0, q2)], axis=0)
            raw = _dot_nt(q_both.astype(BF16), jnp.concatenate([kp, kc], axis=0))
            return j, q_rows, cur, prev, first.astype(jnp.int32), raw

        def softmax(first, raw):
            s = raw + bias_ref[first]
            m = jnp.max(s, axis=-1, keepdims=True)
            p = jnp.exp2(s - m)
            return p.astype(BF16), m, jnp.sum(p, axis=-1, keepdims=True)

        def weighted_values(j, q_rows, cur, prev, probs, ci=ci, class_major=class_major, v_src=v_src):
            vc = v_src[cur].astype(BF16)
            vp = v_src[prev].astype(BF16)
            p_both, m, den = probs
            num = _dot(p_both, jnp.concatenate([vp, vc], axis=0))
            num = jnp.where(lo_head, num[:blk], num[blk:])
            m = jnp.where(lo_head, m[:blk], m[blk:])
            den = jnp.where(lo_head, den[:blk], den[blk:])
            if class_major:
                dst = pl.ds(pl.multiple_of(j * ATT_PITCH, 8), blk)
                cls_scr[0, dst, :] = num
                cls_scr[1, dst, :] = m
                cls_scr[2, dst, :] = den
            else:
                num_scr[(ci,) + q_rows] = num
                m_scr[(ci,) + q_rows] = m
                den_scr[(ci,) + q_rows] = den

        def body(jj, carry, scores=scores, softmax=softmax, weighted_values=weighted_values):
            staged = [scores(jj * ATT_UNROLL + u) for u in range(ATT_UNROLL)]
            probs = [softmax(st[4], st[5]) for st in staged]
            for st, pr in zip(staged, probs):
                weighted_values(st[0], st[1], st[2], st[3], pr)
            return carry

        lax.fori_loop(0, ATT_QT // blk // ATT_UNROLL, body, 0)

    rows = 256

    def combine(i, carry):
        sl = pl.ds(pl.multiple_of(i * rows, rows), rows)

        def class_rows(a):
            base = i * (rows // dil_c)
            return jnp.concatenate(
                [cls_scr[a, pl.ds(base + g, dil_c, stride=ATT_PITCH), :] for g in range(rows // dil_c)],
                axis=0)

        ms = [m_scr[0, sl, :], m_scr[1, sl, :], class_rows(1)]
        nums = [num_scr[0, sl, :], num_scr[1, sl, :], class_rows(0)]
        dens = [den_scr[0, sl, :], den_scr[1, sl, :], class_rows(2)]
        mx = jnp.maximum(jnp.maximum(ms[0], ms[1]), ms[2])
        ws = [jnp.exp2(mc - mx) for mc in ms]
        num = nums[0] * ws[0] + nums[1] * ws[1] + nums[2] * ws[2]
        den = dens[0] * ws[0] + dens[1] * ws[1] + dens[2] * ws[2]
        o_ref[sl, :] = num / den
        return carry

    lax.fori_loop(0, ATT_QT // rows, combine, 0)


def _attn_bias():
    blk = LOCAL_BLK
    qi = jnp.arange(2 * blk)[:, None] % blk
    kj = jnp.arange(2 * blk)[None, :]
    prev_ok = (kj < blk) & (kj >= qi)
    cur_ok = (kj >= blk) & (kj - blk <= qi)
    normal = jnp.where(prev_ok | cur_ok, 0.0, NEG)
    first = jnp.where(cur_ok, 0.0, NEG)
    return jnp.stack([normal, first]).astype(F32)


def _attn_prompt(z3, zc):
    b, s, _ = z3.shape
    n_pairs = D_A // LANES
    dil_c = DILATED_CONFIGS[-1][1]
    assert dil_c * LOCAL_BLK == ATT_QT and len(DILATED_CONFIGS) == 3
    q_spec = pl.BlockSpec((None, ATT_QT, LANES), lambda i, p, t: (i, t, ZC_QA + p))
    k_spec = pl.BlockSpec((None, s, LANES), lambda i, p, t: (i, 0, ZC_KA + p))
    v_spec = pl.BlockSpec((None, s, LANES), lambda i, p, t: (i, 0, ZC_VA + p))
    qc_spec = pl.BlockSpec((None, dil_c, LOCAL_BLK, LANES), lambda i, p, t: (i, 0, t, ZC_QA + p))
    kc_spec = pl.BlockSpec((None, dil_c, s // dil_c, LANES), lambda i, p, t: (i, 0, 0, ZC_KA + p))
    vc_spec = pl.BlockSpec((None, dil_c, s // dil_c, LANES), lambda i, p, t: (i, 0, 0, ZC_VA + p))
    return pl.pallas_call(
        _attn_prompt_kernel,
        grid=(b, n_pairs, s // ATT_QT),
        in_specs=[q_spec, k_spec, v_spec, qc_spec, kc_spec, vc_spec,
                  _const_spec((2, 2 * LOCAL_BLK, 2 * LOCAL_BLK))],
        out_specs=pl.BlockSpec((None, ATT_QT, LANES), lambda i, p, t: (i, t, p)),
        out_shape=jax.ShapeDtypeStruct((b, s, D_A), F32),
        scratch_shapes=[pltpu.VMEM((2, ATT_QT, LANES), F32)] * 3
        + [pltpu.VMEM((3, dil_c * ATT_PITCH, LANES), F32)],
        compiler_params=_params(("parallel", "parallel", "arbitrary")),
        name="attn_prompt",
    )(z3, z3, z3, zc, zc, zc, _attn_bias())


def _attn_sample_kernel(q_ref, kn_ref, vn_ref, kt_ref, vt_ref, o_ref, q_scr, kn_scr, vn_scr):
    t_new = q_ref.shape[0]
    w_buf = kt_ref.shape[1]
    q_scr[...] = jnp.zeros(q_scr.shape, F32)
    kn_scr[...] = jnp.zeros(kn_scr.shape, F32)
    vn_scr[...] = jnp.zeros(vn_scr.shape, F32)
    q_scr[0:t_new, :] = q_ref[...] * (HEAD_DIM ** -0.5)
    kn_scr[0:t_new, :] = kn_ref[...]
    vn_scr[0:t_new, :] = vn_ref[...]

    rows = 2 * SUBLANES
    tq = lax.broadcasted_iota(jnp.int32, (rows, w_buf), 0) & (SUBLANES - 1)
    dist = w_buf + tq - lax.broadcasted_iota(jnp.int32, (rows, w_buf), 1)
    tqn = lax.broadcasted_iota(jnp.int32, (rows, LANES), 0) & (SUBLANES - 1)
    tn = lax.broadcasted_iota(jnp.int32, (rows, LANES), 1)
    dist_new = tqn - tn
    cache_ok, new_ok = [], []
    for window, dil in DILATED_CONFIGS:
        ok = dist <= window
        nk = jnp.logical_and(dist_new >= 0, tn < t_new)
        nk = jnp.logical_and(nk, dist_new <= window)
        if dil > 1:
            ok = jnp.logical_and(ok, (dist & (dil - 1)) == 0)
            nk = jnp.logical_and(nk, (dist_new & (dil - 1)) == 0)
        cache_ok.append(ok)
        new_ok.append(nk)
    lo8 = lax.broadcasted_iota(jnp.int32, (SUBLANES, LANES), 1) < HEAD_DIM

    for p in range(D_A // LANES):
        sl = slice(p * LANES, (p + 1) * LANES)
        q2 = q_scr[:, sl]
        qm = jnp.concatenate([jnp.where(lo8, q2, 0.0), jnp.where(lo8, 0.0, q2)], axis=0).astype(BF16)
        kb = kt_ref[sl, :].astype(BF16)
        vb = vt_ref[sl, :].astype(BF16)
        s = _dot(qm, kb)
        s_new = _dot_nt(qm, kn_scr[:, sl].astype(BF16))
        parts = []
        for c in range(len(DILATED_CONFIGS)):
            sc = jnp.where(cache_ok[c], s, NEG)
            sn = jnp.where(new_ok[c], s_new, NEG)
            m = jnp.maximum(jnp.max(sc, axis=1, keepdims=True), jnp.max(sn, axis=1, keepdims=True))
            pc = jnp.exp(sc - m)
            pn = jnp.exp(sn - m)
            den = jnp.sum(pc, axis=1, keepdims=True) + jnp.sum(pn, axis=1, keepdims=True)
            parts.append((pc, pn, m, den))
        mx = jnp.maximum(jnp.maximum(parts[0][2], parts[1][2]), parts[2][2])
        ws = [jnp.exp(pt[2] - mx) for pt in parts]
        den_all = parts[0][3] * ws[0] + parts[1][3] * ws[1] + parts[2][3] * ws[2]
        coefs = [w / den_all for w in ws]
        p_all = parts[0][0] * coefs[0] + parts[1][0] * coefs[1] + parts[2][0] * coefs[2]
        pn_all = parts[0][1] * coefs[0] + parts[1][1] * coefs[1] + parts[2][1] * coefs[2]
        o16 = _dot_nt(p_all.astype(BF16), vb) + _dot(pn_all.astype(BF16), vn_scr[:, sl].astype(BF16))
        o8 = jnp.where(lo8, o16[0:SUBLANES], o16[SUBLANES:rows])
        o_ref[:, sl] = o8[0:t_new]


def _attn_sample(zs3, kt_all, vt_all, layer):
    b, t_new, _ = zs3.shape
    w_buf = kt_all.shape[-1]

    def new_spec(col):
        return pl.BlockSpec((None, t_new, D_A), lambda i: (i, 0, col))

    cache_spec = pl.BlockSpec((None, None, D_A, w_buf), lambda i: (layer, i, 0, 0))
    return pl.pallas_call(
        _attn_sample_kernel,
        grid=(b,),
        in_specs=[new_spec(0), new_spec(1), new_spec(2), cache_spec, cache_spec],
        out_specs=pl.BlockSpec((None, t_new, D_A), lambda i: (i, 0, 0)),
        out_shape=jax.ShapeDtypeStruct((b, t_new, D_A), F32),
        scratch_shapes=[pltpu.VMEM((SUBLANES, D_A), F32), pltpu.VMEM((LANES, D_A), F32),
                        pltpu.VMEM((LANES, D_A), F32)],
        compiler_params=_params(("parallel",)),
        name="attn_sample",
    )(zs3, zs3, zs3, kt_all, vt_all)


RG_GROUP = SUBLANES
RG_CHUNK = 256
RG_PAD = 8


def _rg_pitch(tcp):
    tiles = tcp // 8 + 1
    return 8 * (tiles if tiles % 2 else tiles + 1)


def _rglru_kernel(xb_ref, gb_ref, conv_ref, h0_ref, cw_ref, cb_ref, wa_ref, ba_ref, wx_ref, bx_ref,
                  lam_ref, g_ref, y_ref, convnew_ref, ht_ref, xp_scr, hist_scr, h_scr, a_scr, b_scr):
    n_seq, tc, _ = xb_ref.shape
    tcp = xp_scr.shape[0] - RG_PAD
    pitch = a_scr.shape[1] // n_seq
    n_hist = CONV_W - 1
    hist = slice(RG_PAD - n_hist, RG_PAD)
    n_groups = D_B // LANES

    @pl.when(pl.program_id(1) == 0)
    def _():
        hist_scr[:, hist, :] = conv_ref[...]
        h_scr[...] = h0_ref[...]

    if tc % 8:
        xp_scr[RG_PAD:, :] = jnp.zeros((tcp, D_B), F32)
    decay = _softplus(-lam_ref[...])

    def gates(g, carry):
        xp_scr[hist, :] = hist_scr[g, hist, :]
        xp_scr[RG_PAD:RG_PAD + tc, :] = xb_ref[g]
        hist_scr[g, hist, :] = xp_scr[RG_PAD + tc - n_hist:RG_PAD + tc, :]
        xc = cb_ref[...]
        for j in range(CONV_W):
            lo = RG_PAD - n_hist + j
            xc = xc + xp_scr[lo:lo + tcp, :] * cw_ref[pl.ds(j, 1), :]
        xcb = xc.astype(BF16)
        r = jax.nn.sigmoid(_dot(xcb, wa_ref[...]) + ba_ref[...])
        gi = jax.nn.sigmoid(_dot(xcb, wx_ref[...]) + bx_ref[...])
        a = jnp.exp(-LRU_C * r * decay)
        bb = jnp.sqrt(1.0 - a * a) * (gi * xc)
        dst = pl.ds(pl.multiple_of(g * pitch, 8), tcp)
        for lg in range(n_groups):
            a_scr[lg, dst, :] = a[:, lg * LANES:(lg + 1) * LANES]
            b_scr[lg, dst, :] = bb[:, lg * LANES:(lg + 1) * LANES]
        return carry

    lax.fori_loop(0, n_seq, gates, 0)

    def step(t, hs):
        rows = pl.ds(t, n_seq, stride=pitch)
        out = []
        for lg in range(n_groups):
            h = a_scr[lg, rows, :] * hs[lg] + b_scr[lg, rows, :]
            b_scr[lg, rows, :] = h
            out.append(h)
        return tuple(out)

    hs = tuple(h_scr[:, lg * LANES:(lg + 1) * LANES] for lg in range(n_groups))
    hs = lax.fori_loop(0, tc, step, hs, unroll=min(8, tc))
    for lg in range(n_groups):
        h_scr[:, lg * LANES:(lg + 1) * LANES] = hs[lg]

    def finish(g, carry):
        src = pl.ds(pl.multiple_of(g * pitch, 8), tcp)
        hseq = jnp.concatenate([b_scr[lg, src, :] for lg in range(n_groups)], axis=1)
        y = _rms(_gelu_tanh(gb_ref[g]) * hseq[0:tc], g_ref[...])
        y_ref[g] = y
        return carry

    lax.fori_loop(0, n_seq, finish, 0)
    convnew_ref[...] = hist_scr[:, hist, :]
    ht_ref[...] = h_scr[...]


def _rglru(z3, conv_buf, h0, sw, layer):
    b, t_len, _ = z3.shape
    tc = min(RG_CHUNK, t_len)
    tcp = -(-tc // 8) * 8
    pitch = _rg_pitch(tcp)
    n_hist = CONV_W - 1
    ng = b // RG_GROUP
    vec = _layer_spec((1, D_B), layer)
    mat = _layer_spec((D_B, D_B), layer)
    seq = pl.BlockSpec((RG_GROUP, tc, D_B), lambda i, c: (i, c, 0))
    hist = pl.BlockSpec((RG_GROUP, n_hist, D_B), lambda i, c: (i, 0, 0))
    state = pl.BlockSpec((None, RG_GROUP, D_B), lambda i, c: (i, 0, 0))
    y, conv_new, h_last = pl.pallas_call(
        _rglru_kernel,
        grid=(ng, t_len // tc),
        in_specs=[pl.BlockSpec((RG_GROUP, tc, D_B), lambda i, c: (i, c, ZC_XB // 3)),
                  pl.BlockSpec((RG_GROUP, tc, D_B), lambda i, c: (i, c, ZC_GB // 3)),
                  hist, state,
                  _layer_spec((CONV_W, D_B), layer), vec, mat, vec, mat, vec, vec, vec],
        out_specs=[seq, hist, state],
        out_shape=[jax.ShapeDtypeStruct((b, t_len, D_B), F32),
                   jax.ShapeDtypeStruct((b, n_hist, D_B), F32),
                   jax.ShapeDtypeStruct((ng, RG_GROUP, D_B), F32)],
        scratch_shapes=[pltpu.VMEM((RG_PAD + tcp, D_B), F32),
                        pltpu.VMEM((RG_GROUP, RG_PAD, D_B), F32),
                        pltpu.VMEM((RG_GROUP, D_B), F32),
                        pltpu.VMEM((D_B // LANES, RG_GROUP * pitch, LANES), F32),
                        pltpu.VMEM((D_B // LANES, RG_GROUP * pitch, LANES), F32)],
        compiler_params=_params(("parallel", "arbitrary")),
        name="rglru",
    )(z3, z3, conv_buf, h0.reshape(ng, RG_GROUP, D_B), sw['conv_w'], sw['conv_b'], sw['w_rg_a'],
      sw['b_rg_a'], sw['w_rg_x'], sw['b_rg_x'], sw['lru_lambda'], sw['g_out_b'])
    return y, conv_new, h_last.reshape(b, D_B)


ML_GROUP = 8


def _split3(x, axis):
    hi = x.astype(BF16).astype(F32)
    r1 = x - hi
    mid = r1.astype(BF16).astype(F32)
    lo = (r1 - mid).astype(BF16).astype(F32)
    return jnp.concatenate([hi, mid, lo], axis=axis).astype(BF16)


def _sum3(x, axis):
    n = x.shape[axis] // 3
    if axis == 0:
        return x[0:n] + x[n:2 * n] + x[2 * n:3 * n]
    return x[:, 0:n] + x[:, n:2 * n] + x[:, 2 * n:3 * n]


def _mlstm_consts():
    L = MLSTM_CHUNK
    t = jnp.arange(L)
    tri_t = (t[:, None] <= t[None, :]).astype(BF16)
    src = jnp.arange(3 * LANES) % LANES
    dst = jnp.arange(N_GATES * LANES) // LANES
    sel = (src[:, None] == dst[None, :]).astype(BF16)
    return tri_t, sel


def _mlstm_t_kernel(n_valid, q0_ref, q1_ref, k0_ref, k1_ref, v0_ref, v1_ref, o0_ref, o1_ref,
                    gr_ref, bias_ref, gain_ref, trit_ref, sel_ref,
                    c0_ref, n0_ref, m0_ref, y_ref, c_out_ref, n_out_ref, m_out_ref,
                    ct_scr, n_scr, m_scr):
    ck = pl.program_id(1)
    n_seq, _, L = gr_ref.shape

    lane = lax.broadcasted_iota(jnp.int32, (L, LANES), 1)
    row = lax.broadcasted_iota(jnp.int32, (L, LANES), 0)
    lo_head = lane < HEAD_DIM
    src_le_dst = row <= lane
    row_lo = row < HEAD_DIM
    blockdiag = ((row ^ lane) & HEAD_DIM) == 0
    sub = lax.broadcasted_iota(jnp.int32, (SUBLANES, L), 0)
    tl = lax.broadcasted_iota(jnp.int32, (SUBLANES, L), 1)
    lo_row = tl[0:1, :] < HEAD_DIM
    is_head_row = sub < N_HEADS_C
    refs = ((q0_ref, k0_ref, v0_ref, o0_ref), (q1_ref, k1_ref, v1_ref, o1_ref))
    units = [(sq, p) for sq in range(n_seq) for p in range(2)]
    pad_rows = jnp.zeros((L - SUBLANES, L), F32)

    @pl.when(ck == 0)
    def _():
        zero = jnp.zeros((HEAD_DIM, HEAD_DIM), F32)
        n_scr[...] = jnp.zeros(n_scr.shape, F32)
        for sq, p in units:
            top = jnp.concatenate([c0_ref[sq, 2 * p], zero], axis=1)
            bot = jnp.concatenate([zero, c0_ref[sq, 2 * p + 1]], axis=1)
            ct_scr[sq, p] = jnp.transpose(jnp.concatenate([top, bot], axis=0))
            n_scr[sq, pl.ds(p, 1), :] = jnp.concatenate(
                [n0_ref[sq, pl.ds(2 * p, 1), :], n0_ref[sq, pl.ds(2 * p + 1, 1), :]], axis=1)
        m_scr[...] = m0_ref[...]

    def cumsum_stage(sq):
        xr = gr_ref[sq] + bias_ref[...]
        gt = jnp.where(is_head_row, xr, _log_sigmoid(xr))
        if n_valid < L:
            gt = jnp.where(tl < n_valid, gt, jnp.where(is_head_row, NEG, 0.0))
        return gt, _sum3(_dot(_split3(gt, 0), trit_ref[...]), 0)

    def gate_stage(sq, gt, csum):
        b = pltpu.roll(csum, N_HEADS_C, axis=0)
        e = gt - b
        cm = e
        shift = 1
        while shift < L:
            cm = jnp.maximum(cm, jnp.where(tl >= shift, pltpu.roll(cm, shift, axis=1), NEG))
            shift *= 2
        m_prev = m_scr[sq]
        g = jnp.maximum(m_prev, cm)
        mt = g + b
        wi = jnp.exp(m_prev - g)
        emt = jnp.exp(-mt)
        m_new = jnp.broadcast_to(mt[:, L - 1:L], (SUBLANES, L))
        b_last = jnp.broadcast_to(b[:, L - 1:L], (SUBLANES, L))
        ws = jnp.exp(e + b_last - m_new)
        wc = jnp.exp(b_last + m_prev - m_new)
        both = jnp.where(is_head_row, e, pltpu.roll(ws, N_HEADS_C, axis=0))
        return g, wi, emt, wc, m_new, _split3(jnp.transpose(jnp.concatenate([both, pad_rows], axis=0)), 1)

    def score_stage(sq, p):
        q_ref, k_ref, v_ref, _ = refs[p]
        q2 = q_ref[sq]
        k2 = k_ref[sq] * (HEAD_DIM ** -0.5)
        q2b = q2.astype(BF16)
        k2b = k2.astype(BF16)
        vt_b = jnp.transpose(v_ref[sq]).astype(BF16)
        ct = ct_scr[sq, p]
        n_pair = n_scr[sq, pl.ds(p, 1), :]
        n2 = jnp.where(jnp.logical_or(jnp.logical_and(sub == 0, tl < HEAD_DIM),
                                      jnp.logical_and(sub == 1, tl >= HEAD_DIM)), n_pair, 0.0)
        qct = _dot_nt(ct.astype(BF16), q2b)
        qn = _dot_nt(n2.astype(BF16), q2b)
        st = [_dot_nt(k2b, jnp.where(lo_head if h == 0 else jnp.logical_not(lo_head), q2, 0.0)
                      .astype(BF16)) for h in range(2)]
        return k2, vt_b, ct, n_pair, qct, qn, st

    def weight_stage(p, gates, staged):
        g, wi, emt, _, _, cols = gates
        qn, st = staged[5], staged[6]
        out = []
        for h in range(2):
            gh = 2 * p + h
            e_c = cols[:, gh * LANES:(gh + 1) * LANES]
            at = jnp.exp(jnp.where(src_le_dst, e_c - g[gh:gh + 1, :], NEG)) * st[h]
            den = jnp.sum(at, axis=0, keepdims=True) + wi[gh:gh + 1, :] * qn[h:h + 1, :]
            inv = 1.0 / jnp.maximum(jnp.abs(den), emt[gh:gh + 1, :])
            out.append((at.astype(BF16), inv))
        return out

    def value_stage(sq, p, gates, staged, weights):
        o_ref = refs[p][3]
        _, wi, _, wc, m_new, cols = gates
        k2, vt_b, ct, n_pair, qct, _, _ = staged
        (at0, inv0), (at1, inv1) = weights
        g0, g1 = 2 * p, 2 * p + 1
        num_t = jnp.where(row_lo, _dot(vt_b, at0), _dot(vt_b, at1))
        wi2 = jnp.where(row_lo, wi[g0:g0 + 1, :], wi[g1:g1 + 1, :])
        inv2 = jnp.where(row_lo, inv0, inv1)
        hout = jnp.transpose((num_t + wi2 * qct) * inv2)
        ws2 = jnp.where(lo_head, cols[:, (N_HEADS_C + g0) * LANES:(N_HEADS_C + g0 + 1) * LANES],
                        cols[:, (N_HEADS_C + g1) * LANES:(N_HEADS_C + g1 + 1) * LANES])
        kw = k2 * ws2
        keep_t = jnp.where(row_lo, wc[g0:g0 + 1, :], wc[g1:g1 + 1, :])
        ct_scr[sq, p] = jnp.where(blockdiag, keep_t * ct + _dot(vt_b, kw.astype(BF16)), 0.0)
        n_scr[sq, pl.ds(p, 1), :] = jnp.where(lo_row, wc[g0:g0 + 1, :], wc[g1:g1 + 1, :]) * n_pair \
            + jnp.sum(kw, axis=0, keepdims=True)
        hc = jax.nn.sigmoid(o_ref[sq]) * hout
        hsq = hc * hc
        ms0 = jnp.sum(jnp.where(lo_head, hsq, 0.0), axis=1, keepdims=True)
        ms1 = jnp.sum(jnp.where(lo_head, 0.0, hsq), axis=1, keepdims=True)
        ms = jnp.where(lo_head, ms0, ms1) * (1.0 / HEAD_DIM)
        y_ref[sq, :, p * LANES:(p + 1) * LANES] = \
            hc * lax.rsqrt(ms + EPS) * gain_ref[:, p * LANES:(p + 1) * LANES]

    sums = [cumsum_stage(sq) for sq in range(n_seq)]
    staged = [score_stage(sq, p) for sq, p in units]
    gates = [gate_stage(sq, *sums[sq]) for sq in range(n_seq)]
    gates = [gt[:5] + (_dot(gt[5], sel_ref[...]),) for gt in gates]
    weights = [weight_stage(p, gates[sq], st) for (sq, p), st in zip(units, staged)]
    for (sq, p), st, wt in zip(units, staged, weights):
        value_stage(sq, p, gates[sq], st, wt)
    for sq in range(n_seq):
        m_scr[sq] = jnp.where(is_head_row, gates[sq][4], 0.0)

    @pl.when(ck == pl.num_programs(1) - 1)
    def _():
        for sq, p in units:
            c_pair = jnp.transpose(ct_scr[sq, p])
            c_out_ref[sq, 2 * p] = c_pair[0:HEAD_DIM, 0:HEAD_DIM]
            c_out_ref[sq, 2 * p + 1] = c_pair[HEAD_DIM:, HEAD_DIM:]
        n_out_ref[...] = n_scr[:, 0:2, :]
        m_out_ref[...] = m_scr[...]


def _mlstm_t(z3, col0, g_rows, n_valid, sw, layer, C0, n0, m0):
    b, t_len, _ = z3.shape
    L = MLSTM_CHUNK
    nc = t_len // L
    G = ML_GROUP
    tri_t, sel = _mlstm_consts()
    m_rows = jnp.broadcast_to(jnp.pad(m0, ((0, 0), (0, SUBLANES - N_HEADS_C)))[:, :, None],
                              (b, SUBLANES, LANES))

    def col(cb):
        return pl.BlockSpec((G, L, LANES), lambda i, c: (i, c, cb - col0))

    state_c = pl.BlockSpec((G, N_HEADS_C, HEAD_DIM, HEAD_DIM), lambda i, c: (i, 0, 0, 0))
    state_m = pl.BlockSpec((G, SUBLANES, LANES), lambda i, c: (i, 0, 0))
    y, c1, n1, m1 = pl.pallas_call(
        functools.partial(_mlstm_t_kernel, n_valid),
        grid=(b // G, nc),
        in_specs=[col(ZC_QC), col(ZC_QC + 1), col(ZC_KC), col(ZC_KC + 1), col(ZC_VC), col(ZC_VC + 1),
                  col(ZC_OC), col(ZC_OC + 1),
                  pl.BlockSpec((G, SUBLANES, L), lambda i, c: (i, 0, c)),
                  _layer_spec((SUBLANES, 1), layer), _layer_spec((1, D_C), layer),
                  _const_spec((L, L)), _const_spec((3 * LANES, N_GATES * LANES)),
                  state_c, pl.BlockSpec((G, N_HEADS_C, HEAD_DIM), lambda i, c: (i, 0, 0)), state_m],
        out_specs=[pl.BlockSpec((G, L, D_C), lambda i, c: (i, c, 0)), state_c,
                   pl.BlockSpec((G, 2, LANES), lambda i, c: (i, 0, 0)), state_m],
        out_shape=[jax.ShapeDtypeStruct((b, t_len, D_C), F32),
                   jax.ShapeDtypeStruct((b, N_HEADS_C, HEAD_DIM, HEAD_DIM), F32),
                   jax.ShapeDtypeStruct((b, 2, LANES), F32),
                   jax.ShapeDtypeStruct((b, SUBLANES, LANES), F32)],
        scratch_shapes=[pltpu.VMEM((G, 2, LANES, LANES), F32), pltpu.VMEM((G, SUBLANES, LANES), F32),
                        pltpu.VMEM((G, SUBLANES, LANES), F32)],
        compiler_params=_params(("parallel", "arbitrary")),
        name="mlstm",
    )(*([z3] * 8), g_rows, sw['b_mlstm_gate'], sw['g_out_c'], tri_t, sel, C0, n0, m_rows)
    return y, c1, n1.reshape(b, N_HEADS_C, HEAD_DIM), m1[:, 0:N_HEADS_C, 0]


def _block_diag(w):
    depth, nb, bs, _ = w.shape
    eye = jnp.eye(nb, dtype=w.dtype)
    return jnp.einsum('lncd,nm->lncmd', w, eye).reshape(depth, nb * bs, nb * bs)


def _stack_weights(p):
    vec = lambda v: v.reshape(v.shape[0], 1, -1)
    bf = lambda w: w.astype(BF16)
    out = {k: vec(p[k]) for k in (
        'g_f1_pre', 'g_f1_post', 'g_mix_pre', 'conv_b', 'b_rg_a', 'b_rg_x', 'lru_lambda', 'g_out_a',
        'g_out_b', 'g_out_c', 'g_mix_post', 'g_f2_pre', 'g_f2_post', 'g_ple_pre', 'g_ple_post')}
    out.update({k: bf(p[k]) for k in (
        'w_f1_gate', 'w_f1_up', 'w_f1_down', 'w_f2_gate', 'w_f2_up', 'w_f2_down', 'w_out',
        'w_ple_gate', 'w_ple_proj')})
    out['w_in'] = bf(jnp.pad(p['w_in'], ((0, 0), (0, 0), (0, Z_COLS - p['w_in'].shape[-1]))))
    out['conv_w'] = p['conv_w']
    out['w_rg_a'] = bf(_block_diag(p['w_rg_a']))
    out['w_rg_x'] = bf(_block_diag(p['w_rg_x']))
    out['b_mlstm_gate'] = jnp.concatenate([p['b_mlstm_i'], p['b_mlstm_f']], axis=1)[:, :, None]
    return out


def _gate_rows(z3, col0):
    lo = (ZC_GATE - col0) * LANES
    return jnp.swapaxes(z3[:, :, lo:lo + N_GATES], 1, 2)


def _layer(x, pe_all, sw, layer, cache, kv_bufs=None):
    b, t_len, d = x.shape
    if cache is None:
        rows, z, new_k, new_v, z_cls, g_rows = _premix(x.reshape(b * t_len, d), sw, layer,
                                                       kv_bufs, t_len)
        z3 = z.reshape(b, t_len, Z_COLS)
        ya = _attn_prompt(z3, z_cls)
        conv_buf = jnp.zeros((b, CONV_W - 1, D_B), F32)
        h0 = jnp.zeros((b, D_B), F32)
        C0 = jnp.zeros((b, N_HEADS_C, HEAD_DIM, HEAD_DIM), F32)
        n0 = jnp.zeros((b, N_HEADS_C, HEAD_DIM), F32)
        m0 = jnp.zeros((b, N_HEADS_C), F32)
        zc, col0, n_valid = z3, 0, MLSTM_CHUNK
    else:
        kt_all, vt_all, conv_buf, h0, C0, n0, m0 = cache
        rows, z = _premix(x.reshape(b * t_len, d), sw, layer)
        z3 = z.reshape(b, t_len, Z_COLS)
        new_k = z3[:, :, ZC_KA * LANES:ZC_KA * LANES + D_A].reshape(b, t_len, N_HEADS_A, HEAD_DIM)
        new_v = z3[:, :, ZC_VA * LANES:ZC_VA * LANES + D_A].reshape(b, t_len, N_HEADS_A, HEAD_DIM)
        ya = _attn_sample(z3, kt_all, vt_all, layer)
        col0 = ZC_QC
        zc = jnp.pad(z3[:, :, col0 * LANES:], ((0, 0), (0, MLSTM_CHUNK - t_len), (0, 0)))
        g_rows = _gate_rows(zc, col0)
        n_valid = t_len
    yb, new_buf, h_last = _rglru(z3, conv_buf, h0, sw, layer)
    yc, C1, n1, m1 = _mlstm_t(zc, col0, g_rows, n_valid, sw, layer, C0, n0, m0)
    yc = yc[:, :t_len]
    rows = _postmix(rows, ya.reshape(b * t_len, D_A), yb.reshape(b * t_len, D_B),
                    yc.reshape(b * t_len, D_C), pe_all, sw, layer)
    return rows.reshape(b, t_len, d), (new_k, new_v, new_buf, h_last, C1, n1, m1)


def _feature_major(cache):
    depth, b, w_buf, nh, dh = cache.shape
    return jnp.transpose(cache, (0, 1, 3, 4, 2)).reshape(depth, b, nh * dh, w_buf)


def kernel(x_prompt, x_sample, cache_k, cache_v, state_conv, state_h, state_C, state_n, state_m, p_prompt, p_sample, g_f1_pre, w_f1_gate, w_f1_up, w_f1_down, g_f1_post, g_mix_pre, w_in, conv_w, conv_b, w_rg_a, b_rg_a, w_rg_x, b_rg_x, lru_lambda, b_mlstm_i, b_mlstm_f, g_out_a, g_out_b, g_out_c, w_out, g_mix_post, g_f2_pre, w_f2_gate, w_f2_up, w_f2_down, g_f2_post, g_ple_pre, w_ple_gate, w_ple_proj, g_ple_post):
    depth = w_in.shape[0]
    sw = _stack_weights(dict(
        g_f1_pre=g_f1_pre, w_f1_gate=w_f1_gate, w_f1_up=w_f1_up, w_f1_down=w_f1_down,
        g_f1_post=g_f1_post, g_mix_pre=g_mix_pre, w_in=w_in, conv_w=conv_w, conv_b=conv_b,
        w_rg_a=w_rg_a, b_rg_a=b_rg_a, w_rg_x=w_rg_x, b_rg_x=b_rg_x, lru_lambda=lru_lambda,
        b_mlstm_i=b_mlstm_i, b_mlstm_f=b_mlstm_f, g_out_a=g_out_a, g_out_b=g_out_b,
        g_out_c=g_out_c, w_out=w_out, g_mix_post=g_mix_post, g_f2_pre=g_f2_pre,
        w_f2_gate=w_f2_gate, w_f2_up=w_f2_up, w_f2_down=w_f2_down, g_f2_post=g_f2_post,
        g_ple_pre=g_ple_pre, w_ple_gate=w_ple_gate, w_ple_proj=w_ple_proj, g_ple_post=g_ple_post))
    xp, xs = x_prompt, x_sample
    pe_prompt = p_prompt.reshape(depth, -1, p_prompt.shape[-1])
    pe_sample = p_sample.reshape(depth, -1, p_sample.shape[-1])
    kt_all = _feature_major(cache_k)
    vt_all = _feature_major(cache_v)
    b_p, s_p, _ = x_prompt.shape
    keep = min(DILATED_CONFIGS[-1][0], s_p)
    kv_bufs = (jnp.zeros((depth, b_p, D_A, keep), F32), jnp.zeros((depth, b_p, D_A, keep), F32))
    sp = [[] for _ in range(5)]
    ss = [[] for _ in range(7)]
    for i in range(depth):
        xp, st_p = _layer(xp, pe_prompt, sw, i, None, kv_bufs)
        kv_bufs = st_p[:2]
        cache_i = (kt_all, vt_all, state_conv[i], state_h[i], state_C[i], state_n[i], state_m[i])
        xs, st_s = _layer(xs, pe_sample, sw, i, cache_i)
        for j in range(5):
            sp[j].append(st_p[2 + j])
        for j in range(7):
            ss[j].append(st_s[j])
    k_prompt, v_prompt = [
        jnp.transpose(buf.reshape(depth, b_p, N_HEADS_A, HEAD_DIM, keep), (0, 1, 4, 2, 3))
        for buf in kv_bufs]
    conv_prompt, h_prompt, C_prompt, n_prompt, m_prompt = [jnp.stack(a) for a in sp]
    k_sample, v_sample, conv_sample, h_sample, C_sample, n_sample, m_sample = [jnp.stack(a) for a in ss]
    return (xp, xs, k_prompt, v_prompt, k_sample, v_sample, conv_prompt, conv_sample,
            h_prompt, h_sample, C_prompt, C_sample, n_prompt, n_sample, m_prompt, m_sample)
```

```python
import functools

import jax
import jax.numpy as jnp
from jax import lax
from jax.experimental import pallas as pl
from jax.experimental.pallas import tpu as pltpu

F32 = jnp.float32
BF16 = jnp.bfloat16

EPS = 1e-6
NEG = -1e30
HEAD_DIM = 64
N_HEADS_A = 6
D_A = N_HEADS_A * HEAD_DIM
D_B = 384
N_HEADS_C = 4
D_C = N_HEADS_C * HEAD_DIM
N_GATES = 2 * N_HEADS_C
DILATED_CONFIGS = ((128, 1), (512, 4), (2048, 16))
LOCAL_BLK = 128
CONV_W = 4
LRU_C = 8.0
MLSTM_CHUNK = 128
LANES = 128
SUBLANES = 8
Z_COLS = 3072
FF_CHUNK = 256
ROW_TILE = 512
VMEM_LIMIT = 58 * 1024 * 1024

ZC_QA, ZC_KA, ZC_VA, ZC_XB, ZC_GB = 0, 3, 6, 9, 12
ZC_QC, ZC_KC, ZC_VC, ZC_OC, ZC_GATE = 15, 17, 19, 21, 23


def _rms(x, g):
    return x * lax.rsqrt(jnp.mean(x * x, axis=-1, keepdims=True) + EPS) * g


def _dot(a, b):
    return jnp.dot(a, b, preferred_element_type=F32)


def _dot_nt(a, b):
    return lax.dot_general(a, b, (((1,), (1,)), ((), ())), preferred_element_type=F32)


def _softplus(x):
    return jnp.maximum(x, 0.0) + jnp.log1p(jnp.exp(-jnp.abs(x)))


def _log_sigmoid(x):
    return -_softplus(-x)


def _sigmoid_tanh(x):
    return 0.5 * jnp.tanh(0.5 * x) + 0.5


def _gelu_tanh(x):
    return 0.5 * x * (1.0 + jnp.tanh(0.7978845608028654 * (x + 0.044715 * (x * x * x))))


def _const_spec(shape):
    nd = len(shape)
    return pl.BlockSpec(shape, lambda *_: (0,) * nd, pipeline_mode=pl.Buffered(1))


def _layer_spec(shape, layer, block=None):
    nd = len(shape)
    idx = (0,) * nd if block is None else block
    return pl.BlockSpec((None,) + tuple(shape), lambda *_: (layer,) + idx,
                        pipeline_mode=pl.Buffered(1))


def _params(sem):
    return pltpu.CompilerParams(dimension_semantics=sem, vmem_limit_bytes=VMEM_LIMIT)


def _swiglu_into(acc_ref, h, wg_ref, wu_ref, wd_ref):
    for c in range(wg_ref.shape[1] // FF_CHUNK):
        sl = slice(c * FF_CHUNK, (c + 1) * FF_CHUNK)
        g = _dot(h, wg_ref[:, sl])
        u = _dot(h, wu_ref[:, sl])
        a = (g * jax.nn.sigmoid(g) * u).astype(BF16)
        d = _dot(a, wd_ref[sl, :])
        if c == 0:
            acc_ref[...] = d
        else:
            acc_ref[...] += d


def _premix_kernel(emit_kv, x_ref, gpre_ref, wg_ref, wu_ref, wd_ref, gpost_ref, gmix_ref, win_ref,
                   *rest):
    if emit_kv:
        _, _, xo_ref, z_ref, kt_ref, vt_ref, zc_ref, gt_ref, acc_ref, stage_ref = rest
    else:
        xo_ref, z_ref, acc_ref = rest
    x = x_ref[...]
    _swiglu_into(acc_ref, _rms(x, gpre_ref[...]).astype(BF16), wg_ref, wu_ref, wd_ref)
    x = x + 0.5 * _rms(acc_ref[...], gpost_ref[...])
    xo_ref[...] = x
    h = _rms(x, gmix_ref[...]).astype(BF16)
    for c in range(win_ref.shape[1] // 256):
        sl = slice(c * 256, (c + 1) * 256)
        zc = _dot(h, win_ref[:, sl])
        z_ref[:, sl] = zc
        if not emit_kv:
            continue
        n_cls, rows_per_cls, width = zc_ref.shape
        for half in range(2):
            lg = 2 * c + half
            if lg == ZC_GATE:
                gt_ref[...] = zc[:, half * LANES:(half + 1) * LANES].T[0:N_GATES, :]
            if lg >= width // LANES:
                continue
            cols = zc[:, half * LANES:(half + 1) * LANES]
            lanes = slice(lg * LANES, (lg + 1) * LANES)
            if ZC_KA <= lg < ZC_KA + D_A // LANES:
                kt_ref[(lg - ZC_KA) * LANES:(lg - ZC_KA + 1) * LANES, :] = cols.T
            if ZC_VA <= lg < ZC_VA + D_A // LANES:
                vt_ref[(lg - ZC_VA) * LANES:(lg - ZC_VA + 1) * LANES, :] = cols.T
            stage_ref[lg % 2] = cols
            for r in range(n_cls):
                zc_ref[r, :, lanes] = stage_ref[lg % 2, pl.ds(r, rows_per_cls, stride=n_cls), :]


def _premix(x, sw, layer, kv_bufs=None, seq_len=None):
    m, d = x.shape
    ff = sw['w_f1_gate'].shape[-1]
    tm = min(ROW_TILE, m)
    row = pl.BlockSpec((tm, d), lambda i: (i, 0))
    vec = _layer_spec((1, d), layer)
    in_specs = [row, vec, _layer_spec((d, ff), layer), _layer_spec((d, ff), layer),
                _layer_spec((ff, d), layer), vec, vec, _layer_spec((d, Z_COLS), layer)]
    out_specs = [row, pl.BlockSpec((tm, Z_COLS), lambda i: (i, 0))]
    out_shape = [jax.ShapeDtypeStruct((m, d), F32), jax.ShapeDtypeStruct((m, Z_COLS), F32)]
    args = [x, sw['g_f1_pre'], sw['w_f1_gate'], sw['w_f1_up'], sw['w_f1_down'], sw['g_f1_post'],
            sw['g_mix_pre'], sw['w_in']]
    aliases = {}
    scratch = [pltpu.VMEM((tm, d), F32)]
    if kv_bufs is not None:
        keep = kv_bufs[0].shape[-1]
        tiles_per_seq = seq_len // tm
        first_kept = (seq_len - keep) // tm
        kv_spec = pl.BlockSpec(
            (None, None, D_A, tm),
            lambda i: (layer, i // tiles_per_seq, 0, jnp.maximum(i % tiles_per_seq - first_kept, 0)))
        for buf in kv_bufs:
            aliases[len(args)] = len(out_shape)
            in_specs.append(pl.BlockSpec(memory_space=pl.ANY))
            args.append(buf)
            out_specs.append(kv_spec)
            out_shape.append(jax.ShapeDtypeStruct(buf.shape, F32))
        n_cls = DILATED_CONFIGS[-1][1]
        out_specs.append(pl.BlockSpec((None, n_cls, tm // n_cls, 3 * D_A),
                                      lambda i: (i // tiles_per_seq, 0, i % tiles_per_seq, 0)))
        out_shape.append(jax.ShapeDtypeStruct((m // seq_len, n_cls, seq_len // n_cls, 3 * D_A), F32))
        out_specs.append(pl.BlockSpec((None, N_GATES, tm),
                                      lambda i: (i // tiles_per_seq, 0, i % tiles_per_seq)))
        out_shape.append(jax.ShapeDtypeStruct((m // seq_len, N_GATES, seq_len), F32))
        scratch.append(pltpu.VMEM((2, tm, LANES), F32))
    return pl.pallas_call(
        functools.partial(_premix_kernel, kv_bufs is not None),
        grid=(m // tm,),
        in_specs=in_specs,
        out_specs=out_specs,
        out_shape=out_shape,
        scratch_shapes=scratch,
        input_output_aliases=aliases,
        compiler_params=_params(("arbitrary",)),
        name="premix",
    )(*args)


def _postmix_kernel(x_ref, ya_ref, yb_ref, yc_ref, pe_ref, ga_ref, wa_ref, wb_ref, wc_ref, gmix_ref,
                    gpre_ref, wg_ref, wu_ref, wd_ref, gpost_ref,
                    gple_ref, wpg_ref, wpp_ref, gple_post_ref, o_ref, acc_ref, x_scr):
    ya = _rms(ya_ref[...], ga_ref[...]).astype(BF16)
    y = _dot(ya, wa_ref[...])
    y = y + _dot(yb_ref[...].astype(BF16), wb_ref[...])
    y = y + _dot(yc_ref[...].astype(BF16), wc_ref[...])
    x_scr[...] = x_ref[...] + _rms(y, gmix_ref[...])
    _swiglu_into(acc_ref, _rms(x_scr[...], gpre_ref[...]).astype(BF16), wg_ref, wu_ref, wd_ref)
    x_scr[...] = x_scr[...] + 0.5 * _rms(acc_ref[...], gpost_ref[...])
    h = _rms(x_scr[...], gple_ref[...]).astype(BF16)
    gate = jax.nn.sigmoid(_dot(h, wpg_ref[...]))
    proj = _dot(pe_ref[...].astype(BF16), wpp_ref[...])
    o_ref[...] = x_scr[...] + _rms(gate * proj, gple_post_ref[...])


def _postmix(x, ya, yb, yc, pe_all, sw, layer):
    m, d = x.shape
    dp = pe_all.shape[-1]
    ff = sw['w_f2_gate'].shape[-1]
    tm = min(ROW_TILE, m)

    def row(width):
        return pl.BlockSpec((tm, width), lambda i: (i, 0))

    vec = _layer_spec((1, d), layer)
    return pl.pallas_call(
        _postmix_kernel,
        grid=(m // tm,),
        in_specs=[row(d), row(D_A), row(D_B), row(D_C),
                  pl.BlockSpec((None, tm, dp), lambda i: (layer, i, 0)),
                  _layer_spec((1, D_A), layer),
                  _layer_spec((D_A, d), layer, (0, 0)),
                  _layer_spec((D_B, d), layer, (D_A // D_B, 0)),
                  _layer_spec((D_C, d), layer, ((D_A + D_B) // D_C, 0)), vec,
                  vec, _layer_spec((d, ff), layer), _layer_spec((d, ff), layer),
                  _layer_spec((ff, d), layer), vec,
                  vec, _layer_spec((d, d), layer), _layer_spec((dp, d), layer), vec],
        out_specs=row(d),
        out_shape=jax.ShapeDtypeStruct((m, d), F32),
        scratch_shapes=[pltpu.VMEM((tm, d), F32), pltpu.VMEM((tm, d), F32)],
        compiler_params=_params(("parallel",)),
        name="postmix",
    )(x, ya, yb, yc, pe_all, sw['g_out_a'], sw['w_out'], sw['w_out'], sw['w_out'], sw['g_mix_post'],
      sw['g_f2_pre'], sw['w_f2_gate'], sw['w_f2_up'], sw['w_f2_down'], sw['g_f2_post'],
      sw['g_ple_pre'], sw['w_ple_gate'], sw['w_ple_proj'], sw['g_ple_post'])


ATT_QT = 2048
ATT_UNROLL = 16


ATT_PITCH = LOCAL_BLK + 8


def _attn_prompt_kernel(q_ref, k_ref, v_ref, qc_ref, kc_ref, vc_ref, bias_ref, o_ref,
                        num_scr, m_scr, den_scr, cls_scr):
    t0 = pl.program_id(2) * ATT_QT
    blk = LOCAL_BLK
    lo_head = lax.broadcasted_iota(jnp.int32, (blk, LANES), 1) < HEAD_DIM
    scale = HEAD_DIM ** -0.5 * 1.4426950408889634
    n_cfg = len(DILATED_CONFIGS)
    dil_c = DILATED_CONFIGS[n_cfg - 1][1]

    for ci, (_, dil) in enumerate(DILATED_CONFIGS):
        span = blk * dil
        class_major = span == ATT_QT

        def rows_of(j, dil=dil, span=span, class_major=class_major):
            if class_major:
                tile = pl.program_id(2)
                first = tile == 0
                cur = pl.multiple_of(tile * blk, 8)
                prev = jnp.where(first, cur, cur - blk)
                return ((j,), (j, pl.ds(cur, blk)), (j, pl.ds(pl.multiple_of(prev, 8), blk)), first)
            qs = j % dil + span * (j // dil)
            ks = t0 + qs
            first = ks < span
            ps = jnp.where(first, ks, ks - span)
            return ((pl.ds(qs, blk, stride=dil),), (pl.ds(ks, blk, stride=dil),),
                    (pl.ds(ps, blk, stride=dil),), first)

        q_src, k_src, v_src = (qc_ref, kc_ref, vc_ref) if class_major else (q_ref, k_ref, v_ref)

        def scores(j, rows_of=rows_of, q_src=q_src, k_src=k_src):
            q_rows, cur, prev, first = rows_of(j)
            q2 = q_src[q_rows] * scale
            kc = k_src[cur].astype(BF16)
            kp = k_src[prev].astype(BF16)
            q_both = jnp.concatenate([jnp.where(lo_head, q2, 0.0), jnp.where(lo_head, 0.0, q2)], axis=0)
            raw = _dot_nt(q_both.astype(BF16), jnp.concatenate([kp, kc], axis=0))
            return j, q_rows, cur, prev, first.astype(jnp.int32), raw

        def softmax(first, raw):
            s = raw + bias_ref[first]
            m = jnp.max(s, axis=-1, keepdims=True)
            p = jnp.exp2(s - m)
            return p.astype(BF16), m, jnp.sum(p, axis=-1, keepdims=True)

        def weighted_values(j, q_rows, cur, prev, probs, ci=ci, class_major=class_major, v_src=v_src):
            vc = v_src[cur].astype(BF16)
            vp = v_src[prev].astype(BF16)
            p_both, m, den = probs
            num = _dot(p_both, jnp.concatenate([vp, vc], axis=0))
            num = jnp.where(lo_head, num[:blk], num[blk:])
            m = jnp.where(lo_head, m[:blk], m[blk:])
            den = jnp.where(lo_head, den[:blk], den[blk:])
            if class_major:
                dst = pl.ds(pl.multiple_of(j * ATT_PITCH, 8), blk)
                cls_scr[0, dst, :] = num
                cls_scr[1, dst, :] = m
                cls_scr[2, dst, :] = den
            else:
                num_scr[(ci,) + q_rows] = num
                m_scr[(ci,) + q_rows] = m
                den_scr[(ci,) + q_rows] = den

        def body(jj, carry, scores=scores, softmax=softmax, weighted_values=weighted_values):
            staged = [scores(jj * ATT_UNROLL + u) for u in range(ATT_UNROLL)]
            probs = [softmax(st[4], st[5]) for st in staged]
            for st, pr in zip(staged, probs):
                weighted_values(st[0], st[1], st[2], st[3], pr)
            return carry

        lax.fori_loop(0, ATT_QT // blk // ATT_UNROLL, body, 0)

    rows = 256

    def combine(i, carry):
        sl = pl.ds(pl.multiple_of(i * rows, rows), rows)

        def class_rows(a):
            base = i * (rows // dil_c)
            return jnp.concatenate(
                [cls_scr[a, pl.ds(base + g, dil_c, stride=ATT_PITCH), :] for g in range(rows // dil_c)],
                axis=0)

        ms = [m_scr[0, sl, :], m_scr[1, sl, :], class_rows(1)]
        nums = [num_scr[0, sl, :], num_scr[1, sl, :], class_rows(0)]
        dens = [den_scr[0, sl, :], den_scr[1, sl, :], class_rows(2)]
        mx = jnp.maximum(jnp.maximum(ms[0], ms[1]), ms[2])
        ws = [jnp.exp2(mc - mx) for mc in ms]
        num = nums[0] * ws[0] + nums[1] * ws[1] + nums[2] * ws[2]
        den = dens[0] * ws[0] + dens[1] * ws[1] + dens[2] * ws[2]
        o_ref[sl, :] = num / den
        return carry

    lax.fori_loop(0, ATT_QT // rows, combine, 0)


def _attn_bias():
    blk = LOCAL_BLK
    qi = jnp.arange(2 * blk)[:, None] % blk
    kj = jnp.arange(2 * blk)[None, :]
    prev_ok = (kj < blk) & (kj >= qi)
    cur_ok = (kj >= blk) & (kj - blk <= qi)
    normal = jnp.where(prev_ok | cur_ok, 0.0, NEG)
    first = jnp.where(cur_ok, 0.0, NEG)
    return jnp.stack([normal, first]).astype(F32)


def _attn_prompt(z3, zc):
    b, s, _ = z3.shape
    n_pairs = D_A // LANES
    dil_c = DILATED_CONFIGS[-1][1]
    assert dil_c * LOCAL_BLK == ATT_QT and len(DILATED_CONFIGS) == 3
    q_spec = pl.BlockSpec((None, ATT_QT, LANES), lambda i, p, t: (i, t, ZC_QA + p))
    k_spec = pl.BlockSpec((None, s, LANES), lambda i, p, t: (i, 0, ZC_KA + p))
    v_spec = pl.BlockSpec((None, s, LANES), lambda i, p, t: (i, 0, ZC_VA + p))
    qc_spec = pl.BlockSpec((None, dil_c, LOCAL_BLK, LANES), lambda i, p, t: (i, 0, t, ZC_QA + p))
    kc_spec = pl.BlockSpec((None, dil_c, s // dil_c, LANES), lambda i, p, t: (i, 0, 0, ZC_KA + p))
    vc_spec = pl.BlockSpec((None, dil_c, s // dil_c, LANES), lambda i, p, t: (i, 0, 0, ZC_VA + p))
    return pl.pallas_call(
        _attn_prompt_kernel,
        grid=(b, n_pairs, s // ATT_QT),
        in_specs=[q_spec, k_spec, v_spec, qc_spec, kc_spec, vc_spec,
                  _const_spec((2, 2 * LOCAL_BLK, 2 * LOCAL_BLK))],
        out_specs=pl.BlockSpec((None, ATT_QT, LANES), lambda i, p, t: (i, t, p)),
        out_shape=jax.ShapeDtypeStruct((b, s, D_A), F32),
        scratch_shapes=[pltpu.VMEM((2, ATT_QT, LANES), F32)] * 3
        + [pltpu.VMEM((3, dil_c * ATT_PITCH, LANES), F32)],
        compiler_params=_params(("parallel", "parallel", "arbitrary")),
        name="attn_prompt",
    )(z3, z3, z3, zc, zc, zc, _attn_bias())


def _attn_sample_kernel(q_ref, kn_ref, vn_ref, kt_ref, vt_ref, o_ref, q_scr, kn_scr, vn_scr):
    t_new = q_ref.shape[0]
    w_buf = kt_ref.shape[1]
    q_scr[...] = jnp.zeros(q_scr.shape, F32)
    kn_scr[...] = jnp.zeros(kn_scr.shape, F32)
    vn_scr[...] = jnp.zeros(vn_scr.shape, F32)
    q_scr[0:t_new, :] = q_ref[...] * (HEAD_DIM ** -0.5)
    kn_scr[0:t_new, :] = kn_ref[...]
    vn_scr[0:t_new, :] = vn_ref[...]

    rows = 2 * SUBLANES
    tq = lax.broadcasted_iota(jnp.int32, (rows, w_buf), 0) & (SUBLANES - 1)
    dist = w_buf + tq - lax.broadcasted_iota(jnp.int32, (rows, w_buf), 1)
    tqn = lax.broadcasted_iota(jnp.int32, (rows, LANES), 0) & (SUBLANES - 1)
    tn = lax.broadcasted_iota(jnp.int32, (rows, LANES), 1)
    dist_new = tqn - tn
    cache_ok, new_ok = [], []
    for window, dil in DILATED_CONFIGS:
        ok = dist <= window
        nk = jnp.logical_and(dist_new >= 0, tn < t_new)
        nk = jnp.logical_and(nk, dist_new <= window)
        if dil > 1:
            ok = jnp.logical_and(ok, (dist & (dil - 1)) == 0)
            nk = jnp.logical_and(nk, (dist_new & (dil - 1)) == 0)
        cache_ok.append(ok)
        new_ok.append(nk)
    lo8 = lax.broadcasted_iota(jnp.int32, (SUBLANES, LANES), 1) < HEAD_DIM

    for p in range(D_A // LANES):
        sl = slice(p * LANES, (p + 1) * LANES)
        q2 = q_scr[:, sl]
        qm = jnp.concatenate([jnp.where(lo8, q2, 0.0), jnp.where(lo8, 0.0, q2)], axis=0).astype(BF16)
        kb = kt_ref[sl, :].astype(BF16)
        vb = vt_ref[sl, :].astype(BF16)
        s = _dot(qm, kb)
        s_new = _dot_nt(qm, kn_scr[:, sl].astype(BF16))
        parts = []
        for c in range(len(DILATED_CONFIGS)):
            sc = jnp.where(cache_ok[c], s, NEG)
            sn = jnp.where(new_ok[c], s_new, NEG)
            m = jnp.maximum(jnp.max(sc, axis=1, keepdims=True), jnp.max(sn, axis=1, keepdims=True))
            pc = jnp.exp(sc - m)
            pn = jnp.exp(sn - m)
            den = jnp.sum(pc, axis=1, keepdims=True) + jnp.sum(pn, axis=1, keepdims=True)
            parts.append((pc, pn, m, den))
        mx = jnp.maximum(jnp.maximum(parts[0][2], parts[1][2]), parts[2][2])
        ws = [jnp.exp(pt[2] - mx) for pt in parts]
        den_all = parts[0][3] * ws[0] + parts[1][3] * ws[1] + parts[2][3] * ws[2]
        coefs = [w / den_all for w in ws]
        p_all = parts[0][0] * coefs[0] + parts[1][0] * coefs[1] + parts[2][0] * coefs[2]
        pn_all = parts[0][1] * coefs[0] + parts[1][1] * coefs[1] + parts[2][1] * coefs[2]
        o16 = _dot_nt(p_all.astype(BF16), vb) + _dot(pn_all.astype(BF16), vn_scr[:, sl].astype(BF16))
        o8 = jnp.where(lo8, o16[0:SUBLANES], o16[SUBLANES:rows])
        o_ref[:, sl] = o8[0:t_new]


def _attn_sample(zs3, kt_all, vt_all, layer):
    b, t_new, _ = zs3.shape
    w_buf = kt_all.shape[-1]

    def new_spec(col):
        return pl.BlockSpec((None, t_new, D_A), lambda i: (i, 0, col))

    cache_spec = pl.BlockSpec((None, None, D_A, w_buf), lambda i: (layer, i, 0, 0))
    return pl.pallas_call(
        _attn_sample_kernel,
        grid=(b,),
        in_specs=[new_spec(0), new_spec(1), new_spec(2), cache_spec, cache_spec],
        out_specs=pl.BlockSpec((None, t_new, D_A), lambda i: (i, 0, 0)),
        out_shape=jax.ShapeDtypeStruct((b, t_new, D_A), F32),
        scratch_shapes=[pltpu.VMEM((SUBLANES, D_A), F32), pltpu.VMEM((LANES, D_A), F32),
                        pltpu.VMEM((LANES, D_A), F32)],
        compiler_params=_params(("parallel",)),
        name="attn_sample",
    )(zs3, zs3, zs3, kt_all, vt_all)


RG_GROUP = SUBLANES
RG_CHUNK = 256
RG_PAD = 8


def _rg_pitch(tcp):
    tiles = tcp // 8 + 1
    return 8 * (tiles if tiles % 2 else tiles + 1)


def _rglru_kernel(xb_ref, gb_ref, conv_ref, h0_ref, cw_ref, cb_ref, wa_ref, ba_ref, wx_ref, bx_ref,
                  lam_ref, g_ref, y_ref, convnew_ref, ht_ref, xp_scr, hist_scr, h_scr, a_scr, b_scr):
    n_seq, tc, _ = xb_ref.shape
    tcp = xp_scr.shape[0] - RG_PAD
    pitch = a_scr.shape[1] // n_seq
    n_hist = CONV_W - 1
    hist = slice(RG_PAD - n_hist, RG_PAD)
    n_groups = D_B // LANES

    @pl.when(pl.program_id(1) == 0)
    def _():
        hist_scr[:, hist, :] = conv_ref[...]
        h_scr[...] = h0_ref[...]

    if tc % 8:
        xp_scr[RG_PAD:, :] = jnp.zeros((tcp, D_B), F32)
    decay = _softplus(-lam_ref[...])

    def gates(g, carry):
        xp_scr[hist, :] = hist_scr[g, hist, :]
        xp_scr[RG_PAD:RG_PAD + tc, :] = xb_ref[g]
        hist_scr[g, hist, :] = xp_scr[RG_PAD + tc - n_hist:RG_PAD + tc, :]
        xc = cb_ref[...]
        for j in range(CONV_W):
            lo = RG_PAD - n_hist + j
            xc = xc + xp_scr[lo:lo + tcp, :] * cw_ref[pl.ds(j, 1), :]
        xcb = xc.astype(BF16)
        r = _sigmoid_tanh(_dot(xcb, wa_ref[...]) + ba_ref[...])
        gi = _sigmoid_tanh(_dot(xcb, wx_ref[...]) + bx_ref[...])
        a = jnp.exp(-LRU_C * r * decay)
        om = 1.0 - a * a
        bb = jnp.where(om > 0.0, om * lax.rsqrt(om), 0.0) * (gi * xc)
        dst = pl.ds(pl.multiple_of(g * pitch, 8), tcp)
        for lg in range(n_groups):
            a_scr[lg, dst, :] = a[:, lg * LANES:(lg + 1) * LANES]
            b_scr[lg, dst, :] = bb[:, lg * LANES:(lg + 1) * LANES]
        return carry

    lax.fori_loop(0, n_seq, gates, 0)

    def step(t, hs):
        rows = pl.ds(t, n_seq, stride=pitch)
        out = []
        for lg in range(n_groups):
            h = a_scr[lg, rows, :] * hs[lg] + b_scr[lg, rows, :]
            b_scr[lg, rows, :] = h
            out.append(h)
        return tuple(out)

    hs = tuple(h_scr[:, lg * LANES:(lg + 1) * LANES] for lg in range(n_groups))
    hs = lax.fori_loop(0, tc, step, hs, unroll=min(8, tc))
    for lg in range(n_groups):
        h_scr[:, lg * LANES:(lg + 1) * LANES] = hs[lg]

    def finish(g, carry):
        src = pl.ds(pl.multiple_of(g * pitch, 8), tcp)
        hseq = jnp.concatenate([b_scr[lg, src, :] for lg in range(n_groups)], axis=1)
        y = _rms(_gelu_tanh(gb_ref[g]) * hseq[0:tc], g_ref[...])
        y_ref[g] = y
        return carry

    lax.fori_loop(0, n_seq, finish, 0)
    convnew_ref[...] = hist_scr[:, hist, :]
    ht_ref[...] = h_scr[...]


def _rglru(z3, conv_buf, h0, sw, layer):
    b, t_len, _ = z3.shape
    tc = min(RG_CHUNK, t_len)
    tcp = -(-tc // 8) * 8
    pitch = _rg_pitch(tcp)
    n_hist = CONV_W - 1
    ng = b // RG_GROUP
    vec = _layer_spec((1, D_B), layer)
    mat = _layer_spec((D_B, D_B), layer)
    seq = pl.BlockSpec((RG_GROUP, tc, D_B), lambda i, c: (i, c, 0))
    hist = pl.BlockSpec((RG_GROUP, n_hist, D_B), lambda i, c: (i, 0, 0))
    state = pl.BlockSpec((None, RG_GROUP, D_B), lambda i, c: (i, 0, 0))
    y, conv_new, h_last = pl.pallas_call(
        _rglru_kernel,
        grid=(ng, t_len // tc),
        in_specs=[pl.BlockSpec((RG_GROUP, tc, D_B), lambda i, c: (i, c, ZC_XB // 3)),
                  pl.BlockSpec((RG_GROUP, tc, D_B), lambda i, c: (i, c, ZC_GB // 3)),
                  hist, state,
                  _layer_spec((CONV_W, D_B), layer), vec, mat, vec, mat, vec, vec, vec],
        out_specs=[seq, hist, state],
        out_shape=[jax.ShapeDtypeStruct((b, t_len, D_B), F32),
                   jax.ShapeDtypeStruct((b, n_hist, D_B), F32),
                   jax.ShapeDtypeStruct((ng, RG_GROUP, D_B), F32)],
        scratch_shapes=[pltpu.VMEM((RG_PAD + tcp, D_B), F32),
                        pltpu.VMEM((RG_GROUP, RG_PAD, D_B), F32),
                        pltpu.VMEM((RG_GROUP, D_B), F32),
                        pltpu.VMEM((D_B // LANES, RG_GROUP * pitch, LANES), F32),
                        pltpu.VMEM((D_B // LANES, RG_GROUP * pitch, LANES), F32)],
        compiler_params=_params(("parallel", "arbitrary")),
        name="rglru",
    )(z3, z3, conv_buf, h0.reshape(ng, RG_GROUP, D_B), sw['conv_w'], sw['conv_b'], sw['w_rg_a'],
      sw['b_rg_a'], sw['w_rg_x'], sw['b_rg_x'], sw['lru_lambda'], sw['g_out_b'])
    return y, conv_new, h_last.reshape(b, D_B)


ML_GROUP = 8


def _split3(x, axis):
    hi = x.astype(BF16).astype(F32)
    r1 = x - hi
    mid = r1.astype(BF16).astype(F32)
    lo = (r1 - mid).astype(BF16).astype(F32)
    return jnp.concatenate([hi, mid, lo], axis=axis).astype(BF16)


def _sum3(x, axis):
    n = x.shape[axis] // 3
    if axis == 0:
        return x[0:n] + x[n:2 * n] + x[2 * n:3 * n]
    return x[:, 0:n] + x[:, n:2 * n] + x[:, 2 * n:3 * n]


def _mlstm_consts():
    L = MLSTM_CHUNK
    t = jnp.arange(L)
    tri_t = (t[:, None] <= t[None, :]).astype(BF16)
    src = jnp.arange(3 * LANES) % LANES
    dst = jnp.arange(N_GATES * LANES) // LANES
    sel = (src[:, None] == dst[None, :]).astype(BF16)
    return tri_t, sel


def _mlstm_t_kernel(n_valid, q0_ref, q1_ref, k0_ref, k1_ref, v0_ref, v1_ref, o0_ref, o1_ref,
                    gr_ref, bias_ref, gain_ref, trit_ref, sel_ref,
                    c0_ref, n0_ref, m0_ref, y_ref, c_out_ref, n_out_ref, m_out_ref,
                    ct_scr, n_scr, m_scr):
    ck = pl.program_id(1)
    n_seq, _, L = gr_ref.shape

    lane = lax.broadcasted_iota(jnp.int32, (L, LANES), 1)
    row = lax.broadcasted_iota(jnp.int32, (L, LANES), 0)
    lo_head = lane < HEAD_DIM
    src_le_dst = row <= lane
    row_lo = row < HEAD_DIM
    blockdiag = ((row ^ lane) & HEAD_DIM) == 0
    sub = lax.broadcasted_iota(jnp.int32, (SUBLANES, L), 0)
    tl = lax.broadcasted_iota(jnp.int32, (SUBLANES, L), 1)
    lo_row = tl[0:1, :] < HEAD_DIM
    is_head_row = sub < N_HEADS_C
    refs = ((q0_ref, k0_ref, v0_ref, o0_ref), (q1_ref, k1_ref, v1_ref, o1_ref))
    units = [(sq, p) for sq in range(n_seq) for p in range(2)]
    pad_rows = jnp.zeros((L - SUBLANES, L), F32)

    @pl.when(ck == 0)
    def _():
        zero = jnp.zeros((HEAD_DIM, HEAD_DIM), F32)
        n_scr[...] = jnp.zeros(n_scr.shape, F32)
        for sq, p in units:
            top = jnp.concatenate([c0_ref[sq, 2 * p], zero], axis=1)
            bot = jnp.concatenate([zero, c0_ref[sq, 2 * p + 1]], axis=1)
            ct_scr[sq, p] = jnp.transpose(jnp.concatenate([top, bot], axis=0))
            n_scr[sq, pl.ds(p, 1), :] = jnp.concatenate(
                [n0_ref[sq, pl.ds(2 * p, 1), :], n0_ref[sq, pl.ds(2 * p + 1, 1), :]], axis=1)
        m_scr[...] = m0_ref[...]

    def cumsum_stage(sq):
        xr = gr_ref[sq] + bias_ref[...]
        gt = jnp.where(is_head_row, xr, _log_sigmoid(xr))
        if n_valid < L:
            gt = jnp.where(tl < n_valid, gt, jnp.where(is_head_row, NEG, 0.0))
        return gt, _sum3(_dot(_split3(gt, 0), trit_ref[...]), 0)

    def gate_stage(sq, gt, csum):
        b = pltpu.roll(csum, N_HEADS_C, axis=0)
        e = gt - b
        cm = e
        shift = 1
        while shift < L:
            cm = jnp.maximum(cm, jnp.where(tl >= shift, pltpu.roll(cm, shift, axis=1), NEG))
            shift *= 2
        m_prev = m_scr[sq]
        g = jnp.maximum(m_prev, cm)
        mt = g + b
        wi = jnp.exp(m_prev - g)
        emt = jnp.exp(-mt)
        m_new = jnp.broadcast_to(mt[:, L - 1:L], (SUBLANES, L))
        b_last = jnp.broadcast_to(b[:, L - 1:L], (SUBLANES, L))
        ws = jnp.exp(e + b_last - m_new)
        wc = jnp.exp(b_last + m_prev - m_new)
        both = jnp.where(is_head_row, e, pltpu.roll(ws, N_HEADS_C, axis=0))
        return g, wi, emt, wc, m_new, _split3(jnp.transpose(jnp.concatenate([both, pad_rows], axis=0)), 1)

    def score_stage(sq, p):
        q_ref, k_ref, v_ref, _ = refs[p]
        q2 = q_ref[sq]
        k2 = k_ref[sq] * (HEAD_DIM ** -0.5)
        q2b = q2.astype(BF16)
        k2b = k2.astype(BF16)
        vt_b = jnp.transpose(v_ref[sq]).astype(BF16)
        ct = ct_scr[sq, p]
        n_pair = n_scr[sq, pl.ds(p, 1), :]
        n2 = jnp.where(jnp.logical_or(jnp.logical_and(sub == 0, tl < HEAD_DIM),
                                      jnp.logical_and(sub == 1, tl >= HEAD_DIM)), n_pair, 0.0)
        qct = _dot_nt(ct.astype(BF16), q2b)
        qn = _dot_nt(n2.astype(BF16), q2b)
        st = [_dot_nt(k2b, jnp.where(lo_head if h == 0 else jnp.logical_not(lo_head), q2, 0.0)
                      .astype(BF16)) for h in range(2)]
        return k2, vt_b, ct, n_pair, qct, qn, st

    def weight_stage(p, gates, staged):
        g, wi, emt, _, _, cols = gates
        qn, st = staged[5], staged[6]
        out = []
        for h in range(2):
            gh = 2 * p + h
            e_c = cols[:, gh * LANES:(gh + 1) * LANES]
            at = jnp.exp(jnp.where(src_le_dst, e_c - g[gh:gh + 1, :], NEG)) * st[h]
            den = jnp.sum(at, axis=0, keepdims=True) + wi[gh:gh + 1, :] * qn[h:h + 1, :]
            inv = 1.0 / jnp.maximum(jnp.abs(den), emt[gh:gh + 1, :])
            out.append((at.astype(BF16), inv))
        return out

    def value_stage(sq, p, gates, staged, weights):
        o_ref = refs[p][3]
        _, wi, _, wc, m_new, cols = gates
        k2, vt_b, ct, n_pair, qct, _, _ = staged
        (at0, inv0), (at1, inv1) = weights
        g0, g1 = 2 * p, 2 * p + 1
        num_t = jnp.where(row_lo, _dot(vt_b, at0), _dot(vt_b, at1))
        wi2 = jnp.where(row_lo, wi[g0:g0 + 1, :], wi[g1:g1 + 1, :])
        inv2 = jnp.where(row_lo, inv0, inv1)
        hout = jnp.transpose((num_t + wi2 * qct) * inv2)
        ws2 = jnp.where(lo_head, cols[:, (N_HEADS_C + g0) * LANES:(N_HEADS_C + g0 + 1) * LANES],
                        cols[:, (N_HEADS_C + g1) * LANES:(N_HEADS_C + g1 + 1) * LANES])
        kw = k2 * ws2
        keep_t = jnp.where(row_lo, wc[g0:g0 + 1, :], wc[g1:g1 + 1, :])
        ct_scr[sq, p] = jnp.where(blockdiag, keep_t * ct + _dot(vt_b, kw.astype(BF16)), 0.0)
        n_scr[sq, pl.ds(p, 1), :] = jnp.where(lo_row, wc[g0:g0 + 1, :], wc[g1:g1 + 1, :]) * n_pair \
            + jnp.sum(kw, axis=0, keepdims=True)
        hc = jax.nn.sigmoid(o_ref[sq]) * hout
        hsq = hc * hc
        ms0 = jnp.sum(jnp.where(lo_head, hsq, 0.0), axis=1, keepdims=True)
        ms1 = jnp.sum(jnp.where(lo_head, 0.0, hsq), axis=1, keepdims=True)
        ms = jnp.where(lo_head, ms0, ms1) * (1.0 / HEAD_DIM)
        y_ref[sq, :, p * LANES:(p + 1) * LANES] = \
            hc * lax.rsqrt(ms + EPS) * gain_ref[:, p * LANES:(p + 1) * LANES]

    sums = [cumsum_stage(sq) for sq in range(n_seq)]
    staged = [score_stage(sq, p) for sq, p in units]
    gates = [gate_stage(sq, *sums[sq]) for sq in range(n_seq)]
    gates = [gt[:5] + (_dot(gt[5], sel_ref[...]),) for gt in gates]
    weights = [weight_stage(p, gates[sq], st) for (sq, p), st in zip(units, staged)]
    for (sq, p), st, wt in zip(units, staged, weights):
        value_stage(sq, p, gates[sq], st, wt)
    for sq in range(n_seq):
        m_scr[sq] = jnp.where(is_head_row, gates[sq][4], 0.0)

    @pl.when(ck == pl.num_programs(1) - 1)
    def _():
        for sq, p in units:
            c_pair = jnp.transpose(ct_scr[sq, p])
            c_out_ref[sq, 2 * p] = c_pair[0:HEAD_DIM, 0:HEAD_DIM]
            c_out_ref[sq, 2 * p + 1] = c_pair[HEAD_DIM:, HEAD_DIM:]
        n_out_ref[...] = n_scr[:, 0:2, :]
        m_out_ref[...] = m_scr[...]


def _mlstm_t(z3, col0, g_rows, n_valid, sw, layer, C0, n0, m0):
    b, t_len, _ = z3.shape
    L = MLSTM_CHUNK
    nc = t_len // L
    G = ML_GROUP
    tri_t, sel = _mlstm_consts()
    m_rows = jnp.broadcast_to(jnp.pad(m0, ((0, 0), (0, SUBLANES - N_HEADS_C)))[:, :, None],
                              (b, SUBLANES, LANES))

    def col(cb):
        return pl.BlockSpec((G, L, LANES), lambda i, c: (i, c, cb - col0))

    state_c = pl.BlockSpec((G, N_HEADS_C, HEAD_DIM, HEAD_DIM), lambda i, c: (i, 0, 0, 0))
    state_m = pl.BlockSpec((G, SUBLANES, LANES), lambda i, c: (i, 0, 0))
    y, c1, n1, m1 = pl.pallas_call(
        functools.partial(_mlstm_t_kernel, n_valid),
        grid=(b // G, nc),
        in_specs=[col(ZC_QC), col(ZC_QC + 1), col(ZC_KC), col(ZC_KC + 1), col(ZC_VC), col(ZC_VC + 1),
                  col(ZC_OC), col(ZC_OC + 1),
                  pl.BlockSpec((G, SUBLANES, L), lambda i, c: (i, 0, c)),
                  _layer_spec((SUBLANES, 1), layer), _layer_spec((1, D_C), layer),
                  _const_spec((L, L)), _const_spec((3 * LANES, N_GATES * LANES)),
                  state_c, pl.BlockSpec((G, N_HEADS_C, HEAD_DIM), lambda i, c: (i, 0, 0)), state_m],
        out_specs=[pl.BlockSpec((G, L, D_C), lambda i, c: (i, c, 0)), state_c,
                   pl.BlockSpec((G, 2, LANES), lambda i, c: (i, 0, 0)), state_m],
        out_shape=[jax.ShapeDtypeStruct((b, t_len, D_C), F32),
                   jax.ShapeDtypeStruct((b, N_HEADS_C, HEAD_DIM, HEAD_DIM), F32),
                   jax.ShapeDtypeStruct((b, 2, LANES), F32),
                   jax.ShapeDtypeStruct((b, SUBLANES, LANES), F32)],
        scratch_shapes=[pltpu.VMEM((G, 2, LANES, LANES), F32), pltpu.VMEM((G, SUBLANES, LANES), F32),
                        pltpu.VMEM((G, SUBLANES, LANES), F32)],
        compiler_params=_params(("parallel", "arbitrary")),
        name="mlstm",
    )(*([z3] * 8), g_rows, sw['b_mlstm_gate'], sw['g_out_c'], tri_t, sel, C0, n0, m_rows)
    return y, c1, n1.reshape(b, N_HEADS_C, HEAD_DIM), m1[:, 0:N_HEADS_C, 0]


def _block_diag(w):
    depth, nb, bs, _ = w.shape
    eye = jnp.eye(nb, dtype=w.dtype)
    return jnp.einsum('lncd,nm->lncmd', w, eye).reshape(depth, nb * bs, nb * bs)


def _stack_weights(p):
    vec = lambda v: v.reshape(v.shape[0], 1, -1)
    bf = lambda w: w.astype(BF16)
    out = {k: vec(p[k]) for k in (
        'g_f1_pre', 'g_f1_post', 'g_mix_pre', 'conv_b', 'b_rg_a', 'b_rg_x', 'lru_lambda', 'g_out_a',
        'g_out_b', 'g_out_c', 'g_mix_post', 'g_f2_pre', 'g_f2_post', 'g_ple_pre', 'g_ple_post')}
    out.update({k: bf(p[k]) for k in (
        'w_f1_gate', 'w_f1_up', 'w_f1_down', 'w_f2_gate', 'w_f2_up', 'w_f2_down', 'w_out',
        'w_ple_gate', 'w_ple_proj')})
    out['w_in'] = bf(jnp.pad(p['w_in'], ((0, 0), (0, 0), (0, Z_COLS - p['w_in'].shape[-1]))))
    out['conv_w'] = p['conv_w']
    out['w_rg_a'] = bf(_block_diag(p['w_rg_a']))
    out['w_rg_x'] = bf(_block_diag(p['w_rg_x']))
    out['b_mlstm_gate'] = jnp.concatenate([p['b_mlstm_i'], p['b_mlstm_f']], axis=1)[:, :, None]
    return out


def _gate_rows(z3, col0):
    lo = (ZC_GATE - col0) * LANES
    return jnp.swapaxes(z3[:, :, lo:lo + N_GATES], 1, 2)


def _layer(x, pe_all, sw, layer, cache, kv_bufs=None):
    b, t_len, d = x.shape
    if cache is None:
        rows, z, new_k, new_v, z_cls, g_rows = _premix(x.reshape(b * t_len, d), sw, layer,
                                                       kv_bufs, t_len)
        z3 = z.reshape(b, t_len, Z_COLS)
        ya = _attn_prompt(z3, z_cls)
        conv_buf = jnp.zeros((b, CONV_W - 1, D_B), F32)
        h0 = jnp.zeros((b, D_B), F32)
        C0 = jnp.zeros((b, N_HEADS_C, HEAD_DIM, HEAD_DIM), F32)
        n0 = jnp.zeros((b, N_HEADS_C, HEAD_DIM), F32)
        m0 = jnp.zeros((b, N_HEADS_C), F32)
        zc, col0, n_valid = z3, 0, MLSTM_CHUNK
    else:
        kt_all, vt_all, conv_buf, h0, C0, n0, m0 = cache
        rows, z = _premix(x.reshape(b * t_len, d), sw, layer)
        z3 = z.reshape(b, t_len, Z_COLS)
        new_k = z3[:, :, ZC_KA * LANES:ZC_KA * LANES + D_A].reshape(b, t_len, N_HEADS_A, HEAD_DIM)
        new_v = z3[:, :, ZC_VA * LANES:ZC_VA * LANES + D_A].reshape(b, t_len, N_HEADS_A, HEAD_DIM)
        ya = _attn_sample(z3, kt_all, vt_all, layer)
        col0 = ZC_QC
        zc = jnp.pad(z3[:, :, col0 * LANES:], ((0, 0), (0, MLSTM_CHUNK - t_len), (0, 0)))
        g_rows = _gate_rows(zc, col0)
        n_valid = t_len
    yb, new_buf, h_last = _rglru(z3, conv_buf, h0, sw, layer)
    yc, C1, n1, m1 = _mlstm_t(zc, col0, g_rows, n_valid, sw, layer, C0, n0, m0)
    yc = yc[:, :t_len]
    rows = _postmix(rows, ya.reshape(b * t_len, D_A), yb.reshape(b * t_len, D_B),
                    yc.reshape(b * t_len, D_C), pe_all, sw, layer)
    return rows.reshape(b, t_len, d), (new_k, new_v, new_buf, h_last, C1, n1, m1)


def _feature_major(cache):
    depth, b, w_buf, nh, dh = cache.shape
    return jnp.transpose(cache, (0, 1, 3, 4, 2)).reshape(depth, b, nh * dh, w_buf)


def kernel(x_prompt, x_sample, cache_k, cache_v, state_conv, state_h, state_C, state_n, state_m, p_prompt, p_sample, g_f1_pre, w_f1_gate, w_f1_up, w_f1_down, g_f1_post, g_mix_pre, w_in, conv_w, conv_b, w_rg_a, b_rg_a, w_rg_x, b_rg_x, lru_lambda, b_mlstm_i, b_mlstm_f, g_out_a, g_out_b, g_out_c, w_out, g_mix_post, g_f2_pre, w_f2_gate, w_f2_up, w_f2_down, g_f2_post, g_ple_pre, w_ple_gate, w_ple_proj, g_ple_post):
    depth = w_in.shape[0]
    sw = _stack_weights(dict(
        g_f1_pre=g_f1_pre, w_f1_gate=w_f1_gate, w_f1_up=w_f1_up, w_f1_down=w_f1_down,
        g_f1_post=g_f1_post, g_mix_pre=g_mix_pre, w_in=w_in, conv_w=conv_w, conv_b=conv_b,
        w_rg_a=w_rg_a, b_rg_a=b_rg_a, w_rg_x=w_rg_x, b_rg_x=b_rg_x, lru_lambda=lru_lambda,
        b_mlstm_i=b_mlstm_i, b_mlstm_f=b_mlstm_f, g_out_a=g_out_a, g_out_b=g_out_b,
        g_out_c=g_out_c, w_out=w_out, g_mix_post=g_mix_post, g_f2_pre=g_f2_pre,
        w_f2_gate=w_f2_gate, w_f2_up=w_f2_up, w_f2_down=w_f2_down, g_f2_post=g_f2_post,
        g_ple_pre=g_ple_pre, w_ple_gate=w_ple_gate, w_ple_proj=w_ple_proj, g_ple_post=g_ple_post))
    xp, xs = x_prompt, x_sample
    pe_prompt = p_prompt.reshape(depth, -1, p_prompt.shape[-1])
    pe_sample = p_sample.reshape(depth, -1, p_sample.shape[-1])
    kt_all = _feature_major(cache_k)
    vt_all = _feature_major(cache_v)
    b_p, s_p, _ = x_prompt.shape
    keep = min(DILATED_CONFIGS[-1][0], s_p)
    kv_bufs = (jnp.zeros((depth, b_p, D_A, keep), F32), jnp.zeros((depth, b_p, D_A, keep), F32))
    sp = [[] for _ in range(5)]
    ss = [[] for _ in range(7)]
    for i in range(depth):
        xp, st_p = _layer(xp, pe_prompt, sw, i, None, kv_bufs)
        kv_bufs = st_p[:2]
        cache_i = (kt_all, vt_all, state_conv[i], state_h[i], state_C[i], state_n[i], state_m[i])
        xs, st_s = _layer(xs, pe_sample, sw, i, cache_i)
        for j in range(5):
            sp[j].append(st_p[2 + j])
        for j in range(7):
            ss[j].append(st_s[j])
    k_prompt, v_prompt = [
        jnp.transpose(buf.reshape(depth, b_p, N_HEADS_A, HEAD_DIM, keep), (0, 1, 4, 2, 3))
        for buf in kv_bufs]
    conv_prompt, h_prompt, C_prompt, n_prompt, m_prompt = [jnp.stack(a) for a in sp]
    k_sample, v_sample, conv_sample, h_sample, C_sample, n_sample, m_sample = [jnp.stack(a) for a in ss]
    return (xp, xs, k_prompt, v_prompt, k_sample, v_sample, conv_prompt, conv_sample,
            h_prompt, h_sample, C_prompt, C_sample, n_prompt, n_sample, m_prompt, m_sample)
```

```python
import functools

import jax
import jax.numpy as jnp
from jax import lax
from jax.experimental import pallas as pl
from jax.experimental.pallas import tpu as pltpu

F32 = jnp.float32
BF16 = jnp.bfloat16

EPS = 1e-6
NEG = -1e30
LOG2_E = 1.4426950408889634
SQRT_2_OVER_PI = 0.7978845608028654
GELU_CUBIC = 0.044715
HEAD_DIM = 64
N_HEADS_A = 6
D_A = N_HEADS_A * HEAD_DIM
D_B = 384
N_HEADS_C = 4
D_C = N_HEADS_C * HEAD_DIM
N_GATES = 2 * N_HEADS_C
DILATED_CONFIGS = ((128, 1), (512, 4), (2048, 16))
LOCAL_BLK = 128
CONV_W = 4
LRU_C = 8.0
MLSTM_CHUNK = 128
LANES = 128
SUBLANES = 8
Z_COLS = 3072
FF_CHUNK = 256
ROW_TILE = 512
VMEM_LIMIT = 58 * 1024 * 1024

ZC_QA, ZC_KA, ZC_VA, ZC_XB, ZC_GB = 0, 3, 6, 9, 12
ZC_QC, ZC_KC, ZC_VC, ZC_OC, ZC_GATE = 15, 17, 19, 21, 23


def _rms(x, g):
    return x * lax.rsqrt(jnp.mean(x * x, axis=-1, keepdims=True) + EPS) * g


def _dot(a, b):
    return jnp.dot(a, b, preferred_element_type=F32)


def _dot_nt(a, b):
    return lax.dot_general(a, b, (((1,), (1,)), ((), ())), preferred_element_type=F32)


def _softplus(x):
    return jnp.maximum(x, 0.0) + jnp.log1p(jnp.exp(-jnp.abs(x)))


def _log_sigmoid(x):
    return -_softplus(-x)


def _sigmoid_tanh(x):
    return 0.5 * jnp.tanh(0.5 * x) + 0.5


def _gelu_tanh(x):
    return 0.5 * x * (1.0 + jnp.tanh(SQRT_2_OVER_PI * (x + GELU_CUBIC * (x * x * x))))


def _const_spec(shape):
    nd = len(shape)
    return pl.BlockSpec(shape, lambda *_: (0,) * nd, pipeline_mode=pl.Buffered(1))


def _layer_spec(shape, layer, block=None):
    nd = len(shape)
    idx = (0,) * nd if block is None else block
    return pl.BlockSpec((None,) + tuple(shape), lambda *_: (layer,) + idx,
                        pipeline_mode=pl.Buffered(1))


def _params(sem):
    return pltpu.CompilerParams(dimension_semantics=sem, vmem_limit_bytes=VMEM_LIMIT)


def _swiglu_into(acc_ref, h, wg_ref, wu_ref, wd_ref):
    for c in range(wg_ref.shape[1] // FF_CHUNK):
        sl = slice(c * FF_CHUNK, (c + 1) * FF_CHUNK)
        g = _dot(h, wg_ref[:, sl])
        u = _dot(h, wu_ref[:, sl])
        a = (g * jax.nn.sigmoid(g) * u).astype(BF16)
        d = _dot(a, wd_ref[sl, :])
        if c == 0:
            acc_ref[...] = d
        else:
            acc_ref[...] += d


def _premix_kernel(emit_kv, x_ref, gpre_ref, wg_ref, wu_ref, wd_ref, gpost_ref, gmix_ref, win_ref,
                   *rest):
    if emit_kv:
        _, _, xo_ref, z_ref, kt_ref, vt_ref, zc_ref, gt_ref, acc_ref, stage_ref = rest
    else:
        xo_ref, z_ref, acc_ref = rest
    x = x_ref[...]
    _swiglu_into(acc_ref, _rms(x, gpre_ref[...]).astype(BF16), wg_ref, wu_ref, wd_ref)
    x = x + 0.5 * _rms(acc_ref[...], gpost_ref[...])
    xo_ref[...] = x
    h = _rms(x, gmix_ref[...]).astype(BF16)
    for c in range(win_ref.shape[1] // 256):
        sl = slice(c * 256, (c + 1) * 256)
        zc = _dot(h, win_ref[:, sl])
        z_ref[:, sl] = zc
        if not emit_kv:
            continue
        n_cls, rows_per_cls, width = zc_ref.shape
        for half in range(2):
            lg = 2 * c + half
            if lg == ZC_GATE:
                gt_ref[...] = zc[:, half * LANES:(half + 1) * LANES].T[0:N_GATES, :]
            if lg >= width // LANES:
                continue
            cols = zc[:, half * LANES:(half + 1) * LANES]
            lanes = slice(lg * LANES, (lg + 1) * LANES)
            if ZC_KA <= lg < ZC_KA + D_A // LANES:
                kt_ref[(lg - ZC_KA) * LANES:(lg - ZC_KA + 1) * LANES, :] = cols.T
            if ZC_VA <= lg < ZC_VA + D_A // LANES:
                vt_ref[(lg - ZC_VA) * LANES:(lg - ZC_VA + 1) * LANES, :] = cols.T
            stage_ref[lg % 2] = cols
            for r in range(n_cls):
                zc_ref[r, :, lanes] = stage_ref[lg % 2, pl.ds(r, rows_per_cls, stride=n_cls), :]


def _premix(x, sw, layer, kv_bufs=None, seq_len=None):
    m, d = x.shape
    ff = sw['w_f1_gate'].shape[-1]
    tm = min(ROW_TILE, m)
    row = pl.BlockSpec((tm, d), lambda i: (i, 0))
    vec = _layer_spec((1, d), layer)
    in_specs = [row, vec, _layer_spec((d, ff), layer), _layer_spec((d, ff), layer),
                _layer_spec((ff, d), layer), vec, vec, _layer_spec((d, Z_COLS), layer)]
    out_specs = [row, pl.BlockSpec((tm, Z_COLS), lambda i: (i, 0))]
    out_shape = [jax.ShapeDtypeStruct((m, d), F32), jax.ShapeDtypeStruct((m, Z_COLS), F32)]
    args = [x, sw['g_f1_pre'], sw['w_f1_gate'], sw['w_f1_up'], sw['w_f1_down'], sw['g_f1_post'],
            sw['g_mix_pre'], sw['w_in']]
    aliases = {}
    scratch = [pltpu.VMEM((tm, d), F32)]
    if kv_bufs is not None:
        keep = kv_bufs[0].shape[-1]
        tiles_per_seq = seq_len // tm
        first_kept = (seq_len - keep) // tm
        kv_spec = pl.BlockSpec(
            (None, None, D_A, tm),
            lambda i: (layer, i // tiles_per_seq, 0, jnp.maximum(i % tiles_per_seq - first_kept, 0)))
        for buf in kv_bufs:
            aliases[len(args)] = len(out_shape)
            in_specs.append(pl.BlockSpec(memory_space=pl.ANY))
            args.append(buf)
            out_specs.append(kv_spec)
            out_shape.append(jax.ShapeDtypeStruct(buf.shape, F32))
        n_cls = DILATED_CONFIGS[-1][1]
        out_specs.append(pl.BlockSpec((None, n_cls, tm // n_cls, 3 * D_A),
                                      lambda i: (i // tiles_per_seq, 0, i % tiles_per_seq, 0)))
        out_shape.append(jax.ShapeDtypeStruct((m // seq_len, n_cls, seq_len // n_cls, 3 * D_A), F32))
        out_specs.append(pl.BlockSpec((None, N_GATES, tm),
                                      lambda i: (i // tiles_per_seq, 0, i % tiles_per_seq)))
        out_shape.append(jax.ShapeDtypeStruct((m // seq_len, N_GATES, seq_len), F32))
        scratch.append(pltpu.VMEM((2, tm, LANES), F32))
    return pl.pallas_call(
        functools.partial(_premix_kernel, kv_bufs is not None),
        grid=(m // tm,),
        in_specs=in_specs,
        out_specs=out_specs,
        out_shape=out_shape,
        scratch_shapes=scratch,
        input_output_aliases=aliases,
        compiler_params=_params(("arbitrary",)),
        name="premix",
    )(*args)


def _postmix_kernel(x_ref, ya_ref, yb_ref, yc_ref, pe_ref, ga_ref, wa_ref, wb_ref, wc_ref, gmix_ref,
                    gpre_ref, wg_ref, wu_ref, wd_ref, gpost_ref,
                    gple_ref, wpg_ref, wpp_ref, gple_post_ref, o_ref, acc_ref, x_scr):
    ya = _rms(ya_ref[...], ga_ref[...]).astype(BF16)
    y = _dot(ya, wa_ref[...])
    y = y + _dot(yb_ref[...].astype(BF16), wb_ref[...])
    y = y + _dot(yc_ref[...].astype(BF16), wc_ref[...])
    x_scr[...] = x_ref[...] + _rms(y, gmix_ref[...])
    _swiglu_into(acc_ref, _rms(x_scr[...], gpre_ref[...]).astype(BF16), wg_ref, wu_ref, wd_ref)
    x_scr[...] = x_scr[...] + 0.5 * _rms(acc_ref[...], gpost_ref[...])
    h = _rms(x_scr[...], gple_ref[...]).astype(BF16)
    gate = jax.nn.sigmoid(_dot(h, wpg_ref[...]))
    proj = _dot(pe_ref[...].astype(BF16), wpp_ref[...])
    o_ref[...] = x_scr[...] + _rms(gate * proj, gple_post_ref[...])


def _postmix(x, ya, yb, yc, pe_all, sw, layer):
    m, d = x.shape
    dp = pe_all.shape[-1]
    ff = sw['w_f2_gate'].shape[-1]
    tm = min(ROW_TILE, m)

    def row(width):
        return pl.BlockSpec((tm, width), lambda i: (i, 0))

    vec = _layer_spec((1, d), layer)
    return pl.pallas_call(
        _postmix_kernel,
        grid=(m // tm,),
        in_specs=[row(d), row(D_A), row(D_B), row(D_C),
                  pl.BlockSpec((None, tm, dp), lambda i: (layer, i, 0)),
                  _layer_spec((1, D_A), layer),
                  _layer_spec((D_A, d), layer, (0, 0)),
                  _layer_spec((D_B, d), layer, (D_A // D_B, 0)),
                  _layer_spec((D_C, d), layer, ((D_A + D_B) // D_C, 0)), vec,
                  vec, _layer_spec((d, ff), layer), _layer_spec((d, ff), layer),
                  _layer_spec((ff, d), layer), vec,
                  vec, _layer_spec((d, d), layer), _layer_spec((dp, d), layer), vec],
        out_specs=row(d),
        out_shape=jax.ShapeDtypeStruct((m, d), F32),
        scratch_shapes=[pltpu.VMEM((tm, d), F32), pltpu.VMEM((tm, d), F32)],
        compiler_params=_params(("parallel",)),
        name="postmix",
    )(x, ya, yb, yc, pe_all, sw['g_out_a'], sw['w_out'], sw['w_out'], sw['w_out'], sw['g_mix_post'],
      sw['g_f2_pre'], sw['w_f2_gate'], sw['w_f2_up'], sw['w_f2_down'], sw['g_f2_post'],
      sw['g_ple_pre'], sw['w_ple_gate'], sw['w_ple_proj'], sw['g_ple_post'])


ATT_QT = 2048
ATT_UNROLL = 16


ATT_PITCH = LOCAL_BLK + 8


def _attn_prompt_kernel(q_ref, k_ref, v_ref, qc_ref, kc_ref, vc_ref, bias_ref, o_ref,
                        num_scr, m_scr, den_scr, cls_scr):
    t0 = pl.program_id(2) * ATT_QT
    blk = LOCAL_BLK
    lo_head = lax.broadcasted_iota(jnp.int32, (blk, LANES), 1) < HEAD_DIM
    scale = HEAD_DIM ** -0.5 * LOG2_E
    n_cfg = len(DILATED_CONFIGS)
    dil_c = DILATED_CONFIGS[n_cfg - 1][1]

    for ci, (_, dil) in enumerate(DILATED_CONFIGS):
        span = blk * dil
        class_major = span == ATT_QT

        def rows_of(j, dil=dil, span=span, class_major=class_major):
            if class_major:
                tile = pl.program_id(2)
                first = tile == 0
                cur = pl.multiple_of(tile * blk, 8)
                prev = jnp.where(first, cur, cur - blk)
                return ((j,), (j, pl.ds(cur, blk)), (j, pl.ds(pl.multiple_of(prev, 8), blk)), first)
            qs = j % dil + span * (j // dil)
            ks = t0 + qs
            first = ks < span
            ps = jnp.where(first, ks, ks - span)
            return ((pl.ds(qs, blk, stride=dil),), (pl.ds(ks, blk, stride=dil),),
                    (pl.ds(ps, blk, stride=dil),), first)

        q_src, k_src, v_src = (qc_ref, kc_ref, vc_ref) if class_major else (q_ref, k_ref, v_ref)

        def scores(j, rows_of=rows_of, q_src=q_src, k_src=k_src):
            q_rows, cur, prev, first = rows_of(j)
            q2 = q_src[q_rows] * scale
            kc = k_src[cur].astype(BF16)
            kp = k_src[prev].astype(BF16)
            q_both = jnp.concatenate([jnp.where(lo_head, q2, 0.0), jnp.where(lo_head, 0.0, q2)], axis=0)
            raw = _dot_nt(q_both.astype(BF16), jnp.concatenate([kp, kc], axis=0))
            return j, q_rows, cur, prev, first.astype(jnp.int32), raw

        def softmax(first, raw):
            s = raw + bias_ref[first]
            m = jnp.max(s, axis=-1, keepdims=True)
            p = jnp.exp2(s - m)
            return p.astype(BF16), m, jnp.sum(p, axis=-1, keepdims=True)

        def weighted_values(j, q_rows, cur, prev, probs, ci=ci, class_major=class_major, v_src=v_src):
            vc = v_src[cur].astype(BF16)
            vp = v_src[prev].astype(BF16)
            p_both, m, den = probs
            num = _dot(p_both, jnp.concatenate([vp, vc], axis=0))
            num = jnp.where(lo_head, num[:blk], num[blk:])
            m = jnp.where(lo_head, m[:blk], m[blk:])
            den = jnp.where(lo_head, den[:blk], den[blk:])
            if class_major:
                dst = pl.ds(pl.multiple_of(j * ATT_PITCH, 8), blk)
                cls_scr[0, dst, :] = num
                cls_scr[1, dst, :] = m
                cls_scr[2, dst, :] = den
            else:
                num_scr[(ci,) + q_rows] = num
                m_scr[(ci,) + q_rows] = m
                den_scr[(ci,) + q_rows] = den

        def body(jj, carry, scores=scores, softmax=softmax, weighted_values=weighted_values):
            staged = [scores(jj * ATT_UNROLL + u) for u in range(ATT_UNROLL)]
            probs = [softmax(st[4], st[5]) for st in staged]
            for st, pr in zip(staged, probs):
                weighted_values(st[0], st[1], st[2], st[3], pr)
            return carry

        lax.fori_loop(0, ATT_QT // blk // ATT_UNROLL, body, 0)

    rows = 256

    def combine(i, carry):
        sl = pl.ds(pl.multiple_of(i * rows, rows), rows)

        def class_rows(a):
            base = i * (rows // dil_c)
            return jnp.concatenate(
                [cls_scr[a, pl.ds(base + g, dil_c, stride=ATT_PITCH), :] for g in range(rows // dil_c)],
                axis=0)

        ms = [m_scr[0, sl, :], m_scr[1, sl, :], class_rows(1)]
        nums = [num_scr[0, sl, :], num_scr[1, sl, :], class_rows(0)]
        dens = [den_scr[0, sl, :], den_scr[1, sl, :], class_rows(2)]
        mx = jnp.maximum(jnp.maximum(ms[0], ms[1]), ms[2])
        ws = [jnp.exp2(mc - mx) for mc in ms]
        num = nums[0] * ws[0] + nums[1] * ws[1] + nums[2] * ws[2]
        den = dens[0] * ws[0] + dens[1] * ws[1] + dens[2] * ws[2]
        o_ref[sl, :] = num / den
        return carry

    lax.fori_loop(0, ATT_QT // rows, combine, 0)


def _attn_bias():
    blk = LOCAL_BLK
    qi = jnp.arange(2 * blk)[:, None] % blk
    kj = jnp.arange(2 * blk)[None, :]
    prev_ok = (kj < blk) & (kj >= qi)
    cur_ok = (kj >= blk) & (kj - blk <= qi)
    normal = jnp.where(prev_ok | cur_ok, 0.0, NEG)
    first = jnp.where(cur_ok, 0.0, NEG)
    return jnp.stack([normal, first]).astype(F32)


def _attn_prompt(z3, zc):
    b, s, _ = z3.shape
    n_pairs = D_A // LANES
    dil_c = DILATED_CONFIGS[-1][1]
    assert dil_c * LOCAL_BLK == ATT_QT and len(DILATED_CONFIGS) == 3
    q_spec = pl.BlockSpec((None, ATT_QT, LANES), lambda i, p, t: (i, t, ZC_QA + p))
    k_spec = pl.BlockSpec((None, s, LANES), lambda i, p, t: (i, 0, ZC_KA + p))
    v_spec = pl.BlockSpec((None, s, LANES), lambda i, p, t: (i, 0, ZC_VA + p))
    qc_spec = pl.BlockSpec((None, dil_c, LOCAL_BLK, LANES), lambda i, p, t: (i, 0, t, ZC_QA + p))
    kc_spec = pl.BlockSpec((None, dil_c, s // dil_c, LANES), lambda i, p, t: (i, 0, 0, ZC_KA + p))
    vc_spec = pl.BlockSpec((None, dil_c, s // dil_c, LANES), lambda i, p, t: (i, 0, 0, ZC_VA + p))
    return pl.pallas_call(
        _attn_prompt_kernel,
        grid=(b, n_pairs, s // ATT_QT),
        in_specs=[q_spec, k_spec, v_spec, qc_spec, kc_spec, vc_spec,
                  _const_spec((2, 2 * LOCAL_BLK, 2 * LOCAL_BLK))],
        out_specs=pl.BlockSpec((None, ATT_QT, LANES), lambda i, p, t: (i, t, p)),
        out_shape=jax.ShapeDtypeStruct((b, s, D_A), F32),
        scratch_shapes=[pltpu.VMEM((2, ATT_QT, LANES), F32)] * 3
        + [pltpu.VMEM((3, dil_c * ATT_PITCH, LANES), F32)],
        compiler_params=_params(("parallel", "parallel", "arbitrary")),
        name="attn_prompt",
    )(z3, z3, z3, zc, zc, zc, _attn_bias())


def _attn_sample_kernel(q_ref, kn_ref, vn_ref, kt_ref, vt_ref, o_ref, q_scr, kn_scr, vn_scr):
    t_new = q_ref.shape[0]
    w_buf = kt_ref.shape[1]
    q_scr[...] = jnp.zeros(q_scr.shape, F32)
    kn_scr[...] = jnp.zeros(kn_scr.shape, F32)
    vn_scr[...] = jnp.zeros(vn_scr.shape, F32)
    q_scr[0:t_new, :] = q_ref[...] * (HEAD_DIM ** -0.5)
    kn_scr[0:t_new, :] = kn_ref[...]
    vn_scr[0:t_new, :] = vn_ref[...]

    rows = 2 * SUBLANES
    tq = lax.broadcasted_iota(jnp.int32, (rows, w_buf), 0) & (SUBLANES - 1)
    dist = w_buf + tq - lax.broadcasted_iota(jnp.int32, (rows, w_buf), 1)
    tqn = lax.broadcasted_iota(jnp.int32, (rows, LANES), 0) & (SUBLANES - 1)
    tn = lax.broadcasted_iota(jnp.int32, (rows, LANES), 1)
    dist_new = tqn - tn
    cache_ok, new_ok = [], []
    for window, dil in DILATED_CONFIGS:
        ok = dist <= window
        nk = jnp.logical_and(dist_new >= 0, tn < t_new)
        nk = jnp.logical_and(nk, dist_new <= window)
        if dil > 1:
            ok = jnp.logical_and(ok, (dist & (dil - 1)) == 0)
            nk = jnp.logical_and(nk, (dist_new & (dil - 1)) == 0)
        cache_ok.append(ok)
        new_ok.append(nk)
    lo8 = lax.broadcasted_iota(jnp.int32, (SUBLANES, LANES), 1) < HEAD_DIM

    for p in range(D_A // LANES):
        sl = slice(p * LANES, (p + 1) * LANES)
        q2 = q_scr[:, sl]
        qm = jnp.concatenate([jnp.where(lo8, q2, 0.0), jnp.where(lo8, 0.0, q2)], axis=0).astype(BF16)
        kb = kt_ref[sl, :].astype(BF16)
        vb = vt_ref[sl, :].astype(BF16)
        s = _dot(qm, kb)
        s_new = _dot_nt(qm, kn_scr[:, sl].astype(BF16))
        parts = []
        for c in range(len(DILATED_CONFIGS)):
            sc = jnp.where(cache_ok[c], s, NEG)
            sn = jnp.where(new_ok[c], s_new, NEG)
            m = jnp.maximum(jnp.max(sc, axis=1, keepdims=True), jnp.max(sn, axis=1, keepdims=True))
            pc = jnp.exp(sc - m)
            pn = jnp.exp(sn - m)
            den = jnp.sum(pc, axis=1, keepdims=True) + jnp.sum(pn, axis=1, keepdims=True)
            parts.append((pc, pn, m, den))
        mx = jnp.maximum(jnp.maximum(parts[0][2], parts[1][2]), parts[2][2])
        ws = [jnp.exp(pt[2] - mx) for pt in parts]
        den_all = parts[0][3] * ws[0] + parts[1][3] * ws[1] + parts[2][3] * ws[2]
        coefs = [w / den_all for w in ws]
        p_all = parts[0][0] * coefs[0] + parts[1][0] * coefs[1] + parts[2][0] * coefs[2]
        pn_all = parts[0][1] * coefs[0] + parts[1][1] * coefs[1] + parts[2][1] * coefs[2]
        o16 = _dot_nt(p_all.astype(BF16), vb) + _dot(pn_all.astype(BF16), vn_scr[:, sl].astype(BF16))
        o8 = jnp.where(lo8, o16[0:SUBLANES], o16[SUBLANES:rows])
        o_ref[:, sl] = o8[0:t_new]


def _attn_sample(zs3, kt_all, vt_all, layer):
    b, t_new, _ = zs3.shape
    w_buf = kt_all.shape[-1]

    def new_spec(col):
        return pl.BlockSpec((None, t_new, D_A), lambda i: (i, 0, col))

    cache_spec = pl.BlockSpec((None, None, D_A, w_buf), lambda i: (layer, i, 0, 0))
    return pl.pallas_call(
        _attn_sample_kernel,
        grid=(b,),
        in_specs=[new_spec(0), new_spec(1), new_spec(2), cache_spec, cache_spec],
        out_specs=pl.BlockSpec((None, t_new, D_A), lambda i: (i, 0, 0)),
        out_shape=jax.ShapeDtypeStruct((b, t_new, D_A), F32),
        scratch_shapes=[pltpu.VMEM((SUBLANES, D_A), F32), pltpu.VMEM((LANES, D_A), F32),
                        pltpu.VMEM((LANES, D_A), F32)],
        compiler_params=_params(("parallel",)),
        name="attn_sample",
    )(zs3, zs3, zs3, kt_all, vt_all)


RG_GROUP = SUBLANES
RG_CHUNK = 256
RG_PAD = 8


def _rg_pitch(tcp):
    tiles = tcp // 8 + 1
    return 8 * (tiles if tiles % 2 else tiles + 1)


def _rglru_kernel(xb_ref, gb_ref, conv_ref, h0_ref, cw_ref, cb_ref, wa_ref, ba_ref, wx_ref, bx_ref,
                  lam_ref, g_ref, y_ref, convnew_ref, ht_ref, xp_scr, hist_scr, h_scr, a_scr, b_scr):
    n_seq, tc, _ = xb_ref.shape
    tcp = xp_scr.shape[0] - RG_PAD
    pitch = a_scr.shape[1] // n_seq
    n_hist = CONV_W - 1
    hist = slice(RG_PAD - n_hist, RG_PAD)
    n_groups = D_B // LANES

    @pl.when(pl.program_id(1) == 0)
    def _():
        hist_scr[:, hist, :] = conv_ref[...]
        h_scr[...] = h0_ref[...]

    if tc % 8:
        xp_scr[RG_PAD:, :] = jnp.zeros((tcp, D_B), F32)
    decay = _softplus(-lam_ref[...])

    def gates(g, carry):
        xp_scr[hist, :] = hist_scr[g, hist, :]
        xp_scr[RG_PAD:RG_PAD + tc, :] = xb_ref[g]
        hist_scr[g, hist, :] = xp_scr[RG_PAD + tc - n_hist:RG_PAD + tc, :]
        xc = cb_ref[...]
        for j in range(CONV_W):
            lo = RG_PAD - n_hist + j
            xc = xc + xp_scr[lo:lo + tcp, :] * cw_ref[pl.ds(j, 1), :]
        xcb = xc.astype(BF16)
        r = _sigmoid_tanh(_dot(xcb, wa_ref[...]) + ba_ref[...])
        gi = _sigmoid_tanh(_dot(xcb, wx_ref[...]) + bx_ref[...])
        a = jnp.exp(-LRU_C * r * decay)
        om = 1.0 - a * a
        bb = jnp.where(om > 0.0, om * lax.rsqrt(om), 0.0) * (gi * xc)
        dst = pl.ds(pl.multiple_of(g * pitch, 8), tcp)
        for lg in range(n_groups):
            a_scr[lg, dst, :] = a[:, lg * LANES:(lg + 1) * LANES]
            b_scr[lg, dst, :] = bb[:, lg * LANES:(lg + 1) * LANES]
        return carry

    lax.fori_loop(0, n_seq, gates, 0)

    def step(t, hs):
        rows = pl.ds(t, n_seq, stride=pitch)
        out = []
        for lg in range(n_groups):
            h = a_scr[lg, rows, :] * hs[lg] + b_scr[lg, rows, :]
            b_scr[lg, rows, :] = h
            out.append(h)
        return tuple(out)

    hs = tuple(h_scr[:, lg * LANES:(lg + 1) * LANES] for lg in range(n_groups))
    hs = lax.fori_loop(0, tc, step, hs, unroll=min(8, tc))
    for lg in range(n_groups):
        h_scr[:, lg * LANES:(lg + 1) * LANES] = hs[lg]

    def finish(g, carry):
        src = pl.ds(pl.multiple_of(g * pitch, 8), tcp)
        hseq = jnp.concatenate([b_scr[lg, src, :] for lg in range(n_groups)], axis=1)
        y = _rms(_gelu_tanh(gb_ref[g]) * hseq[0:tc], g_ref[...])
        y_ref[g] = y
        return carry

    lax.fori_loop(0, n_seq, finish, 0)
    convnew_ref[...] = hist_scr[:, hist, :]
    ht_ref[...] = h_scr[...]


def _rglru(z3, conv_buf, h0, sw, layer):
    b, t_len, _ = z3.shape
    tc = min(RG_CHUNK, t_len)
    tcp = -(-tc // 8) * 8
    pitch = _rg_pitch(tcp)
    n_hist = CONV_W - 1
    ng = b // RG_GROUP
    vec = _layer_spec((1, D_B), layer)
    mat = _layer_spec((D_B, D_B), layer)
    seq = pl.BlockSpec((RG_GROUP, tc, D_B), lambda i, c: (i, c, 0))
    hist = pl.BlockSpec((RG_GROUP, n_hist, D_B), lambda i, c: (i, 0, 0))
    state = pl.BlockSpec((None, RG_GROUP, D_B), lambda i, c: (i, 0, 0))
    y, conv_new, h_last = pl.pallas_call(
        _rglru_kernel,
        grid=(ng, t_len // tc),
        in_specs=[pl.BlockSpec((RG_GROUP, tc, D_B), lambda i, c: (i, c, ZC_XB // 3)),
                  pl.BlockSpec((RG_GROUP, tc, D_B), lambda i, c: (i, c, ZC_GB // 3)),
                  hist, state,
                  _layer_spec((CONV_W, D_B), layer), vec, mat, vec, mat, vec, vec, vec],
        out_specs=[seq, hist, state],
        out_shape=[jax.ShapeDtypeStruct((b, t_len, D_B), F32),
                   jax.ShapeDtypeStruct((b, n_hist, D_B), F32),
                   jax.ShapeDtypeStruct((ng, RG_GROUP, D_B), F32)],
        scratch_shapes=[pltpu.VMEM((RG_PAD + tcp, D_B), F32),
                        pltpu.VMEM((RG_GROUP, RG_PAD, D_B), F32),
                        pltpu.VMEM((RG_GROUP, D_B), F32),
                        pltpu.VMEM((D_B // LANES, RG_GROUP * pitch, LANES), F32),
                        pltpu.VMEM((D_B // LANES, RG_GROUP * pitch, LANES), F32)],
        compiler_params=_params(("parallel", "arbitrary")),
        name="rglru",
    )(z3, z3, conv_buf, h0.reshape(ng, RG_GROUP, D_B), sw['conv_w'], sw['conv_b'], sw['w_rg_a'],
      sw['b_rg_a'], sw['w_rg_x'], sw['b_rg_x'], sw['lru_lambda'], sw['g_out_b'])
    return y, conv_new, h_last.reshape(b, D_B)


ML_GROUP = 8


def _split3(x, axis):
    hi = x.astype(BF16).astype(F32)
    r1 = x - hi
    mid = r1.astype(BF16).astype(F32)
    lo = (r1 - mid).astype(BF16).astype(F32)
    return jnp.concatenate([hi, mid, lo], axis=axis).astype(BF16)


def _sum3(x, axis):
    n = x.shape[axis] // 3
    if axis == 0:
        return x[0:n] + x[n:2 * n] + x[2 * n:3 * n]
    return x[:, 0:n] + x[:, n:2 * n] + x[:, 2 * n:3 * n]


def _mlstm_consts():
    L = MLSTM_CHUNK
    t = jnp.arange(L)
    tri_t = (t[:, None] <= t[None, :]).astype(BF16)
    src = jnp.arange(3 * LANES) % LANES
    dst = jnp.arange(N_GATES * LANES) // LANES
    sel = (src[:, None] == dst[None, :]).astype(BF16)
    return tri_t, sel


def _mlstm_t_kernel(n_valid, q0_ref, q1_ref, k0_ref, k1_ref, v0_ref, v1_ref, o0_ref, o1_ref,
                    gr_ref, bias_ref, gain_ref, trit_ref, sel_ref,
                    c0_ref, n0_ref, m0_ref, y_ref, c_out_ref, n_out_ref, m_out_ref,
                    ct_scr, n_scr, m_scr):
    ck = pl.program_id(1)
    n_seq, _, L = gr_ref.shape

    lane = lax.broadcasted_iota(jnp.int32, (L, LANES), 1)
    row = lax.broadcasted_iota(jnp.int32, (L, LANES), 0)
    lo_head = lane < HEAD_DIM
    src_le_dst = row <= lane
    row_lo = row < HEAD_DIM
    blockdiag = ((row ^ lane) & HEAD_DIM) == 0
    sub = lax.broadcasted_iota(jnp.int32, (SUBLANES, L), 0)
    tl = lax.broadcasted_iota(jnp.int32, (SUBLANES, L), 1)
    lo_row = tl[0:1, :] < HEAD_DIM
    is_head_row = sub < N_HEADS_C
    refs = ((q0_ref, k0_ref, v0_ref, o0_ref), (q1_ref, k1_ref, v1_ref, o1_ref))
    units = [(sq, p) for sq in range(n_seq) for p in range(2)]
    pad_rows = jnp.zeros((L - SUBLANES, L), F32)

    @pl.when(ck == 0)
    def _():
        zero = jnp.zeros((HEAD_DIM, HEAD_DIM), F32)
        n_scr[...] = jnp.zeros(n_scr.shape, F32)
        for sq, p in units:
            top = jnp.concatenate([c0_ref[sq, 2 * p], zero], axis=1)
            bot = jnp.concatenate([zero, c0_ref[sq, 2 * p + 1]], axis=1)
            ct_scr[sq, p] = jnp.transpose(jnp.concatenate([top, bot], axis=0))
            n_scr[sq, pl.ds(p, 1), :] = jnp.concatenate(
                [n0_ref[sq, pl.ds(2 * p, 1), :], n0_ref[sq, pl.ds(2 * p + 1, 1), :]], axis=1)
        m_scr[...] = m0_ref[...]

    def cumsum_stage(sq):
        xr = gr_ref[sq] + bias_ref[...]
        gt = jnp.where(is_head_row, xr, _log_sigmoid(xr))
        if n_valid < L:
            gt = jnp.where(tl < n_valid, gt, jnp.where(is_head_row, NEG, 0.0))
        return gt, _sum3(_dot(_split3(gt, 0), trit_ref[...]), 0)

    def gate_stage(sq, gt, csum):
        b = pltpu.roll(csum, N_HEADS_C, axis=0)
        e = gt - b
        cm = e
        shift = 1
        while shift < L:
            cm = jnp.maximum(cm, jnp.where(tl >= shift, pltpu.roll(cm, shift, axis=1), NEG))
            shift *= 2
        m_prev = m_scr[sq]
        g = jnp.maximum(m_prev, cm)
        mt = g + b
        wi = jnp.exp(m_prev - g)
        emt = jnp.exp(-mt)
        m_new = jnp.broadcast_to(mt[:, L - 1:L], (SUBLANES, L))
        b_last = jnp.broadcast_to(b[:, L - 1:L], (SUBLANES, L))
        ws = jnp.exp(e + b_last - m_new)
        wc = jnp.exp(b_last + m_prev - m_new)
        both = jnp.where(is_head_row, e, pltpu.roll(ws, N_HEADS_C, axis=0))
        return g, wi, emt, wc, m_new, _split3(jnp.transpose(jnp.concatenate([both, pad_rows], axis=0)), 1)

    def score_stage(sq, p):
        q_ref, k_ref, v_ref, _ = refs[p]
        q2 = q_ref[sq]
        k2 = k_ref[sq] * (HEAD_DIM ** -0.5)
        q2b = q2.astype(BF16)
        k2b = k2.astype(BF16)
        vt_b = jnp.transpose(v_ref[sq]).astype(BF16)
        ct = ct_scr[sq, p]
        n_pair = n_scr[sq, pl.ds(p, 1), :]
        n2 = jnp.where(jnp.logical_or(jnp.logical_and(sub == 0, tl < HEAD_DIM),
                                      jnp.logical_and(sub == 1, tl >= HEAD_DIM)), n_pair, 0.0)
        qct = _dot_nt(ct.astype(BF16), q2b)
        qn = _dot_nt(n2.astype(BF16), q2b)
        st = [_dot_nt(k2b, jnp.where(lo_head if h == 0 else jnp.logical_not(lo_head), q2, 0.0)
                      .astype(BF16)) for h in range(2)]
        return k2, vt_b, ct, n_pair, qct, qn, st

    def weight_stage(p, gates, staged):
        g, wi, emt, _, _, cols = gates
        qn, st = staged[5], staged[6]
        out = []
        for h in range(2):
            gh = 2 * p + h
            e_c = cols[:, gh * LANES:(gh + 1) * LANES]
            at = jnp.exp(jnp.where(src_le_dst, e_c - g[gh:gh + 1, :], NEG)) * st[h]
            den = jnp.sum(at, axis=0, keepdims=True) + wi[gh:gh + 1, :] * qn[h:h + 1, :]
            inv = 1.0 / jnp.maximum(jnp.abs(den), emt[gh:gh + 1, :])
            out.append((at.astype(BF16), inv))
        return out

    def value_stage(sq, p, gates, staged, weights):
        o_ref = refs[p][3]
        _, wi, _, wc, m_new, cols = gates
        k2, vt_b, ct, n_pair, qct, _, _ = staged
        (at0, inv0), (at1, inv1) = weights
        g0, g1 = 2 * p, 2 * p + 1
        num_t = jnp.where(row_lo, _dot(vt_b, at0), _dot(vt_b, at1))
        wi2 = jnp.where(row_lo, wi[g0:g0 + 1, :], wi[g1:g1 + 1, :])
        inv2 = jnp.where(row_lo, inv0, inv1)
        hout = jnp.transpose((num_t + wi2 * qct) * inv2)
        ws2 = jnp.where(lo_head, cols[:, (N_HEADS_C + g0) * LANES:(N_HEADS_C + g0 + 1) * LANES],
                        cols[:, (N_HEADS_C + g1) * LANES:(N_HEADS_C + g1 + 1) * LANES])
        kw = k2 * ws2
        keep_t = jnp.where(row_lo, wc[g0:g0 + 1, :], wc[g1:g1 + 1, :])
        ct_scr[sq, p] = jnp.where(blockdiag, keep_t * ct + _dot(vt_b, kw.astype(BF16)), 0.0)
        n_scr[sq, pl.ds(p, 1), :] = jnp.where(lo_row, wc[g0:g0 + 1, :], wc[g1:g1 + 1, :]) * n_pair \
            + jnp.sum(kw, axis=0, keepdims=True)
        hc = jax.nn.sigmoid(o_ref[sq]) * hout
        hsq = hc * hc
        ms0 = jnp.sum(jnp.where(lo_head, hsq, 0.0), axis=1, keepdims=True)
        ms1 = jnp.sum(jnp.where(lo_head, 0.0, hsq), axis=1, keepdims=True)
        ms = jnp.where(lo_head, ms0, ms1) * (1.0 / HEAD_DIM)
        y_ref[sq, :, p * LANES:(p + 1) * LANES] = \
            hc * lax.rsqrt(ms + EPS) * gain_ref[:, p * LANES:(p + 1) * LANES]

    sums = [cumsum_stage(sq) for sq in range(n_seq)]
    staged = [score_stage(sq, p) for sq, p in units]
    gates = [gate_stage(sq, *sums[sq]) for sq in range(n_seq)]
    gates = [gt[:5] + (_dot(gt[5], sel_ref[...]),) for gt in gates]
    weights = [weight_stage(p, gates[sq], st) for (sq, p), st in zip(units, staged)]
    for (sq, p), st, wt in zip(units, staged, weights):
        value_stage(sq, p, gates[sq], st, wt)
    for sq in range(n_seq):
        m_scr[sq] = jnp.where(is_head_row, gates[sq][4], 0.0)

    @pl.when(ck == pl.num_programs(1) - 1)
    def _():
        for sq, p in units:
            c_pair = jnp.transpose(ct_scr[sq, p])
            c_out_ref[sq, 2 * p] = c_pair[0:HEAD_DIM, 0:HEAD_DIM]
            c_out_ref[sq, 2 * p + 1] = c_pair[HEAD_DIM:, HEAD_DIM:]
        n_out_ref[...] = n_scr[:, 0:2, :]
        m_out_ref[...] = m_scr[...]


def _mlstm_t(z3, col0, g_rows, n_valid, sw, layer, C0, n0, m0):
    b, t_len, _ = z3.shape
    L = MLSTM_CHUNK
    nc = t_len // L
    G = ML_GROUP
    tri_t, sel = _mlstm_consts()
    m_rows = jnp.broadcast_to(jnp.pad(m0, ((0, 0), (0, SUBLANES - N_HEADS_C)))[:, :, None],
                              (b, SUBLANES, LANES))

    def col(cb):
        return pl.BlockSpec((G, L, LANES), lambda i, c: (i, c, cb - col0))

    state_c = pl.BlockSpec((G, N_HEADS_C, HEAD_DIM, HEAD_DIM), lambda i, c: (i, 0, 0, 0))
    state_m = pl.BlockSpec((G, SUBLANES, LANES), lambda i, c: (i, 0, 0))
    y, c1, n1, m1 = pl.pallas_call(
        functools.partial(_mlstm_t_kernel, n_valid),
        grid=(b // G, nc),
        in_specs=[col(ZC_QC), col(ZC_QC + 1), col(ZC_KC), col(ZC_KC + 1), col(ZC_VC), col(ZC_VC + 1),
                  col(ZC_OC), col(ZC_OC + 1),
                  pl.BlockSpec((G, SUBLANES, L), lambda i, c: (i, 0, c)),
                  _layer_spec((SUBLANES, 1), layer), _layer_spec((1, D_C), layer),
                  _const_spec((L, L)), _const_spec((3 * LANES, N_GATES * LANES)),
                  state_c, pl.BlockSpec((G, N_HEADS_C, HEAD_DIM), lambda i, c: (i, 0, 0)), state_m],
        out_specs=[pl.BlockSpec((G, L, D_C), lambda i, c: (i, c, 0)), state_c,
                   pl.BlockSpec((G, 2, LANES), lambda i, c: (i, 0, 0)), state_m],
        out_shape=[jax.ShapeDtypeStruct((b, t_len, D_C), F32),
                   jax.ShapeDtypeStruct((b, N_HEADS_C, HEAD_DIM, HEAD_DIM), F32),
                   jax.ShapeDtypeStruct((b, 2, LANES), F32),
                   jax.ShapeDtypeStruct((b, SUBLANES, LANES), F32)],
        scratch_shapes=[pltpu.VMEM((G, 2, LANES, LANES), F32), pltpu.VMEM((G, SUBLANES, LANES), F32),
                        pltpu.VMEM((G, SUBLANES, LANES), F32)],
        compiler_params=_params(("parallel", "arbitrary")),
        name="mlstm",
    )(*([z3] * 8), g_rows, sw['b_mlstm_gate'], sw['g_out_c'], tri_t, sel, C0, n0, m_rows)
    return y, c1, n1.reshape(b, N_HEADS_C, HEAD_DIM), m1[:, 0:N_HEADS_C, 0]


def _block_diag(w):
    depth, nb, bs, _ = w.shape
    eye = jnp.eye(nb, dtype=w.dtype)
    return jnp.einsum('lncd,nm->lncmd', w, eye).reshape(depth, nb * bs, nb * bs)


def _stack_weights(p):
    vec = lambda v: v.reshape(v.shape[0], 1, -1)
    bf = lambda w: w.astype(BF16)
    out = {k: vec(p[k]) for k in (
        'g_f1_pre', 'g_f1_post', 'g_mix_pre', 'conv_b', 'b_rg_a', 'b_rg_x', 'lru_lambda', 'g_out_a',
        'g_out_b', 'g_out_c', 'g_mix_post', 'g_f2_pre', 'g_f2_post', 'g_ple_pre', 'g_ple_post')}
    out.update({k: bf(p[k]) for k in (
        'w_f1_gate', 'w_f1_up', 'w_f1_down', 'w_f2_gate', 'w_f2_up', 'w_f2_down', 'w_out',
        'w_ple_gate', 'w_ple_proj')})
    out['w_in'] = bf(jnp.pad(p['w_in'], ((0, 0), (0, 0), (0, Z_COLS - p['w_in'].shape[-1]))))
    out['conv_w'] = p['conv_w']
    out['w_rg_a'] = bf(_block_diag(p['w_rg_a']))
    out['w_rg_x'] = bf(_block_diag(p['w_rg_x']))
    out['b_mlstm_gate'] = jnp.concatenate([p['b_mlstm_i'], p['b_mlstm_f']], axis=1)[:, :, None]
    return out


def _gate_rows(z3, col0):
    lo = (ZC_GATE - col0) * LANES
    return jnp.swapaxes(z3[:, :, lo:lo + N_GATES], 1, 2)


def _layer(x, pe_all, sw, layer, cache, kv_bufs=None):
    b, t_len, d = x.shape
    if cache is None:
        rows, z, new_k, new_v, z_cls, g_rows = _premix(x.reshape(b * t_len, d), sw, layer,
                                                       kv_bufs, t_len)
        z3 = z.reshape(b, t_len, Z_COLS)
        ya = _attn_prompt(z3, z_cls)
        conv_buf = jnp.zeros((b, CONV_W - 1, D_B), F32)
        h0 = jnp.zeros((b, D_B), F32)
        C0 = jnp.zeros((b, N_HEADS_C, HEAD_DIM, HEAD_DIM), F32)
        n0 = jnp.zeros((b, N_HEADS_C, HEAD_DIM), F32)
        m0 = jnp.zeros((b, N_HEADS_C), F32)
        zc, col0, n_valid = z3, 0, MLSTM_CHUNK
    else:
        kt_all, vt_all, conv_buf, h0, C0, n0, m0 = cache
        rows, z = _premix(x.reshape(b * t_len, d), sw, layer)
        z3 = z.reshape(b, t_len, Z_COLS)
        new_k = z3[:, :, ZC_KA * LANES:ZC_KA * LANES + D_A].reshape(b, t_len, N_HEADS_A, HEAD_DIM)
        new_v = z3[:, :, ZC_VA * LANES:ZC_VA * LANES + D_A].reshape(b, t_len, N_HEADS_A, HEAD_DIM)
        ya = _attn_sample(z3, kt_all, vt_all, layer)
        col0 = ZC_QC
        zc = jnp.pad(z3[:, :, col0 * LANES:], ((0, 0), (0, MLSTM_CHUNK - t_len), (0, 0)))
        g_rows = _gate_rows(zc, col0)
        n_valid = t_len
    yb, new_buf, h_last = _rglru(z3, conv_buf, h0, sw, layer)
    yc, C1, n1, m1 = _mlstm_t(zc, col0, g_rows, n_valid, sw, layer, C0, n0, m0)
    yc = yc[:, :t_len]
    rows = _postmix(rows, ya.reshape(b * t_len, D_A), yb.reshape(b * t_len, D_B),
                    yc.reshape(b * t_len, D_C), pe_all, sw, layer)
    return rows.reshape(b, t_len, d), (new_k, new_v, new_buf, h_last, C1, n1, m1)


def _feature_major(cache):
    depth, b, w_buf, nh, dh = cache.shape
    return jnp.transpose(cache, (0, 1, 3, 4, 2)).reshape(depth, b, nh * dh, w_buf)


def kernel(x_prompt, x_sample, cache_k, cache_v, state_conv, state_h, state_C, state_n, state_m, p_prompt, p_sample, g_f1_pre, w_f1_gate, w_f1_up, w_f1_down, g_f1_post, g_mix_pre, w_in, conv_w, conv_b, w_rg_a, b_rg_a, w_rg_x, b_rg_x, lru_lambda, b_mlstm_i, b_mlstm_f, g_out_a, g_out_b, g_out_c, w_out, g_mix_post, g_f2_pre, w_f2_gate, w_f2_up, w_f2_down, g_f2_post, g_ple_pre, w_ple_gate, w_ple_proj, g_ple_post):
    depth = w_in.shape[0]
    sw = _stack_weights(dict(
        g_f1_pre=g_f1_pre, w_f1_gate=w_f1_gate, w_f1_up=w_f1_up, w_f1_down=w_f1_down,
        g_f1_post=g_f1_post, g_mix_pre=g_mix_pre, w_in=w_in, conv_w=conv_w, conv_b=conv_b,
        w_rg_a=w_rg_a, b_rg_a=b_rg_a, w_rg_x=w_rg_x, b_rg_x=b_rg_x, lru_lambda=lru_lambda,
        b_mlstm_i=b_mlstm_i, b_mlstm_f=b_mlstm_f, g_out_a=g_out_a, g_out_b=g_out_b,
        g_out_c=g_out_c, w_out=w_out, g_mix_post=g_mix_post, g_f2_pre=g_f2_pre,
        w_f2_gate=w_f2_gate, w_f2_up=w_f2_up, w_f2_down=w_f2_down, g_f2_post=g_f2_post,
        g_ple_pre=g_ple_pre, w_ple_gate=w_ple_gate, w_ple_proj=w_ple_proj, g_ple_post=g_ple_post))
    xp, xs = x_prompt, x_sample
    pe_prompt = p_prompt.reshape(depth, -1, p_prompt.shape[-1])
    pe_sample = p_sample.reshape(depth, -1, p_sample.shape[-1])
    kt_all = _feature_major(cache_k)
    vt_all = _feature_major(cache_v)
    b_p, s_p, _ = x_prompt.shape
    keep = min(DILATED_CONFIGS[-1][0], s_p)
    kv_bufs = (jnp.zeros((depth, b_p, D_A, keep), F32), jnp.zeros((depth, b_p, D_A, keep), F32))
    sp = [[] for _ in range(5)]
    ss = [[] for _ in range(7)]
    for i in range(depth):
        xp, st_p = _layer(xp, pe_prompt, sw, i, None, kv_bufs)
        kv_bufs = st_p[:2]
        cache_i = (kt_all, vt_all, state_conv[i], state_h[i], state_C[i], state_n[i], state_m[i])
        xs, st_s = _layer(xs, pe_sample, sw, i, cache_i)
        for j in range(5):
            sp[j].append(st_p[2 + j])
        for j in range(7):
            ss[j].append(st_s[j])
    k_prompt, v_prompt = [
        jnp.transpose(buf.reshape(depth, b_p, N_HEADS_A, HEAD_DIM, keep), (0, 1, 4, 2, 3))
        for buf in kv_bufs]
    conv_prompt, h_prompt, C_prompt, n_prompt, m_prompt = [jnp.stack(a) for a in sp]
    k_sample, v_sample, conv_sample, h_sample, C_sample, n_sample, m_sample = [jnp.stack(a) for a in ss]
    return (xp, xs, k_prompt, v_prompt, k_sample, v_sample, conv_prompt, conv_sample,
            h_prompt, h_sample, C_prompt, C_sample, n_prompt, n_sample, m_prompt, m_sample)
```

```python
import functools

import jax
import jax.numpy as jnp
from jax import lax
from jax.experimental import pallas as pl
from jax.experimental.pallas import tpu as pltpu

F32 = jnp.float32
BF16 = jnp.bfloat16

EPS = 1e-6
NEG = -1e30
LOG2_E = 1.4426950408889634
SQRT_2_OVER_PI = 0.7978845608028654
GELU_CUBIC = 0.044715
HEAD_DIM = 64
N_HEADS_A = 6
D_A = N_HEADS_A * HEAD_DIM
D_B = 384
N_HEADS_C = 4
D_C = N_HEADS_C * HEAD_DIM
N_GATES = 2 * N_HEADS_C
DILATED_CONFIGS = ((128, 1), (512, 4), (2048, 16))
LOCAL_BLK = 128
CONV_W = 4
LRU_C = 8.0
MLSTM_CHUNK = 128
LANES = 128
SUBLANES = 8
Z_COLS = 3072
FF_CHUNK = 256
ROW_TILE = 512
VMEM_LIMIT = 58 * 1024 * 1024

ZC_QA, ZC_KA, ZC_VA, ZC_XB, ZC_GB = 0, 3, 6, 9, 12
ZC_QC, ZC_KC, ZC_VC, ZC_OC, ZC_GATE = 15, 17, 19, 21, 23


def _rms(x, g):
    return x * lax.rsqrt(jnp.mean(x * x, axis=-1, keepdims=True) + EPS) * g


def _dot(a, b):
    return jnp.dot(a, b, preferred_element_type=F32)


def _dot_nt(a, b):
    return lax.dot_general(a, b, (((1,), (1,)), ((), ())), preferred_element_type=F32)


def _softplus(x):
    return jnp.maximum(x, 0.0) + jnp.log1p(jnp.exp(-jnp.abs(x)))


def _log_sigmoid(x):
    return -_softplus(-x)


def _sigmoid_tanh(x):
    return 0.5 * jnp.tanh(0.5 * x) + 0.5


def _gelu_tanh(x):
    return 0.5 * x * (1.0 + jnp.tanh(SQRT_2_OVER_PI * (x + GELU_CUBIC * (x * x * x))))


def _const_spec(shape):
    nd = len(shape)
    return pl.BlockSpec(shape, lambda *_: (0,) * nd, pipeline_mode=pl.Buffered(1))


def _layer_spec(shape, layer, block=None):
    nd = len(shape)
    idx = (0,) * nd if block is None else block
    return pl.BlockSpec((None,) + tuple(shape), lambda *_: (layer,) + idx,
                        pipeline_mode=pl.Buffered(1))


def _params(sem):
    return pltpu.CompilerParams(dimension_semantics=sem, vmem_limit_bytes=VMEM_LIMIT)


def _swiglu_into(acc_ref, h, wg_ref, wu_ref, wd_ref):
    for c in range(wg_ref.shape[1] // FF_CHUNK):
        sl = slice(c * FF_CHUNK, (c + 1) * FF_CHUNK)
        g = _dot(h, wg_ref[:, sl])
        u = _dot(h, wu_ref[:, sl])
        a = (g * jax.nn.sigmoid(g) * u).astype(BF16)
        d = _dot(a, wd_ref[sl, :])
        if c == 0:
            acc_ref[...] = d
        else:
            acc_ref[...] += d


def _premix_kernel(emit_kv, x_ref, gpre_ref, wg_ref, wu_ref, wd_ref, gpost_ref, gmix_ref, win_ref,
                   *rest):
    if emit_kv:
        _, _, xo_ref, z_ref, kt_ref, vt_ref, zc_ref, gt_ref, acc_ref, stage_ref = rest
    else:
        xo_ref, z_ref, acc_ref = rest
    x = x_ref[...]
    _swiglu_into(acc_ref, _rms(x, gpre_ref[...]).astype(BF16), wg_ref, wu_ref, wd_ref)
    x = x + 0.5 * _rms(acc_ref[...], gpost_ref[...])
    xo_ref[...] = x
    h = _rms(x, gmix_ref[...]).astype(BF16)
    for c in range(win_ref.shape[1] // 256):
        sl = slice(c * 256, (c + 1) * 256)
        zc = _dot(h, win_ref[:, sl])
        z_ref[:, sl] = zc
        if not emit_kv:
            continue
        n_cls, rows_per_cls, width = zc_ref.shape
        for half in range(2):
            lg = 2 * c + half
            if lg == ZC_GATE:
                gt_ref[...] = zc[:, half * LANES:(half + 1) * LANES].T[0:N_GATES, :]
            if lg >= width // LANES:
                continue
            cols = zc[:, half * LANES:(half + 1) * LANES]
            lanes = slice(lg * LANES, (lg + 1) * LANES)
            if ZC_KA <= lg < ZC_KA + D_A // LANES:
                kt_ref[(lg - ZC_KA) * LANES:(lg - ZC_KA + 1) * LANES, :] = cols.T
            if ZC_VA <= lg < ZC_VA + D_A // LANES:
                vt_ref[(lg - ZC_VA) * LANES:(lg - ZC_VA + 1) * LANES, :] = cols.T
            stage_ref[lg % 2] = cols
            for r in range(n_cls):
                zc_ref[r, :, lanes] = stage_ref[lg % 2, pl.ds(r, rows_per_cls, stride=n_cls), :]


def _premix(x, sw, layer, kv_bufs=None, seq_len=None):
    m, d = x.shape
    ff = sw['w_f1_gate'].shape[-1]
    tm = min(ROW_TILE, m)
    row = pl.BlockSpec((tm, d), lambda i: (i, 0))
    vec = _layer_spec((1, d), layer)
    in_specs = [row, vec, _layer_spec((d, ff), layer), _layer_spec((d, ff), layer),
                _layer_spec((ff, d), layer), vec, vec, _layer_spec((d, Z_COLS), layer)]
    out_specs = [row, pl.BlockSpec((tm, Z_COLS), lambda i: (i, 0))]
    out_shape = [jax.ShapeDtypeStruct((m, d), F32), jax.ShapeDtypeStruct((m, Z_COLS), F32)]
    args = [x, sw['g_f1_pre'], sw['w_f1_gate'], sw['w_f1_up'], sw['w_f1_down'], sw['g_f1_post'],
            sw['g_mix_pre'], sw['w_in']]
    aliases = {}
    scratch = [pltpu.VMEM((tm, d), F32)]
    if kv_bufs is not None:
        keep = kv_bufs[0].shape[-1]
        tiles_per_seq = seq_len // tm
        first_kept = (seq_len - keep) // tm
        kv_spec = pl.BlockSpec(
            (None, None, D_A, tm),
            lambda i: (layer, i // tiles_per_seq, 0, jnp.maximum(i % tiles_per_seq - first_kept, 0)))
        for buf in kv_bufs:
            aliases[len(args)] = len(out_shape)
            in_specs.append(pl.BlockSpec(memory_space=pl.ANY))
            args.append(buf)
            out_specs.append(kv_spec)
            out_shape.append(jax.ShapeDtypeStruct(buf.shape, F32))
        n_cls = DILATED_CONFIGS[-1][1]
        out_specs.append(pl.BlockSpec((None, n_cls, tm // n_cls, 3 * D_A),
                                      lambda i: (i // tiles_per_seq, 0, i % tiles_per_seq, 0)))
        out_shape.append(jax.ShapeDtypeStruct((m // seq_len, n_cls, seq_len // n_cls, 3 * D_A), F32))
        out_specs.append(pl.BlockSpec((None, N_GATES, tm),
                                      lambda i: (i // tiles_per_seq, 0, i % tiles_per_seq)))
        out_shape.append(jax.ShapeDtypeStruct((m // seq_len, N_GATES, seq_len), F32))
        scratch.append(pltpu.VMEM((2, tm, LANES), F32))
    return pl.pallas_call(
        functools.partial(_premix_kernel, kv_bufs is not None),
        grid=(m // tm,),
        in_specs=in_specs,
        out_specs=out_specs,
        out_shape=out_shape,
        scratch_shapes=scratch,
        input_output_aliases=aliases,
        compiler_params=_params(("arbitrary",)),
        name="premix",
    )(*args)


def _postmix_kernel(x_ref, ya_ref, hb_ref, gb_ref, yc_ref, pe_ref, ga_ref, gob_ref,
                    wa_ref, wb_ref, wc_ref, gmix_ref,
                    gpre_ref, wg_ref, wu_ref, wd_ref, gpost_ref,
                    gple_ref, wpg_ref, wpp_ref, gple_post_ref, o_ref, acc_ref, x_scr):
    ya = _rms(ya_ref[...], ga_ref[...]).astype(BF16)
    yb = _rms(_gelu_tanh(gb_ref[...]) * hb_ref[...], gob_ref[...])
    y = _dot(ya, wa_ref[...])
    y = y + _dot(yb.astype(BF16), wb_ref[...])
    y = y + _dot(yc_ref[...].astype(BF16), wc_ref[...])
    x_scr[...] = x_ref[...] + _rms(y, gmix_ref[...])
    _swiglu_into(acc_ref, _rms(x_scr[...], gpre_ref[...]).astype(BF16), wg_ref, wu_ref, wd_ref)
    x_scr[...] = x_scr[...] + 0.5 * _rms(acc_ref[...], gpost_ref[...])
    h = _rms(x_scr[...], gple_ref[...]).astype(BF16)
    gate = jax.nn.sigmoid(_dot(h, wpg_ref[...]))
    proj = _dot(pe_ref[...].astype(BF16), wpp_ref[...])
    o_ref[...] = x_scr[...] + _rms(gate * proj, gple_post_ref[...])


def _postmix(x, ya, hb, z, yc, pe_all, sw, layer):
    m, d = x.shape
    dp = pe_all.shape[-1]
    ff = sw['w_f2_gate'].shape[-1]
    tm = min(ROW_TILE, m)

    def row(width):
        return pl.BlockSpec((tm, width), lambda i: (i, 0))

    vec = _layer_spec((1, d), layer)
    return pl.pallas_call(
        _postmix_kernel,
        grid=(m // tm,),
        in_specs=[row(d), row(D_A), row(D_B),
                  pl.BlockSpec((tm, D_B), lambda i: (i, ZC_GB * LANES // D_B)), row(D_C),
                  pl.BlockSpec((None, tm, dp), lambda i: (layer, i, 0)),
                  _layer_spec((1, D_A), layer), _layer_spec((1, D_B), layer),
                  _layer_spec((D_A, d), layer, (0, 0)),
                  _layer_spec((D_B, d), layer, (D_A // D_B, 0)),
                  _layer_spec((D_C, d), layer, ((D_A + D_B) // D_C, 0)), vec,
                  vec, _layer_spec((d, ff), layer), _layer_spec((d, ff), layer),
                  _layer_spec((ff, d), layer), vec,
                  vec, _layer_spec((d, d), layer), _layer_spec((dp, d), layer), vec],
        out_specs=row(d),
        out_shape=jax.ShapeDtypeStruct((m, d), F32),
        scratch_shapes=[pltpu.VMEM((tm, d), F32), pltpu.VMEM((tm, d), F32)],
        compiler_params=_params(("parallel",)),
        name="postmix",
    )(x, ya, hb, z, yc, pe_all, sw['g_out_a'], sw['g_out_b'],
      sw['w_out'], sw['w_out'], sw['w_out'], sw['g_mix_post'],
      sw['g_f2_pre'], sw['w_f2_gate'], sw['w_f2_up'], sw['w_f2_down'], sw['g_f2_post'],
      sw['g_ple_pre'], sw['w_ple_gate'], sw['w_ple_proj'], sw['g_ple_post'])


ATT_QT = 2048
ATT_UNROLL = 16


ATT_PITCH = LOCAL_BLK + 8


def _attn_prompt_kernel(q_ref, k_ref, v_ref, qc_ref, kc_ref, vc_ref, bias_ref, o_ref,
                        num_scr, m_scr, den_scr, cls_scr):
    t0 = pl.program_id(2) * ATT_QT
    blk = LOCAL_BLK
    lo_head = lax.broadcasted_iota(jnp.int32, (blk, LANES), 1) < HEAD_DIM
    scale = HEAD_DIM ** -0.5 * LOG2_E
    n_cfg = len(DILATED_CONFIGS)
    dil_c = DILATED_CONFIGS[n_cfg - 1][1]

    for ci, (_, dil) in enumerate(DILATED_CONFIGS):
        span = blk * dil
        class_major = span == ATT_QT

        def rows_of(j, dil=dil, span=span, class_major=class_major):
            if class_major:
                tile = pl.program_id(2)
                first = tile == 0
                cur = pl.multiple_of(tile * blk, 8)
                prev = jnp.where(first, cur, cur - blk)
                return ((j,), (j, pl.ds(cur, blk)), (j, pl.ds(pl.multiple_of(prev, 8), blk)), first)
            qs = j % dil + span * (j // dil)
            ks = t0 + qs
            first = ks < span
            ps = jnp.where(first, ks, ks - span)
            return ((pl.ds(qs, blk, stride=dil),), (pl.ds(ks, blk, stride=dil),),
                    (pl.ds(ps, blk, stride=dil),), first)

        q_src, k_src, v_src = (qc_ref, kc_ref, vc_ref) if class_major else (q_ref, k_ref, v_ref)

        def scores(j, rows_of=rows_of, q_src=q_src, k_src=k_src):
            q_rows, cur, prev, first = rows_of(j)
            q2 = q_src[q_rows] * scale
            kc = k_src[cur].astype(BF16)
            kp = k_src[prev].astype(BF16)
            q_both = jnp.concatenate([jnp.where(lo_head, q2, 0.0), jnp.where(lo_head, 0.0, q2)], axis=0)
            raw = _dot_nt(q_both.astype(BF16), jnp.concatenate([kp, kc], axis=0))
            return j, q_rows, cur, prev, first.astype(jnp.int32), raw

        def softmax(first, raw):
            s = raw + bias_ref[first]
            m = jnp.max(s, axis=-1, keepdims=True)
            p = jnp.exp2(s - m)
            return p.astype(BF16), m, jnp.sum(p, axis=-1, keepdims=True)

        def weighted_values(j, q_rows, cur, prev, probs, ci=ci, class_major=class_major, v_src=v_src):
            vc = v_src[cur].astype(BF16)
            vp = v_src[prev].astype(BF16)
            p_both, m, den = probs
            num = _dot(p_both, jnp.concatenate([vp, vc], axis=0))
            num = jnp.where(lo_head, num[:blk], num[blk:])
            m = jnp.where(lo_head, m[:blk], m[blk:])
            den = jnp.where(lo_head, den[:blk], den[blk:])
            if class_major:
                dst = pl.ds(pl.multiple_of(j * ATT_PITCH, 8), blk)
                cls_scr[0, dst, :] = num
                cls_scr[1, dst, :] = m
                cls_scr[2, dst, :] = den
            else:
                num_scr[(ci,) + q_rows] = num
                m_scr[(ci,) + q_rows] = m
                den_scr[(ci,) + q_rows] = den

        def body(jj, carry, scores=scores, softmax=softmax, weighted_values=weighted_values):
            staged = [scores(jj * ATT_UNROLL + u) for u in range(ATT_UNROLL)]
            probs = [softmax(st[4], st[5]) for st in staged]
            for st, pr in zip(staged, probs):
                weighted_values(st[0], st[1], st[2], st[3], pr)
            return carry

        lax.fori_loop(0, ATT_QT // blk // ATT_UNROLL, body, 0)

    rows = 256

    def combine(i, carry):
        sl = pl.ds(pl.multiple_of(i * rows, rows), rows)

        def class_rows(a):
            base = i * (rows // dil_c)
            return jnp.concatenate(
                [cls_scr[a, pl.ds(base + g, dil_c, stride=ATT_PITCH), :] for g in range(rows // dil_c)],
                axis=0)

        ms = [m_scr[0, sl, :], m_scr[1, sl, :], class_rows(1)]
        nums = [num_scr[0, sl, :], num_scr[1, sl, :], class_rows(0)]
        dens = [den_scr[0, sl, :], den_scr[1, sl, :], class_rows(2)]
        mx = jnp.maximum(jnp.maximum(ms[0], ms[1]), ms[2])
        ws = [jnp.exp2(mc - mx) for mc in ms]
        num = nums[0] * ws[0] + nums[1] * ws[1] + nums[2] * ws[2]
        den = dens[0] * ws[0] + dens[1] * ws[1] + dens[2] * ws[2]
        o_ref[sl, :] = num / den
        return carry

    lax.fori_loop(0, ATT_QT // rows, combine, 0)


def _attn_bias():
    blk = LOCAL_BLK
    qi = jnp.arange(2 * blk)[:, None] % blk
    kj = jnp.arange(2 * blk)[None, :]
    prev_ok = (kj < blk) & (kj >= qi)
    cur_ok = (kj >= blk) & (kj - blk <= qi)
    normal = jnp.where(prev_ok | cur_ok, 0.0, NEG)
    first = jnp.where(cur_ok, 0.0, NEG)
    return jnp.stack([normal, first]).astype(F32)


def _attn_prompt(z3, zc):
    b, s, _ = z3.shape
    n_pairs = D_A // LANES
    dil_c = DILATED_CONFIGS[-1][1]
    assert dil_c * LOCAL_BLK == ATT_QT and len(DILATED_CONFIGS) == 3
    q_spec = pl.BlockSpec((None, ATT_QT, LANES), lambda i, p, t: (i, t, ZC_QA + p))
    k_spec = pl.BlockSpec((None, s, LANES), lambda i, p, t: (i, 0, ZC_KA + p))
    v_spec = pl.BlockSpec((None, s, LANES), lambda i, p, t: (i, 0, ZC_VA + p))
    qc_spec = pl.BlockSpec((None, dil_c, LOCAL_BLK, LANES), lambda i, p, t: (i, 0, t, ZC_QA + p))
    kc_spec = pl.BlockSpec((None, dil_c, s // dil_c, LANES), lambda i, p, t: (i, 0, 0, ZC_KA + p))
    vc_spec = pl.BlockSpec((None, dil_c, s // dil_c, LANES), lambda i, p, t: (i, 0, 0, ZC_VA + p))
    return pl.pallas_call(
        _attn_prompt_kernel,
        grid=(b, n_pairs, s // ATT_QT),
        in_specs=[q_spec, k_spec, v_spec, qc_spec, kc_spec, vc_spec,
                  _const_spec((2, 2 * LOCAL_BLK, 2 * LOCAL_BLK))],
        out_specs=pl.BlockSpec((None, ATT_QT, LANES), lambda i, p, t: (i, t, p)),
        out_shape=jax.ShapeDtypeStruct((b, s, D_A), F32),
        scratch_shapes=[pltpu.VMEM((2, ATT_QT, LANES), F32)] * 3
        + [pltpu.VMEM((3, dil_c * ATT_PITCH, LANES), F32)],
        compiler_params=_params(("parallel", "parallel", "arbitrary")),
        name="attn_prompt",
    )(z3, z3, z3, zc, zc, zc, _attn_bias())


def _attn_sample_kernel(q_ref, kn_ref, vn_ref, kt_ref, vt_ref, o_ref, q_scr, kn_scr, vn_scr):
    t_new = q_ref.shape[0]
    w_buf = kt_ref.shape[1]
    q_scr[...] = jnp.zeros(q_scr.shape, F32)
    kn_scr[...] = jnp.zeros(kn_scr.shape, F32)
    vn_scr[...] = jnp.zeros(vn_scr.shape, F32)
    q_scr[0:t_new, :] = q_ref[...] * (HEAD_DIM ** -0.5)
    kn_scr[0:t_new, :] = kn_ref[...]
    vn_scr[0:t_new, :] = vn_ref[...]

    rows = 2 * SUBLANES
    tq = lax.broadcasted_iota(jnp.int32, (rows, w_buf), 0) & (SUBLANES - 1)
    dist = w_buf + tq - lax.broadcasted_iota(jnp.int32, (rows, w_buf), 1)
    tqn = lax.broadcasted_iota(jnp.int32, (rows, LANES), 0) & (SUBLANES - 1)
    tn = lax.broadcasted_iota(jnp.int32, (rows, LANES), 1)
    dist_new = tqn - tn
    cache_ok, new_ok = [], []
    for window, dil in DILATED_CONFIGS:
        ok = dist <= window
        nk = jnp.logical_and(dist_new >= 0, tn < t_new)
        nk = jnp.logical_and(nk, dist_new <= window)
        if dil > 1:
            ok = jnp.logical_and(ok, (dist & (dil - 1)) == 0)
            nk = jnp.logical_and(nk, (dist_new & (dil - 1)) == 0)
        cache_ok.append(ok)
        new_ok.append(nk)
    lo8 = lax.broadcasted_iota(jnp.int32, (SUBLANES, LANES), 1) < HEAD_DIM

    for p in range(D_A // LANES):
        sl = slice(p * LANES, (p + 1) * LANES)
        q2 = q_scr[:, sl]
        qm = jnp.concatenate([jnp.where(lo8, q2, 0.0), jnp.where(lo8, 0.0, q2)], axis=0).astype(BF16)
        kb = kt_ref[sl, :].astype(BF16)
        vb = vt_ref[sl, :].astype(BF16)
        s = _dot(qm, kb)
        s_new = _dot_nt(qm, kn_scr[:, sl].astype(BF16))
        parts = []
        for c in range(len(DILATED_CONFIGS)):
            sc = jnp.where(cache_ok[c], s, NEG)
            sn = jnp.where(new_ok[c], s_new, NEG)
            m = jnp.maximum(jnp.max(sc, axis=1, keepdims=True), jnp.max(sn, axis=1, keepdims=True))
            pc = jnp.exp(sc - m)
            pn = jnp.exp(sn - m)
            den = jnp.sum(pc, axis=1, keepdims=True) + jnp.sum(pn, axis=1, keepdims=True)
            parts.append((pc, pn, m, den))
        mx = jnp.maximum(jnp.maximum(parts[0][2], parts[1][2]), parts[2][2])
        ws = [jnp.exp(pt[2] - mx) for pt in parts]
        den_all = parts[0][3] * ws[0] + parts[1][3] * ws[1] + parts[2][3] * ws[2]
        coefs = [w / den_all for w in ws]
        p_all = parts[0][0] * coefs[0] + parts[1][0] * coefs[1] + parts[2][0] * coefs[2]
        pn_all = parts[0][1] * coefs[0] + parts[1][1] * coefs[1] + parts[2][1] * coefs[2]
        o16 = _dot_nt(p_all.astype(BF16), vb) + _dot(pn_all.astype(BF16), vn_scr[:, sl].astype(BF16))
        o8 = jnp.where(lo8, o16[0:SUBLANES], o16[SUBLANES:rows])
        o_ref[:, sl] = o8[0:t_new]


def _attn_sample(zs3, kt_all, vt_all, layer):
    b, t_new, _ = zs3.shape
    w_buf = kt_all.shape[-1]

    def new_spec(col):
        return pl.BlockSpec((None, t_new, D_A), lambda i: (i, 0, col))

    cache_spec = pl.BlockSpec((None, None, D_A, w_buf), lambda i: (layer, i, 0, 0))
    return pl.pallas_call(
        _attn_sample_kernel,
        grid=(b,),
        in_specs=[new_spec(0), new_spec(1), new_spec(2), cache_spec, cache_spec],
        out_specs=pl.BlockSpec((None, t_new, D_A), lambda i: (i, 0, 0)),
        out_shape=jax.ShapeDtypeStruct((b, t_new, D_A), F32),
        scratch_shapes=[pltpu.VMEM((SUBLANES, D_A), F32), pltpu.VMEM((LANES, D_A), F32),
                        pltpu.VMEM((LANES, D_A), F32)],
        compiler_params=_params(("parallel",)),
        name="attn_sample",
    )(zs3, zs3, zs3, kt_all, vt_all)


RG_GROUP = SUBLANES
RG_CHUNK = 256
RG_PAD = 8


def _rg_pitch(tcp):
    tiles = tcp // 8 + 1
    return 8 * (tiles if tiles % 2 else tiles + 1)


def _rglru_kernel(xb_ref, conv_ref, h0_ref, cw_ref, cb_ref, wa_ref, ba_ref, wx_ref, bx_ref,
                  lam_ref, y_ref, convnew_ref, ht_ref, xp_scr, hist_scr, h_scr, a_scr, b_scr):
    n_seq, tc, _ = xb_ref.shape
    tcp = xp_scr.shape[0] - RG_PAD
    pitch = a_scr.shape[1] // n_seq
    n_hist = CONV_W - 1
    hist = slice(RG_PAD - n_hist, RG_PAD)
    n_groups = D_B // LANES

    @pl.when(pl.program_id(1) == 0)
    def _():
        hist_scr[:, hist, :] = conv_ref[...]
        h_scr[...] = h0_ref[...]

    if tc % 8:
        xp_scr[RG_PAD:, :] = jnp.zeros((tcp, D_B), F32)
    decay = _softplus(-lam_ref[...])

    def gates(g, carry):
        xp_scr[hist, :] = hist_scr[g, hist, :]
        xp_scr[RG_PAD:RG_PAD + tc, :] = xb_ref[g]
        hist_scr[g, hist, :] = xp_scr[RG_PAD + tc - n_hist:RG_PAD + tc, :]
        xc = cb_ref[...]
        for j in range(CONV_W):
            lo = RG_PAD - n_hist + j
            xc = xc + xp_scr[lo:lo + tcp, :] * cw_ref[pl.ds(j, 1), :]
        xcb = xc.astype(BF16)
        r = _sigmoid_tanh(_dot(xcb, wa_ref[...]) + ba_ref[...])
        gi = _sigmoid_tanh(_dot(xcb, wx_ref[...]) + bx_ref[...])
        a = jnp.exp(-LRU_C * r * decay)
        om = 1.0 - a * a
        bb = jnp.where(om > 0.0, om * lax.rsqrt(om), 0.0) * (gi * xc)
        dst = pl.ds(pl.multiple_of(g * pitch, 8), tcp)
        for lg in range(n_groups):
            a_scr[lg, dst, :] = a[:, lg * LANES:(lg + 1) * LANES]
            b_scr[lg, dst, :] = bb[:, lg * LANES:(lg + 1) * LANES]
        return carry

    lax.fori_loop(0, n_seq, gates, 0)

    def step(t, hs):
        rows = pl.ds(t, n_seq, stride=pitch)
        out = []
        for lg in range(n_groups):
            h = a_scr[lg, rows, :] * hs[lg] + b_scr[lg, rows, :]
            b_scr[lg, rows, :] = h
            out.append(h)
        return tuple(out)

    hs = tuple(h_scr[:, lg * LANES:(lg + 1) * LANES] for lg in range(n_groups))
    hs = lax.fori_loop(0, tc, step, hs, unroll=min(8, tc))
    for lg in range(n_groups):
        h_scr[:, lg * LANES:(lg + 1) * LANES] = hs[lg]

    def finish(g, carry):
        src = pl.ds(pl.multiple_of(g * pitch, 8), tcp)
        hseq = jnp.concatenate([b_scr[lg, src, :] for lg in range(n_groups)], axis=1)
        y_ref[g] = hseq[0:tc]
        return carry

    lax.fori_loop(0, n_seq, finish, 0)
    convnew_ref[...] = hist_scr[:, hist, :]
    ht_ref[...] = h_scr[...]


def _rglru(z3, conv_buf, h0, sw, layer):
    b, t_len, _ = z3.shape
    tc = min(RG_CHUNK, t_len)
    tcp = -(-tc // 8) * 8
    pitch = _rg_pitch(tcp)
    n_hist = CONV_W - 1
    ng = b // RG_GROUP
    vec = _layer_spec((1, D_B), layer)
    mat = _layer_spec((D_B, D_B), layer)
    seq = pl.BlockSpec((RG_GROUP, tc, D_B), lambda i, c: (i, c, 0))
    hist = pl.BlockSpec((RG_GROUP, n_hist, D_B), lambda i, c: (i, 0, 0))
    state = pl.BlockSpec((None, RG_GROUP, D_B), lambda i, c: (i, 0, 0))
    y, conv_new, h_last = pl.pallas_call(
        _rglru_kernel,
        grid=(ng, t_len // tc),
        in_specs=[pl.BlockSpec((RG_GROUP, tc, D_B), lambda i, c: (i, c, ZC_XB // 3)),
                  hist, state,
                  _layer_spec((CONV_W, D_B), layer), vec, mat, vec, mat, vec, vec],
        out_specs=[seq, hist, state],
        out_shape=[jax.ShapeDtypeStruct((b, t_len, D_B), F32),
                   jax.ShapeDtypeStruct((b, n_hist, D_B), F32),
                   jax.ShapeDtypeStruct((ng, RG_GROUP, D_B), F32)],
        scratch_shapes=[pltpu.VMEM((RG_PAD + tcp, D_B), F32),
                        pltpu.VMEM((RG_GROUP, RG_PAD, D_B), F32),
                        pltpu.VMEM((RG_GROUP, D_B), F32),
                        pltpu.VMEM((D_B // LANES, RG_GROUP * pitch, LANES), F32),
                        pltpu.VMEM((D_B // LANES, RG_GROUP * pitch, LANES), F32)],
        compiler_params=_params(("parallel", "arbitrary")),
        name="rglru",
    )(z3, conv_buf, h0.reshape(ng, RG_GROUP, D_B), sw['conv_w'], sw['conv_b'], sw['w_rg_a'],
      sw['b_rg_a'], sw['w_rg_x'], sw['b_rg_x'], sw['lru_lambda'])
    return y, conv_new, h_last.reshape(b, D_B)


ML_GROUP = 8


def _split3(x, axis):
    hi = x.astype(BF16).astype(F32)
    r1 = x - hi
    mid = r1.astype(BF16).astype(F32)
    lo = (r1 - mid).astype(BF16).astype(F32)
    return jnp.concatenate([hi, mid, lo], axis=axis).astype(BF16)


def _sum3(x, axis):
    n = x.shape[axis] // 3
    if axis == 0:
        return x[0:n] + x[n:2 * n] + x[2 * n:3 * n]
    return x[:, 0:n] + x[:, n:2 * n] + x[:, 2 * n:3 * n]


def _mlstm_consts():
    L = MLSTM_CHUNK
    t = jnp.arange(L)
    tri_t = (t[:, None] <= t[None, :]).astype(BF16)
    src = jnp.arange(3 * LANES) % LANES
    dst = jnp.arange(N_GATES * LANES) // LANES
    sel = (src[:, None] == dst[None, :]).astype(BF16)
    return tri_t, sel


def _mlstm_t_kernel(n_valid, q0_ref, q1_ref, k0_ref, k1_ref, v0_ref, v1_ref, o0_ref, o1_ref,
                    gr_ref, bias_ref, gain_ref, trit_ref, sel_ref,
                    c0_ref, n0_ref, m0_ref, y_ref, c_out_ref, n_out_ref, m_out_ref,
                    ct_scr, n_scr, m_scr):
    ck = pl.program_id(1)
    n_seq, _, L = gr_ref.shape

    lane = lax.broadcasted_iota(jnp.int32, (L, LANES), 1)
    row = lax.broadcasted_iota(jnp.int32, (L, LANES), 0)
    lo_head = lane < HEAD_DIM
    src_le_dst = row <= lane
    row_lo = row < HEAD_DIM
    blockdiag = ((row ^ lane) & HEAD_DIM) == 0
    sub = lax.broadcasted_iota(jnp.int32, (SUBLANES, L), 0)
    tl = lax.broadcasted_iota(jnp.int32, (SUBLANES, L), 1)
    lo_row = tl[0:1, :] < HEAD_DIM
    is_head_row = sub < N_HEADS_C
    refs = ((q0_ref, k0_ref, v0_ref, o0_ref), (q1_ref, k1_ref, v1_ref, o1_ref))
    units = [(sq, p) for sq in range(n_seq) for p in range(2)]
    pad_rows = jnp.zeros((L - SUBLANES, L), F32)

    @pl.when(ck == 0)
    def _():
        zero = jnp.zeros((HEAD_DIM, HEAD_DIM), F32)
        n_scr[...] = jnp.zeros(n_scr.shape, F32)
        for sq, p in units:
            top = jnp.concatenate([c0_ref[sq, 2 * p], zero], axis=1)
            bot = jnp.concatenate([zero, c0_ref[sq, 2 * p + 1]], axis=1)
            ct_scr[sq, p] = jnp.transpose(jnp.concatenate([top, bot], axis=0))
            n_scr[sq, pl.ds(p, 1), :] = jnp.concatenate(
                [n0_ref[sq, pl.ds(2 * p, 1), :], n0_ref[sq, pl.ds(2 * p + 1, 1), :]], axis=1)
        m_scr[...] = m0_ref[...]

    def cumsum_stage(sq):
        xr = gr_ref[sq] + bias_ref[...]
        gt = jnp.where(is_head_row, xr, _log_sigmoid(xr))
        if n_valid < L:
            gt = jnp.where(tl < n_valid, gt, jnp.where(is_head_row, NEG, 0.0))
        return gt, _sum3(_dot(_split3(gt, 0), trit_ref[...]), 0)

    def gate_stage(sq, gt, csum):
        b = pltpu.roll(csum, N_HEADS_C, axis=0)
        e = gt - b
        cm = e
        shift = 1
        while shift < L:
            cm = jnp.maximum(cm, jnp.where(tl >= shift, pltpu.roll(cm, shift, axis=1), NEG))
            shift *= 2
        m_prev = m_scr[sq]
        g = jnp.maximum(m_prev, cm)
        mt = g + b
        wi = jnp.exp(m_prev - g)
        emt = jnp.exp(-mt)
        m_new = jnp.broadcast_to(mt[:, L - 1:L], (SUBLANES, L))
        b_last = jnp.broadcast_to(b[:, L - 1:L], (SUBLANES, L))
        ws = jnp.exp(e + b_last - m_new)
        wc = jnp.exp(b_last + m_prev - m_new)
        both = jnp.where(is_head_row, e, pltpu.roll(ws, N_HEADS_C, axis=0))
        return g, wi, emt, wc, m_new, _split3(jnp.transpose(jnp.concatenate([both, pad_rows], axis=0)), 1)

    def score_stage(sq, p):
        q_ref, k_ref, v_ref, _ = refs[p]
        q2 = q_ref[sq]
        k2 = k_ref[sq] * (HEAD_DIM ** -0.5)
        q2b = q2.astype(BF16)
        k2b = k2.astype(BF16)
        vt_b = jnp.transpose(v_ref[sq]).astype(BF16)
        ct = ct_scr[sq, p]
        n_pair = n_scr[sq, pl.ds(p, 1), :]
        n2 = jnp.where(jnp.logical_or(jnp.logical_and(sub == 0, tl < HEAD_DIM),
                                      jnp.logical_and(sub == 1, tl >= HEAD_DIM)), n_pair, 0.0)
        qct = _dot_nt(ct.astype(BF16), q2b)
        qn = _dot_nt(n2.astype(BF16), q2b)
        st = [_dot_nt(k2b, jnp.where(lo_head if h == 0 else jnp.logical_not(lo_head), q2, 0.0)
                      .astype(BF16)) for h in range(2)]
        return k2, vt_b, ct, n_pair, qct, qn, st

    def weight_stage(p, gates, staged):
        g, wi, emt, _, _, cols = gates
        qn, st = staged[5], staged[6]
        out = []
        for h in range(2):
            gh = 2 * p + h
            e_c = cols[:, gh * LANES:(gh + 1) * LANES]
            at = jnp.exp(jnp.where(src_le_dst, e_c - g[gh:gh + 1, :], NEG)) * st[h]
            den = jnp.sum(at, axis=0, keepdims=True) + wi[gh:gh + 1, :] * qn[h:h + 1, :]
            inv = 1.0 / jnp.maximum(jnp.abs(den), emt[gh:gh + 1, :])
            out.append((at.astype(BF16), inv))
        return out

    def value_stage(sq, p, gates, staged, weights):
        o_ref = refs[p][3]
        _, wi, _, wc, m_new, cols = gates
        k2, vt_b, ct, n_pair, qct, _, _ = staged
        (at0, inv0), (at1, inv1) = weights
        g0, g1 = 2 * p, 2 * p + 1
        num_t = jnp.where(row_lo, _dot(vt_b, at0), _dot(vt_b, at1))
        wi2 = jnp.where(row_lo, wi[g0:g0 + 1, :], wi[g1:g1 + 1, :])
        inv2 = jnp.where(row_lo, inv0, inv1)
        hout = jnp.transpose((num_t + wi2 * qct) * inv2)
        ws2 = jnp.where(lo_head, cols[:, (N_HEADS_C + g0) * LANES:(N_HEADS_C + g0 + 1) * LANES],
                        cols[:, (N_HEADS_C + g1) * LANES:(N_HEADS_C + g1 + 1) * LANES])
        kw = k2 * ws2
        keep_t = jnp.where(row_lo, wc[g0:g0 + 1, :], wc[g1:g1 + 1, :])
        ct_scr[sq, p] = jnp.where(blockdiag, keep_t * ct + _dot(vt_b, kw.astype(BF16)), 0.0)
        n_scr[sq, pl.ds(p, 1), :] = jnp.where(lo_row, wc[g0:g0 + 1, :], wc[g1:g1 + 1, :]) * n_pair \
            + jnp.sum(kw, axis=0, keepdims=True)
        hc = jax.nn.sigmoid(o_ref[sq]) * hout
        hsq = hc * hc
        ms0 = jnp.sum(jnp.where(lo_head, hsq, 0.0), axis=1, keepdims=True)
        ms1 = jnp.sum(jnp.where(lo_head, 0.0, hsq), axis=1, keepdims=True)
        ms = jnp.where(lo_head, ms0, ms1) * (1.0 / HEAD_DIM)
        y_ref[sq, :, p * LANES:(p + 1) * LANES] = \
            hc * lax.rsqrt(ms + EPS) * gain_ref[:, p * LANES:(p + 1) * LANES]

    sums = [cumsum_stage(sq) for sq in range(n_seq)]
    staged = [score_stage(sq, p) for sq, p in units]
    gates = [gate_stage(sq, *sums[sq]) for sq in range(n_seq)]
    gates = [gt[:5] + (_dot(gt[5], sel_ref[...]),) for gt in gates]
    weights = [weight_stage(p, gates[sq], st) for (sq, p), st in zip(units, staged)]
    for (sq, p), st, wt in zip(units, staged, weights):
        value_stage(sq, p, gates[sq], st, wt)
    for sq in range(n_seq):
        m_scr[sq] = jnp.where(is_head_row, gates[sq][4], 0.0)

    @pl.when(ck == pl.num_programs(1) - 1)
    def _():
        for sq, p in units:
            c_pair = jnp.transpose(ct_scr[sq, p])
            c_out_ref[sq, 2 * p] = c_pair[0:HEAD_DIM, 0:HEAD_DIM]
            c_out_ref[sq, 2 * p + 1] = c_pair[HEAD_DIM:, HEAD_DIM:]
        n_out_ref[...] = n_scr[:, 0:2, :]
        m_out_ref[...] = m_scr[...]


def _mlstm_t(z3, col0, g_rows, n_valid, sw, layer, C0, n0, m0):
    b, t_len, _ = z3.shape
    L = MLSTM_CHUNK
    nc = t_len // L
    G = ML_GROUP
    tri_t, sel = _mlstm_consts()
    m_rows = jnp.broadcast_to(jnp.pad(m0, ((0, 0), (0, SUBLANES - N_HEADS_C)))[:, :, None],
                              (b, SUBLANES, LANES))

    def col(cb):
        return pl.BlockSpec((G, L, LANES), lambda i, c: (i, c, cb - col0))

    state_c = pl.BlockSpec((G, N_HEADS_C, HEAD_DIM, HEAD_DIM), lambda i, c: (i, 0, 0, 0))
    state_m = pl.BlockSpec((G, SUBLANES, LANES), lambda i, c: (i, 0, 0))
    y, c1, n1, m1 = pl.pallas_call(
        functools.partial(_mlstm_t_kernel, n_valid),
        grid=(b // G, nc),
        in_specs=[col(ZC_QC), col(ZC_QC + 1), col(ZC_KC), col(ZC_KC + 1), col(ZC_VC), col(ZC_VC + 1),
                  col(ZC_OC), col(ZC_OC + 1),
                  pl.BlockSpec((G, SUBLANES, L), lambda i, c: (i, 0, c)),
                  _layer_spec((SUBLANES, 1), layer), _layer_spec((1, D_C), layer),
                  _const_spec((L, L)), _const_spec((3 * LANES, N_GATES * LANES)),
                  state_c, pl.BlockSpec((G, N_HEADS_C, HEAD_DIM), lambda i, c: (i, 0, 0)), state_m],
        out_specs=[pl.BlockSpec((G, L, D_C), lambda i, c: (i, c, 0)), state_c,
                   pl.BlockSpec((G, 2, LANES), lambda i, c: (i, 0, 0)), state_m],
        out_shape=[jax.ShapeDtypeStruct((b, t_len, D_C), F32),
                   jax.ShapeDtypeStruct((b, N_HEADS_C, HEAD_DIM, HEAD_DIM), F32),
                   jax.ShapeDtypeStruct((b, 2, LANES), F32),
                   jax.ShapeDtypeStruct((b, SUBLANES, LANES), F32)],
        scratch_shapes=[pltpu.VMEM((G, 2, LANES, LANES), F32), pltpu.VMEM((G, SUBLANES, LANES), F32),
                        pltpu.VMEM((G, SUBLANES, LANES), F32)],
        compiler_params=_params(("parallel", "arbitrary")),
        name="mlstm",
    )(*([z3] * 8), g_rows, sw['b_mlstm_gate'], sw['g_out_c'], tri_t, sel, C0, n0, m_rows)
    return y, c1, n1.reshape(b, N_HEADS_C, HEAD_DIM), m1[:, 0:N_HEADS_C, 0]


def _block_diag(w):
    depth, nb, bs, _ = w.shape
    eye = jnp.eye(nb, dtype=w.dtype)
    return jnp.einsum('lncd,nm->lncmd', w, eye).reshape(depth, nb * bs, nb * bs)


def _stack_weights(p):
    vec = lambda v: v.reshape(v.shape[0], 1, -1)
    bf = lambda w: w.astype(BF16)
    out = {k: vec(p[k]) for k in (
        'g_f1_pre', 'g_f1_post', 'g_mix_pre', 'conv_b', 'b_rg_a', 'b_rg_x', 'lru_lambda', 'g_out_a',
        'g_out_b', 'g_out_c', 'g_mix_post', 'g_f2_pre', 'g_f2_post', 'g_ple_pre', 'g_ple_post')}
    out.update({k: bf(p[k]) for k in (
        'w_f1_gate', 'w_f1_up', 'w_f1_down', 'w_f2_gate', 'w_f2_up', 'w_f2_down', 'w_out',
        'w_ple_gate', 'w_ple_proj')})
    out['w_in'] = bf(jnp.pad(p['w_in'], ((0, 0), (0, 0), (0, Z_COLS - p['w_in'].shape[-1]))))
    out['conv_w'] = p['conv_w']
    out['w_rg_a'] = bf(_block_diag(p['w_rg_a']))
    out['w_rg_x'] = bf(_block_diag(p['w_rg_x']))
    out['b_mlstm_gate'] = jnp.concatenate([p['b_mlstm_i'], p['b_mlstm_f']], axis=1)[:, :, None]
    return out


def _gate_rows(z3, col0):
    lo = (ZC_GATE - col0) * LANES
    return jnp.swapaxes(z3[:, :, lo:lo + N_GATES], 1, 2)


def _layer(x, pe_all, sw, layer, cache, kv_bufs=None):
    b, t_len, d = x.shape
    if cache is None:
        rows, z, new_k, new_v, z_cls, g_rows = _premix(x.reshape(b * t_len, d), sw, layer,
                                                       kv_bufs, t_len)
        z3 = z.reshape(b, t_len, Z_COLS)
        ya = _attn_prompt(z3, z_cls)
        conv_buf = jnp.zeros((b, CONV_W - 1, D_B), F32)
        h0 = jnp.zeros((b, D_B), F32)
        C0 = jnp.zeros((b, N_HEADS_C, HEAD_DIM, HEAD_DIM), F32)
        n0 = jnp.zeros((b, N_HEADS_C, HEAD_DIM), F32)
        m0 = jnp.zeros((b, N_HEADS_C), F32)
        zc, col0, n_valid = z3, 0, MLSTM_CHUNK
    else:
        kt_all, vt_all, conv_buf, h0, C0, n0, m0 = cache
        rows, z = _premix(x.reshape(b * t_len, d), sw, layer)
        z3 = z.reshape(b, t_len, Z_COLS)
        new_k = z3[:, :, ZC_KA * LANES:ZC_KA * LANES + D_A].reshape(b, t_len, N_HEADS_A, HEAD_DIM)
        new_v = z3[:, :, ZC_VA * LANES:ZC_VA * LANES + D_A].reshape(b, t_len, N_HEADS_A, HEAD_DIM)
        ya = _attn_sample(z3, kt_all, vt_all, layer)
        col0 = ZC_QC
        zc = jnp.pad(z3[:, :, col0 * LANES:], ((0, 0), (0, MLSTM_CHUNK - t_len), (0, 0)))
        g_rows = _gate_rows(zc, col0)
        n_valid = t_len
    yb, new_buf, h_last = _rglru(z3, conv_buf, h0, sw, layer)
    yc, C1, n1, m1 = _mlstm_t(zc, col0, g_rows, n_valid, sw, layer, C0, n0, m0)
    yc = yc[:, :t_len]
    rows = _postmix(rows, ya.reshape(b * t_len, D_A), yb.reshape(b * t_len, D_B), z,
                    yc.reshape(b * t_len, D_C), pe_all, sw, layer)
    return rows.reshape(b, t_len, d), (new_k, new_v, new_buf, h_last, C1, n1, m1)


def _feature_major(cache):
    depth, b, w_buf, nh, dh = cache.shape
    return jnp.transpose(cache, (0, 1, 3, 4, 2)).reshape(depth, b, nh * dh, w_buf)


def kernel(x_prompt, x_sample, cache_k, cache_v, state_conv, state_h, state_C, state_n, state_m, p_prompt, p_sample, g_f1_pre, w_f1_gate, w_f1_up, w_f1_down, g_f1_post, g_mix_pre, w_in, conv_w, conv_b, w_rg_a, b_rg_a, w_rg_x, b_rg_x, lru_lambda, b_mlstm_i, b_mlstm_f, g_out_a, g_out_b, g_out_c, w_out, g_mix_post, g_f2_pre, w_f2_gate, w_f2_up, w_f2_down, g_f2_post, g_ple_pre, w_ple_gate, w_ple_proj, g_ple_post):
    depth = w_in.shape[0]
    sw = _stack_weights(dict(
        g_f1_pre=g_f1_pre, w_f1_gate=w_f1_gate, w_f1_up=w_f1_up, w_f1_down=w_f1_down,
        g_f1_post=g_f1_post, g_mix_pre=g_mix_pre, w_in=w_in, conv_w=conv_w, conv_b=conv_b,
        w_rg_a=w_rg_a, b_rg_a=b_rg_a, w_rg_x=w_rg_x, b_rg_x=b_rg_x, lru_lambda=lru_lambda,
        b_mlstm_i=b_mlstm_i, b_mlstm_f=b_mlstm_f, g_out_a=g_out_a, g_out_b=g_out_b,
        g_out_c=g_out_c, w_out=w_out, g_mix_post=g_mix_post, g_f2_pre=g_f2_pre,
        w_f2_gate=w_f2_gate, w_f2_up=w_f2_up, w_f2_down=w_f2_down, g_f2_post=g_f2_post,
        g_ple_pre=g_ple_pre, w_ple_gate=w_ple_gate, w_ple_proj=w_ple_proj, g_ple_post=g_ple_post))
    xp, xs = x_prompt, x_sample
    pe_prompt = p_prompt.reshape(depth, -1, p_prompt.shape[-1])
    pe_sample = p_sample.reshape(depth, -1, p_sample.shape[-1])
    kt_all = _feature_major(cache_k)
    vt_all = _feature_major(cache_v)
    b_p, s_p, _ = x_prompt.shape
    keep = min(DILATED_CONFIGS[-1][0], s_p)
    kv_bufs = (jnp.zeros((depth, b_p, D_A, keep), F32), jnp.zeros((depth, b_p, D_A, keep), F32))
    sp = [[] for _ in range(5)]
    ss = [[] for _ in range(7)]
    for i in range(depth):
        xp, st_p = _layer(xp, pe_prompt, sw, i, None, kv_bufs)
        kv_bufs = st_p[:2]
        cache_i = (kt_all, vt_all, state_conv[i], state_h[i], state_C[i], state_n[i], state_m[i])
        xs, st_s = _layer(xs, pe_sample, sw, i, cache_i)
        for j in range(5):
            sp[j].append(st_p[2 + j])
        for j in range(7):
            ss[j].append(st_s[j])
    k_prompt, v_prompt = [
        jnp.transpose(buf.reshape(depth, b_p, N_HEADS_A, HEAD_DIM, keep), (0, 1, 4, 2, 3))
        for buf in kv_bufs]
    conv_prompt, h_prompt, C_prompt, n_prompt, m_prompt = [jnp.stack(a) for a in sp]
    k_sample, v_sample, conv_sample, h_sample, C_sample, n_sample, m_sample = [jnp.stack(a) for a in ss]
    return (xp, xs, k_prompt, v_prompt, k_sample, v_sample, conv_prompt, conv_sample,
            h_prompt, h_sample, C_prompt, C_sample, n_prompt, n_sample, m_prompt, m_sample)
```

```python
import functools

import jax
import jax.numpy as jnp
from jax import lax
from jax.experimental import pallas as pl
from jax.experimental.pallas import tpu as pltpu

F32 = jnp.float32
BF16 = jnp.bfloat16

EPS = 1e-6
NEG = -1e30
LOG2_E = 1.4426950408889634
SQRT_2_OVER_PI = 0.7978845608028654
GELU_CUBIC = 0.044715
HEAD_DIM = 64
N_HEADS_A = 6
D_A = N_HEADS_A * HEAD_DIM
D_B = 384
N_HEADS_C = 4
D_C = N_HEADS_C * HEAD_DIM
N_GATES = 2 * N_HEADS_C
DILATED_CONFIGS = ((128, 1), (512, 4), (2048, 16))
LOCAL_BLK = 128
CONV_W = 4
LRU_C = 8.0
MLSTM_CHUNK = 128
LANES = 128
SUBLANES = 8
Z_COLS = 3072
FF_CHUNK = 256
ROW_TILE = 512
VMEM_LIMIT = 58 * 1024 * 1024

ZC_QA, ZC_KA, ZC_VA, ZC_XB, ZC_GB = 0, 3, 6, 9, 12
ZC_QC, ZC_KC, ZC_VC, ZC_OC, ZC_GATE = 15, 17, 19, 21, 23


def _rms(x, g):
    return x * lax.rsqrt(jnp.mean(x * x, axis=-1, keepdims=True) + EPS) * g


def _dot(a, b):
    return jnp.dot(a, b, preferred_element_type=F32)


def _dot_nt(a, b):
    return lax.dot_general(a, b, (((1,), (1,)), ((), ())), preferred_element_type=F32)


def _softplus(x):
    return jnp.maximum(x, 0.0) + jnp.log1p(jnp.exp(-jnp.abs(x)))


def _log_sigmoid(x):
    return -_softplus(-x)


def _sigmoid_tanh(x):
    return 0.5 * jnp.tanh(0.5 * x) + 0.5


def _gelu_tanh(x):
    return 0.5 * x * (1.0 + jnp.tanh(SQRT_2_OVER_PI * (x + GELU_CUBIC * (x * x * x))))


def _const_spec(shape):
    nd = len(shape)
    return pl.BlockSpec(shape, lambda *_: (0,) * nd, pipeline_mode=pl.Buffered(1))


def _layer_spec(shape, layer, block=None):
    nd = len(shape)
    idx = (0,) * nd if block is None else block
    return pl.BlockSpec((None,) + tuple(shape), lambda *_: (layer,) + idx,
                        pipeline_mode=pl.Buffered(1))


def _params(sem):
    return pltpu.CompilerParams(dimension_semantics=sem, vmem_limit_bytes=VMEM_LIMIT)


def _swiglu_into(acc_ref, h, wg_ref, wu_ref, wd_ref):
    for c in range(wg_ref.shape[1] // FF_CHUNK):
        sl = slice(c * FF_CHUNK, (c + 1) * FF_CHUNK)
        g = _dot(h, wg_ref[:, sl])
        u = _dot(h, wu_ref[:, sl])
        a = (g * jax.nn.sigmoid(g) * u).astype(BF16)
        d = _dot(a, wd_ref[sl, :])
        if c == 0:
            acc_ref[...] = d
        else:
            acc_ref[...] += d


def _premix_kernel(emit_kv, x_ref, gpre_ref, wg_ref, wu_ref, wd_ref, gpost_ref, gmix_ref, win_ref,
                   *rest):
    if emit_kv:
        _, _, xo_ref, z_ref, kt_ref, vt_ref, zc_ref, gt_ref, acc_ref, stage_ref = rest
    else:
        xo_ref, z_ref, acc_ref = rest
    x = x_ref[...]
    _swiglu_into(acc_ref, _rms(x, gpre_ref[...]).astype(BF16), wg_ref, wu_ref, wd_ref)
    x = x + 0.5 * _rms(acc_ref[...], gpost_ref[...])
    xo_ref[...] = x
    h = _rms(x, gmix_ref[...]).astype(BF16)
    for c in range(win_ref.shape[1] // 256):
        sl = slice(c * 256, (c + 1) * 256)
        zc = _dot(h, win_ref[:, sl])
        z_ref[:, sl] = zc
        if not emit_kv:
            continue
        n_cls, rows_per_cls, width = zc_ref.shape
        for half in range(2):
            lg = 2 * c + half
            if lg == ZC_GATE:
                gt_ref[...] = zc[:, half * LANES:(half + 1) * LANES].T[0:N_GATES, :]
            if lg >= width // LANES:
                continue
            cols = zc[:, half * LANES:(half + 1) * LANES]
            lanes = slice(lg * LANES, (lg + 1) * LANES)
            if ZC_KA <= lg < ZC_KA + D_A // LANES:
                kt_ref[(lg - ZC_KA) * LANES:(lg - ZC_KA + 1) * LANES, :] = cols.T
            if ZC_VA <= lg < ZC_VA + D_A // LANES:
                vt_ref[(lg - ZC_VA) * LANES:(lg - ZC_VA + 1) * LANES, :] = cols.T
            stage_ref[lg % 2] = cols
            for r in range(n_cls):
                zc_ref[r, :, lanes] = stage_ref[lg % 2, pl.ds(r, rows_per_cls, stride=n_cls), :]


def _premix(x, sw, layer, kv_bufs=None, seq_len=None):
    m, d = x.shape
    ff = sw['w_f1_gate'].shape[-1]
    tm = min(ROW_TILE, m)
    row = pl.BlockSpec((tm, d), lambda i: (i, 0))
    vec = _layer_spec((1, d), layer)
    in_specs = [row, vec, _layer_spec((d, ff), layer), _layer_spec((d, ff), layer),
                _layer_spec((ff, d), layer), vec, vec, _layer_spec((d, Z_COLS), layer)]
    out_specs = [row, pl.BlockSpec((tm, Z_COLS), lambda i: (i, 0))]
    out_shape = [jax.ShapeDtypeStruct((m, d), F32), jax.ShapeDtypeStruct((m, Z_COLS), F32)]
    args = [x, sw['g_f1_pre'], sw['w_f1_gate'], sw['w_f1_up'], sw['w_f1_down'], sw['g_f1_post'],
            sw['g_mix_pre'], sw['w_in']]
    aliases = {}
    scratch = [pltpu.VMEM((tm, d), F32)]
    if kv_bufs is not None:
        keep = kv_bufs[0].shape[-1]
        tiles_per_seq = seq_len // tm
        first_kept = (seq_len - keep) // tm
        kv_spec = pl.BlockSpec(
            (None, None, D_A, tm),
            lambda i: (layer, i // tiles_per_seq, 0, jnp.maximum(i % tiles_per_seq - first_kept, 0)))
        for buf in kv_bufs:
            aliases[len(args)] = len(out_shape)
            in_specs.append(pl.BlockSpec(memory_space=pl.ANY))
            args.append(buf)
            out_specs.append(kv_spec)
            out_shape.append(jax.ShapeDtypeStruct(buf.shape, F32))
        n_cls = DILATED_CONFIGS[-1][1]
        out_specs.append(pl.BlockSpec((None, n_cls, tm // n_cls, 3 * D_A),
                                      lambda i: (i // tiles_per_seq, 0, i % tiles_per_seq, 0)))
        out_shape.append(jax.ShapeDtypeStruct((m // seq_len, n_cls, seq_len // n_cls, 3 * D_A), F32))
        out_specs.append(pl.BlockSpec((None, N_GATES, tm),
                                      lambda i: (i // tiles_per_seq, 0, i % tiles_per_seq)))
        out_shape.append(jax.ShapeDtypeStruct((m // seq_len, N_GATES, seq_len), F32))
        scratch.append(pltpu.VMEM((2, tm, LANES), F32))
    return pl.pallas_call(
        functools.partial(_premix_kernel, kv_bufs is not None),
        grid=(m // tm,),
        in_specs=in_specs,
        out_specs=out_specs,
        out_shape=out_shape,
        scratch_shapes=scratch,
        input_output_aliases=aliases,
        compiler_params=_params(("arbitrary",)),
        name="premix",
    )(*args)


def _postmix_kernel(x_ref, ya_ref, hb_ref, gb_ref, hc_ref, oc0_ref, oc1_ref, pe_ref, ga_ref, gob_ref,
                    goc_ref, wa_ref, wb_ref, wc_ref, gmix_ref,
                    gpre_ref, wg_ref, wu_ref, wd_ref, gpost_ref,
                    gple_ref, wpg_ref, wpp_ref, gple_post_ref, o_ref, acc_ref, x_scr):
    ya = _rms(ya_ref[...], ga_ref[...]).astype(BF16)
    yb = _rms(_gelu_tanh(gb_ref[...]) * hb_ref[...], gob_ref[...])
    lo_head = lax.broadcasted_iota(jnp.int32, (x_ref.shape[0], LANES), 1) < HEAD_DIM
    yc = []
    for p, oc_ref in enumerate((oc0_ref, oc1_ref)):
        lanes = slice(p * LANES, (p + 1) * LANES)
        hc = jax.nn.sigmoid(oc_ref[...]) * hc_ref[:, lanes]
        hsq = hc * hc
        ms0 = jnp.sum(jnp.where(lo_head, hsq, 0.0), axis=1, keepdims=True)
        ms1 = jnp.sum(jnp.where(lo_head, 0.0, hsq), axis=1, keepdims=True)
        ms = jnp.where(lo_head, ms0, ms1) * (1.0 / HEAD_DIM)
        yc.append(hc * lax.rsqrt(ms + EPS) * goc_ref[:, lanes])
    y = _dot(ya, wa_ref[...])
    y = y + _dot(yb.astype(BF16), wb_ref[...])
    y = y + _dot(jnp.concatenate(yc, axis=1).astype(BF16), wc_ref[...])
    x_scr[...] = x_ref[...] + _rms(y, gmix_ref[...])
    _swiglu_into(acc_ref, _rms(x_scr[...], gpre_ref[...]).astype(BF16), wg_ref, wu_ref, wd_ref)
    x_scr[...] = x_scr[...] + 0.5 * _rms(acc_ref[...], gpost_ref[...])
    h = _rms(x_scr[...], gple_ref[...]).astype(BF16)
    gate = jax.nn.sigmoid(_dot(h, wpg_ref[...]))
    proj = _dot(pe_ref[...].astype(BF16), wpp_ref[...])
    o_ref[...] = x_scr[...] + _rms(gate * proj, gple_post_ref[...])


def _postmix(x, ya, hb, z, yc, pe_all, sw, layer):
    m, d = x.shape
    dp = pe_all.shape[-1]
    ff = sw['w_f2_gate'].shape[-1]
    tm = min(ROW_TILE, m)

    def row(width):
        return pl.BlockSpec((tm, width), lambda i: (i, 0))

    vec = _layer_spec((1, d), layer)
    return pl.pallas_call(
        _postmix_kernel,
        grid=(m // tm,),
        in_specs=[row(d), row(D_A), row(D_B),
                  pl.BlockSpec((tm, D_B), lambda i: (i, ZC_GB * LANES // D_B)), row(D_C),
                  pl.BlockSpec((tm, LANES), lambda i: (i, ZC_OC)),
                  pl.BlockSpec((tm, LANES), lambda i: (i, ZC_OC + 1)),
                  pl.BlockSpec((None, tm, dp), lambda i: (layer, i, 0)),
                  _layer_spec((1, D_A), layer), _layer_spec((1, D_B), layer),
                  _layer_spec((1, D_C), layer),
                  _layer_spec((D_A, d), layer, (0, 0)),
                  _layer_spec((D_B, d), layer, (D_A // D_B, 0)),
                  _layer_spec((D_C, d), layer, ((D_A + D_B) // D_C, 0)), vec,
                  vec, _layer_spec((d, ff), layer), _layer_spec((d, ff), layer),
                  _layer_spec((ff, d), layer), vec,
                  vec, _layer_spec((d, d), layer), _layer_spec((dp, d), layer), vec],
        out_specs=row(d),
        out_shape=jax.ShapeDtypeStruct((m, d), F32),
        scratch_shapes=[pltpu.VMEM((tm, d), F32), pltpu.VMEM((tm, d), F32)],
        compiler_params=_params(("parallel",)),
        name="postmix",
    )(x, ya, hb, z, yc, z, z, pe_all, sw['g_out_a'], sw['g_out_b'], sw['g_out_c'],
      sw['w_out'], sw['w_out'], sw['w_out'], sw['g_mix_post'],
      sw['g_f2_pre'], sw['w_f2_gate'], sw['w_f2_up'], sw['w_f2_down'], sw['g_f2_post'],
      sw['g_ple_pre'], sw['w_ple_gate'], sw['w_ple_proj'], sw['g_ple_post'])


ATT_QT = 2048
ATT_UNROLL = 16


ATT_PITCH = LOCAL_BLK + 8


def _attn_prompt_kernel(q_ref, k_ref, v_ref, qc_ref, kc_ref, vc_ref, bias_ref, o_ref,
                        num_scr, m_scr, den_scr, cls_scr):
    t0 = pl.program_id(2) * ATT_QT
    blk = LOCAL_BLK
    lo_head = lax.broadcasted_iota(jnp.int32, (blk, LANES), 1) < HEAD_DIM
    scale = HEAD_DIM ** -0.5 * LOG2_E
    n_cfg = len(DILATED_CONFIGS)
    dil_c = DILATED_CONFIGS[n_cfg - 1][1]

    for ci, (_, dil) in enumerate(DILATED_CONFIGS):
        span = blk * dil
        class_major = span == ATT_QT

        def rows_of(j, dil=dil, span=span, class_major=class_major):
            if class_major:
                tile = pl.program_id(2)
                first = tile == 0
                cur = pl.multiple_of(tile * blk, 8)
                prev = jnp.where(first, cur, cur - blk)
                return ((j,), (j, pl.ds(cur, blk)), (j, pl.ds(pl.multiple_of(prev, 8), blk)), first)
            qs = j % dil + span * (j // dil)
            ks = t0 + qs
            first = ks < span
            ps = jnp.where(first, ks, ks - span)
            return ((pl.ds(qs, blk, stride=dil),), (pl.ds(ks, blk, stride=dil),),
                    (pl.ds(ps, blk, stride=dil),), first)

        q_src, k_src, v_src = (qc_ref, kc_ref, vc_ref) if class_major else (q_ref, k_ref, v_ref)

        def scores(j, rows_of=rows_of, q_src=q_src, k_src=k_src):
            q_rows, cur, prev, first = rows_of(j)
            q2 = q_src[q_rows] * scale
            kc = k_src[cur].astype(BF16)
            kp = k_src[prev].astype(BF16)
            q_both = jnp.concatenate([jnp.where(lo_head, q2, 0.0), jnp.where(lo_head, 0.0, q2)], axis=0)
            raw = _dot_nt(q_both.astype(BF16), jnp.concatenate([kp, kc], axis=0))
            return j, q_rows, cur, prev, first.astype(jnp.int32), raw

        def softmax(first, raw):
            s = raw + bias_ref[first]
            m = jnp.max(s, axis=-1, keepdims=True)
            p = jnp.exp2(s - m)
            return p.astype(BF16), m, jnp.sum(p, axis=-1, keepdims=True)

        def weighted_values(j, q_rows, cur, prev, probs, ci=ci, class_major=class_major, v_src=v_src):
            vc = v_src[cur].astype(BF16)
            vp = v_src[prev].astype(BF16)
            p_both, m, den = probs
            num = _dot(p_both, jnp.concatenate([vp, vc], axis=0))
            num = jnp.where(lo_head, num[:blk], num[blk:])
            m = jnp.where(lo_head, m[:blk], m[blk:])
            den = jnp.where(lo_head, den[:blk], den[blk:])
            if class_major:
                dst = pl.ds(pl.multiple_of(j * ATT_PITCH, 8), blk)
                cls_scr[0, dst, :] = num
                cls_scr[1, dst, :] = m
                cls_scr[2, dst, :] = den
            else:
                num_scr[(ci,) + q_rows] = num
                m_scr[(ci,) + q_rows] = m
                den_scr[(ci,) + q_rows] = den

        def body(jj, carry, scores=scores, softmax=softmax, weighted_values=weighted_values):
            staged = [scores(jj * ATT_UNROLL + u) for u in range(ATT_UNROLL)]
            probs = [softmax(st[4], st[5]) for st in staged]
            for st, pr in zip(staged, probs):
                weighted_values(st[0], st[1], st[2], st[3], pr)
            return carry

        lax.fori_loop(0, ATT_QT // blk // ATT_UNROLL, body, 0)

    rows = 256

    def combine(i, carry):
        sl = pl.ds(pl.multiple_of(i * rows, rows), rows)

        def class_rows(a):
            base = i * (rows // dil_c)
            return jnp.concatenate(
                [cls_scr[a, pl.ds(base + g, dil_c, stride=ATT_PITCH), :] for g in range(rows // dil_c)],
                axis=0)

        ms = [m_scr[0, sl, :], m_scr[1, sl, :], class_rows(1)]
        nums = [num_scr[0, sl, :], num_scr[1, sl, :], class_rows(0)]
        dens = [den_scr[0, sl, :], den_scr[1, sl, :], class_rows(2)]
        mx = jnp.maximum(jnp.maximum(ms[0], ms[1]), ms[2])
        ws = [jnp.exp2(mc - mx) for mc in ms]
        num = nums[0] * ws[0] + nums[1] * ws[1] + nums[2] * ws[2]
        den = dens[0] * ws[0] + dens[1] * ws[1] + dens[2] * ws[2]
        o_ref[sl, :] = num / den
        return carry

    lax.fori_loop(0, ATT_QT // rows, combine, 0)


def _attn_bias():
    blk = LOCAL_BLK
    qi = jnp.arange(2 * blk)[:, None] % blk
    kj = jnp.arange(2 * blk)[None, :]
    prev_ok = (kj < blk) & (kj >= qi)
    cur_ok = (kj >= blk) & (kj - blk <= qi)
    normal = jnp.where(prev_ok | cur_ok, 0.0, NEG)
    first = jnp.where(cur_ok, 0.0, NEG)
    return jnp.stack([normal, first]).astype(F32)


def _attn_prompt(z3, zc):
    b, s, _ = z3.shape
    n_pairs = D_A // LANES
    dil_c = DILATED_CONFIGS[-1][1]
    assert dil_c * LOCAL_BLK == ATT_QT and len(DILATED_CONFIGS) == 3
    q_spec = pl.BlockSpec((None, ATT_QT, LANES), lambda i, p, t: (i, t, ZC_QA + p))
    k_spec = pl.BlockSpec((None, s, LANES), lambda i, p, t: (i, 0, ZC_KA + p))
    v_spec = pl.BlockSpec((None, s, LANES), lambda i, p, t: (i, 0, ZC_VA + p))
    qc_spec = pl.BlockSpec((None, dil_c, LOCAL_BLK, LANES), lambda i, p, t: (i, 0, t, ZC_QA + p))
    kc_spec = pl.BlockSpec((None, dil_c, s // dil_c, LANES), lambda i, p, t: (i, 0, 0, ZC_KA + p))
    vc_spec = pl.BlockSpec((None, dil_c, s // dil_c, LANES), lambda i, p, t: (i, 0, 0, ZC_VA + p))
    return pl.pallas_call(
        _attn_prompt_kernel,
        grid=(b, n_pairs, s // ATT_QT),
        in_specs=[q_spec, k_spec, v_spec, qc_spec, kc_spec, vc_spec,
                  _const_spec((2, 2 * LOCAL_BLK, 2 * LOCAL_BLK))],
        out_specs=pl.BlockSpec((None, ATT_QT, LANES), lambda i, p, t: (i, t, p)),
        out_shape=jax.ShapeDtypeStruct((b, s, D_A), F32),
        scratch_shapes=[pltpu.VMEM((2, ATT_QT, LANES), F32)] * 3
        + [pltpu.VMEM((3, dil_c * ATT_PITCH, LANES), F32)],
        compiler_params=_params(("parallel", "parallel", "arbitrary")),
        name="attn_prompt",
    )(z3, z3, z3, zc, zc, zc, _attn_bias())


def _attn_sample_kernel(q_ref, kn_ref, vn_ref, kt_ref, vt_ref, o_ref, q_scr, kn_scr, vn_scr):
    t_new = q_ref.shape[0]
    w_buf = kt_ref.shape[1]
    q_scr[...] = jnp.zeros(q_scr.shape, F32)
    kn_scr[...] = jnp.zeros(kn_scr.shape, F32)
    vn_scr[...] = jnp.zeros(vn_scr.shape, F32)
    q_scr[0:t_new, :] = q_ref[...] * (HEAD_DIM ** -0.5)
    kn_scr[0:t_new, :] = kn_ref[...]
    vn_scr[0:t_new, :] = vn_ref[...]

    rows = 2 * SUBLANES
    tq = lax.broadcasted_iota(jnp.int32, (rows, w_buf), 0) & (SUBLANES - 1)
    dist = w_buf + tq - lax.broadcasted_iota(jnp.int32, (rows, w_buf), 1)
    tqn = lax.broadcasted_iota(jnp.int32, (rows, LANES), 0) & (SUBLANES - 1)
    tn = lax.broadcasted_iota(jnp.int32, (rows, LANES), 1)
    dist_new = tqn - tn
    cache_ok, new_ok = [], []
    for window, dil in DILATED_CONFIGS:
        ok = dist <= window
        nk = jnp.logical_and(dist_new >= 0, tn < t_new)
        nk = jnp.logical_and(nk, dist_new <= window)
        if dil > 1:
            ok = jnp.logical_and(ok, (dist & (dil - 1)) == 0)
            nk = jnp.logical_and(nk, (dist_new & (dil - 1)) == 0)
        cache_ok.append(ok)
        new_ok.append(nk)
    lo8 = lax.broadcasted_iota(jnp.int32, (SUBLANES, LANES), 1) < HEAD_DIM

    for p in range(D_A // LANES):
        sl = slice(p * LANES, (p + 1) * LANES)
        q2 = q_scr[:, sl]
        qm = jnp.concatenate([jnp.where(lo8, q2, 0.0), jnp.where(lo8, 0.0, q2)], axis=0).astype(BF16)
        kb = kt_ref[sl, :].astype(BF16)
        vb = vt_ref[sl, :].astype(BF16)
        s = _dot(qm, kb)
        s_new = _dot_nt(qm, kn_scr[:, sl].astype(BF16))
        parts = []
        for c in range(len(DILATED_CONFIGS)):
            sc = jnp.where(cache_ok[c], s, NEG)
            sn = jnp.where(new_ok[c], s_new, NEG)
            m = jnp.maximum(jnp.max(sc, axis=1, keepdims=True), jnp.max(sn, axis=1, keepdims=True))
            pc = jnp.exp(sc - m)
            pn = jnp.exp(sn - m)
            den = jnp.sum(pc, axis=1, keepdims=True) + jnp.sum(pn, axis=1, keepdims=True)
            parts.append((pc, pn, m, den))
        mx = jnp.maximum(jnp.maximum(parts[0][2], parts[1][2]), parts[2][2])
        ws = [jnp.exp(pt[2] - mx) for pt in parts]
        den_all = parts[0][3] * ws[0] + parts[1][3] * ws[1] + parts[2][3] * ws[2]
        coefs = [w / den_all for w in ws]
        p_all = parts[0][0] * coefs[0] + parts[1][0] * coefs[1] + parts[2][0] * coefs[2]
        pn_all = parts[0][1] * coefs[0] + parts[1][1] * coefs[1] + parts[2][1] * coefs[2]
        o16 = _dot_nt(p_all.astype(BF16), vb) + _dot(pn_all.astype(BF16), vn_scr[:, sl].astype(BF16))
        o8 = jnp.where(lo8, o16[0:SUBLANES], o16[SUBLANES:rows])
        o_ref[:, sl] = o8[0:t_new]


def _attn_sample(zs3, kt_all, vt_all, layer):
    b, t_new, _ = zs3.shape
    w_buf = kt_all.shape[-1]

    def new_spec(col):
        return pl.BlockSpec((None, t_new, D_A), lambda i: (i, 0, col))

    cache_spec = pl.BlockSpec((None, None, D_A, w_buf), lambda i: (layer, i, 0, 0))
    return pl.pallas_call(
        _attn_sample_kernel,
        grid=(b,),
        in_specs=[new_spec(0), new_spec(1), new_spec(2), cache_spec, cache_spec],
        out_specs=pl.BlockSpec((None, t_new, D_A), lambda i: (i, 0, 0)),
        out_shape=jax.ShapeDtypeStruct((b, t_new, D_A), F32),
        scratch_shapes=[pltpu.VMEM((SUBLANES, D_A), F32), pltpu.VMEM((LANES, D_A), F32),
                        pltpu.VMEM((LANES, D_A), F32)],
        compiler_params=_params(("parallel",)),
        name="attn_sample",
    )(zs3, zs3, zs3, kt_all, vt_all)


RG_GROUP = SUBLANES
RG_CHUNK = 256
RG_PAD = 8


def _rg_pitch(tcp):
    tiles = tcp // 8 + 1
    return 8 * (tiles if tiles % 2 else tiles + 1)


def _rglru_kernel(xb_ref, conv_ref, h0_ref, cw_ref, cb_ref, wa_ref, ba_ref, wx_ref, bx_ref,
                  lam_ref, y_ref, convnew_ref, ht_ref, xp_scr, hist_scr, h_scr, a_scr, b_scr):
    n_seq, tc, _ = xb_ref.shape
    tcp = xp_scr.shape[0] - RG_PAD
    pitch = a_scr.shape[1] // n_seq
    n_hist = CONV_W - 1
    hist = slice(RG_PAD - n_hist, RG_PAD)
    n_groups = D_B // LANES

    @pl.when(pl.program_id(1) == 0)
    def _():
        hist_scr[:, hist, :] = conv_ref[...]
        h_scr[...] = h0_ref[...]

    if tc % 8:
        xp_scr[RG_PAD:, :] = jnp.zeros((tcp, D_B), F32)
    decay = _softplus(-lam_ref[...])

    def gates(g, carry):
        xp_scr[hist, :] = hist_scr[g, hist, :]
        xp_scr[RG_PAD:RG_PAD + tc, :] = xb_ref[g]
        hist_scr[g, hist, :] = xp_scr[RG_PAD + tc - n_hist:RG_PAD + tc, :]
        xc = cb_ref[...]
        for j in range(CONV_W):
            lo = RG_PAD - n_hist + j
            xc = xc + xp_scr[lo:lo + tcp, :] * cw_ref[pl.ds(j, 1), :]
        xcb = xc.astype(BF16)
        r = _sigmoid_tanh(_dot(xcb, wa_ref[...]) + ba_ref[...])
        gi = _sigmoid_tanh(_dot(xcb, wx_ref[...]) + bx_ref[...])
        a = jnp.exp(-LRU_C * r * decay)
        om = 1.0 - a * a
        bb = jnp.where(om > 0.0, om * lax.rsqrt(om), 0.0) * (gi * xc)
        dst = pl.ds(pl.multiple_of(g * pitch, 8), tcp)
        for lg in range(n_groups):
            a_scr[lg, dst, :] = a[:, lg * LANES:(lg + 1) * LANES]
            b_scr[lg, dst, :] = bb[:, lg * LANES:(lg + 1) * LANES]
        return carry

    lax.fori_loop(0, n_seq, gates, 0)

    def step(t, hs):
        rows = pl.ds(t, n_seq, stride=pitch)
        out = []
        for lg in range(n_groups):
            h = a_scr[lg, rows, :] * hs[lg] + b_scr[lg, rows, :]
            b_scr[lg, rows, :] = h
            out.append(h)
        return tuple(out)

    hs = tuple(h_scr[:, lg * LANES:(lg + 1) * LANES] for lg in range(n_groups))
    hs = lax.fori_loop(0, tc, step, hs, unroll=min(8, tc))
    for lg in range(n_groups):
        h_scr[:, lg * LANES:(lg + 1) * LANES] = hs[lg]

    def finish(g, carry):
        src = pl.ds(pl.multiple_of(g * pitch, 8), tcp)
        hseq = jnp.concatenate([b_scr[lg, src, :] for lg in range(n_groups)], axis=1)
        y_ref[g] = hseq[0:tc]
        return carry

    lax.fori_loop(0, n_seq, finish, 0)
    convnew_ref[...] = hist_scr[:, hist, :]
    ht_ref[...] = h_scr[...]


def _rglru(z3, conv_buf, h0, sw, layer):
    b, t_len, _ = z3.shape
    tc = min(RG_CHUNK, t_len)
    tcp = -(-tc // 8) * 8
    pitch = _rg_pitch(tcp)
    n_hist = CONV_W - 1
    ng = b // RG_GROUP
    vec = _layer_spec((1, D_B), layer)
    mat = _layer_spec((D_B, D_B), layer)
    seq = pl.BlockSpec((RG_GROUP, tc, D_B), lambda i, c: (i, c, 0))
    hist = pl.BlockSpec((RG_GROUP, n_hist, D_B), lambda i, c: (i, 0, 0))
    state = pl.BlockSpec((None, RG_GROUP, D_B), lambda i, c: (i, 0, 0))
    y, conv_new, h_last = pl.pallas_call(
        _rglru_kernel,
        grid=(ng, t_len // tc),
        in_specs=[pl.BlockSpec((RG_GROUP, tc, D_B), lambda i, c: (i, c, ZC_XB // 3)),
                  hist, state,
                  _layer_spec((CONV_W, D_B), layer), vec, mat, vec, mat, vec, vec],
        out_specs=[seq, hist, state],
        out_shape=[jax.ShapeDtypeStruct((b, t_len, D_B), F32),
                   jax.ShapeDtypeStruct((b, n_hist, D_B), F32),
                   jax.ShapeDtypeStruct((ng, RG_GROUP, D_B), F32)],
        scratch_shapes=[pltpu.VMEM((RG_PAD + tcp, D_B), F32),
                        pltpu.VMEM((RG_GROUP, RG_PAD, D_B), F32),
                        pltpu.VMEM((RG_GROUP, D_B), F32),
                        pltpu.VMEM((D_B // LANES, RG_GROUP * pitch, LANES), F32),
                        pltpu.VMEM((D_B // LANES, RG_GROUP * pitch, LANES), F32)],
        compiler_params=_params(("parallel", "arbitrary")),
        name="rglru",
    )(z3, conv_buf, h0.reshape(ng, RG_GROUP, D_B), sw['conv_w'], sw['conv_b'], sw['w_rg_a'],
      sw['b_rg_a'], sw['w_rg_x'], sw['b_rg_x'], sw['lru_lambda'])
    return y, conv_new, h_last.reshape(b, D_B)


ML_GROUP = 8


def _split3(x, axis):
    hi = x.astype(BF16).astype(F32)
    r1 = x - hi
    mid = r1.astype(BF16).astype(F32)
    lo = (r1 - mid).astype(BF16).astype(F32)
    return jnp.concatenate([hi, mid, lo], axis=axis).astype(BF16)


def _sum3(x, axis):
    n = x.shape[axis] // 3
    if axis == 0:
        return x[0:n] + x[n:2 * n] + x[2 * n:3 * n]
    return x[:, 0:n] + x[:, n:2 * n] + x[:, 2 * n:3 * n]


def _mlstm_consts():
    L = MLSTM_CHUNK
    t = jnp.arange(L)
    tri_t = (t[:, None] <= t[None, :]).astype(BF16)
    src = jnp.arange(3 * LANES) % LANES
    dst = jnp.arange(N_GATES * LANES) // LANES
    sel = (src[:, None] == dst[None, :]).astype(BF16)
    return tri_t, sel


def _mlstm_t_kernel(n_valid, q0_ref, q1_ref, k0_ref, k1_ref, v0_ref, v1_ref,
                    gr_ref, bias_ref, trit_ref, sel_ref,
                    c0_ref, n0_ref, m0_ref, y_ref, c_out_ref, n_out_ref, m_out_ref,
                    ct_scr, n_scr, m_scr):
    ck = pl.program_id(1)
    n_seq, _, L = gr_ref.shape

    lane = lax.broadcasted_iota(jnp.int32, (L, LANES), 1)
    row = lax.broadcasted_iota(jnp.int32, (L, LANES), 0)
    lo_head = lane < HEAD_DIM
    src_le_dst = row <= lane
    row_lo = row < HEAD_DIM
    blockdiag = ((row ^ lane) & HEAD_DIM) == 0
    sub = lax.broadcasted_iota(jnp.int32, (SUBLANES, L), 0)
    tl = lax.broadcasted_iota(jnp.int32, (SUBLANES, L), 1)
    lo_row = tl[0:1, :] < HEAD_DIM
    is_head_row = sub < N_HEADS_C
    refs = ((q0_ref, k0_ref, v0_ref), (q1_ref, k1_ref, v1_ref))
    units = [(sq, p) for sq in range(n_seq) for p in range(2)]
    pad_rows = jnp.zeros((L - SUBLANES, L), F32)

    @pl.when(ck == 0)
    def _():
        zero = jnp.zeros((HEAD_DIM, HEAD_DIM), F32)
        n_scr[...] = jnp.zeros(n_scr.shape, F32)
        for sq, p in units:
            top = jnp.concatenate([c0_ref[sq, 2 * p], zero], axis=1)
            bot = jnp.concatenate([zero, c0_ref[sq, 2 * p + 1]], axis=1)
            ct_scr[sq, p] = jnp.transpose(jnp.concatenate([top, bot], axis=0))
            n_scr[sq, pl.ds(p, 1), :] = jnp.concatenate(
                [n0_ref[sq, pl.ds(2 * p, 1), :], n0_ref[sq, pl.ds(2 * p + 1, 1), :]], axis=1)
        m_scr[...] = m0_ref[...]

    def cumsum_stage(sq):
        xr = gr_ref[sq] + bias_ref[...]
        gt = jnp.where(is_head_row, xr, _log_sigmoid(xr))
        if n_valid < L:
            gt = jnp.where(tl < n_valid, gt, jnp.where(is_head_row, NEG, 0.0))
        return gt, _sum3(_dot(_split3(gt, 0), trit_ref[...]), 0)

    def gate_stage(sq, gt, csum):
        b = pltpu.roll(csum, N_HEADS_C, axis=0)
        e = gt - b
        cm = e
        shift = 1
        while shift < L:
            cm = jnp.maximum(cm, jnp.where(tl >= shift, pltpu.roll(cm, shift, axis=1), NEG))
            shift *= 2
        m_prev = m_scr[sq]
        g = jnp.maximum(m_prev, cm)
        mt = g + b
        wi = jnp.exp(m_prev - g)
        emt = jnp.exp(-mt)
        m_new = jnp.broadcast_to(mt[:, L - 1:L], (SUBLANES, L))
        b_last = jnp.broadcast_to(b[:, L - 1:L], (SUBLANES, L))
        ws = jnp.exp(e + b_last - m_new)
        wc = jnp.exp(b_last + m_prev - m_new)
        both = jnp.where(is_head_row, e, pltpu.roll(ws, N_HEADS_C, axis=0))
        return g, wi, emt, wc, m_new, _split3(jnp.transpose(jnp.concatenate([both, pad_rows], axis=0)), 1)

    def score_stage(sq, p):
        q_ref, k_ref, v_ref = refs[p]
        q2 = q_ref[sq]
        k2 = k_ref[sq] * (HEAD_DIM ** -0.5)
        q2b = q2.astype(BF16)
        k2b = k2.astype(BF16)
        vt_b = jnp.transpose(v_ref[sq]).astype(BF16)
        ct = ct_scr[sq, p]
        n_pair = n_scr[sq, pl.ds(p, 1), :]
        n2 = jnp.where(jnp.logical_or(jnp.logical_and(sub == 0, tl < HEAD_DIM),
                                      jnp.logical_and(sub == 1, tl >= HEAD_DIM)), n_pair, 0.0)
        qct = _dot_nt(ct.astype(BF16), q2b)
        qn = _dot_nt(n2.astype(BF16), q2b)
        st = [_dot_nt(k2b, jnp.where(lo_head if h == 0 else jnp.logical_not(lo_head), q2, 0.0)
                      .astype(BF16)) for h in range(2)]
        return k2, vt_b, ct, n_pair, qct, qn, st

    def weight_stage(p, gates, staged):
        g, wi, emt, _, _, cols = gates
        qn, st = staged[5], staged[6]
        out = []
        for h in range(2):
            gh = 2 * p + h
            e_c = cols[:, gh * LANES:(gh + 1) * LANES]
            at = jnp.exp(jnp.where(src_le_dst, e_c - g[gh:gh + 1, :], NEG)) * st[h]
            den = jnp.sum(at, axis=0, keepdims=True) + wi[gh:gh + 1, :] * qn[h:h + 1, :]
            inv = 1.0 / jnp.maximum(jnp.abs(den), emt[gh:gh + 1, :])
            out.append((at.astype(BF16), inv))
        return out

    def value_stage(sq, p, gates, staged, weights):
        _, wi, _, wc, m_new, cols = gates
        k2, vt_b, ct, n_pair, qct, _, _ = staged
        (at0, inv0), (at1, inv1) = weights
        g0, g1 = 2 * p, 2 * p + 1
        num_t = jnp.where(row_lo, _dot(vt_b, at0), _dot(vt_b, at1))
        wi2 = jnp.where(row_lo, wi[g0:g0 + 1, :], wi[g1:g1 + 1, :])
        inv2 = jnp.where(row_lo, inv0, inv1)
        hout = jnp.transpose((num_t + wi2 * qct) * inv2)
        ws2 = jnp.where(lo_head, cols[:, (N_HEADS_C + g0) * LANES:(N_HEADS_C + g0 + 1) * LANES],
                        cols[:, (N_HEADS_C + g1) * LANES:(N_HEADS_C + g1 + 1) * LANES])
        kw = k2 * ws2
        keep_t = jnp.where(row_lo, wc[g0:g0 + 1, :], wc[g1:g1 + 1, :])
        ct_scr[sq, p] = jnp.where(blockdiag, keep_t * ct + _dot(vt_b, kw.astype(BF16)), 0.0)
        n_scr[sq, pl.ds(p, 1), :] = jnp.where(lo_row, wc[g0:g0 + 1, :], wc[g1:g1 + 1, :]) * n_pair \
            + jnp.sum(kw, axis=0, keepdims=True)
        y_ref[sq, :, p * LANES:(p + 1) * LANES] = hout

    sums = [cumsum_stage(sq) for sq in range(n_seq)]
    staged = [score_stage(sq, p) for sq, p in units]
    gates = [gate_stage(sq, *sums[sq]) for sq in range(n_seq)]
    gates = [gt[:5] + (_dot(gt[5], sel_ref[...]),) for gt in gates]
    weights = [weight_stage(p, gates[sq], st) for (sq, p), st in zip(units, staged)]
    for (sq, p), st, wt in zip(units, staged, weights):
        value_stage(sq, p, gates[sq], st, wt)
    for sq in range(n_seq):
        m_scr[sq] = jnp.where(is_head_row, gates[sq][4], 0.0)

    @pl.when(ck == pl.num_programs(1) - 1)
    def _():
        for sq, p in units:
            c_pair = jnp.transpose(ct_scr[sq, p])
            c_out_ref[sq, 2 * p] = c_pair[0:HEAD_DIM, 0:HEAD_DIM]
            c_out_ref[sq, 2 * p + 1] = c_pair[HEAD_DIM:, HEAD_DIM:]
        n_out_ref[...] = n_scr[:, 0:2, :]
        m_out_ref[...] = m_scr[...]


def _mlstm_t(z3, col0, g_rows, n_valid, sw, layer, C0, n0, m0):
    b, t_len, _ = z3.shape
    L = MLSTM_CHUNK
    nc = t_len // L
    G = ML_GROUP
    tri_t, sel = _mlstm_consts()
    m_rows = jnp.broadcast_to(jnp.pad(m0, ((0, 0), (0, SUBLANES - N_HEADS_C)))[:, :, None],
                              (b, SUBLANES, LANES))

    def col(cb):
        return pl.BlockSpec((G, L, LANES), lambda i, c: (i, c, cb - col0))

    state_c = pl.BlockSpec((G, N_HEADS_C, HEAD_DIM, HEAD_DIM), lambda i, c: (i, 0, 0, 0))
    state_m = pl.BlockSpec((G, SUBLANES, LANES), lambda i, c: (i, 0, 0))
    y, c1, n1, m1 = pl.pallas_call(
        functools.partial(_mlstm_t_kernel, n_valid),
        grid=(b // G, nc),
        in_specs=[col(ZC_QC), col(ZC_QC + 1), col(ZC_KC), col(ZC_KC + 1), col(ZC_VC), col(ZC_VC + 1),
                  pl.BlockSpec((G, SUBLANES, L), lambda i, c: (i, 0, c)),
                  _layer_spec((SUBLANES, 1), layer),
                  _const_spec((L, L)), _const_spec((3 * LANES, N_GATES * LANES)),
                  state_c, pl.BlockSpec((G, N_HEADS_C, HEAD_DIM), lambda i, c: (i, 0, 0)), state_m],
        out_specs=[pl.BlockSpec((G, L, D_C), lambda i, c: (i, c, 0)), state_c,
                   pl.BlockSpec((G, 2, LANES), lambda i, c: (i, 0, 0)), state_m],
        out_shape=[jax.ShapeDtypeStruct((b, t_len, D_C), F32),
                   jax.ShapeDtypeStruct((b, N_HEADS_C, HEAD_DIM, HEAD_DIM), F32),
                   jax.ShapeDtypeStruct((b, 2, LANES), F32),
                   jax.ShapeDtypeStruct((b, SUBLANES, LANES), F32)],
        scratch_shapes=[pltpu.VMEM((G, 2, LANES, LANES), F32), pltpu.VMEM((G, SUBLANES, LANES), F32),
                        pltpu.VMEM((G, SUBLANES, LANES), F32)],
        compiler_params=_params(("parallel", "arbitrary")),
        name="mlstm",
    )(*([z3] * 6), g_rows, sw['b_mlstm_gate'], tri_t, sel, C0, n0, m_rows)
    return y, c1, n1.reshape(b, N_HEADS_C, HEAD_DIM), m1[:, 0:N_HEADS_C, 0]


def _block_diag(w):
    depth, nb, bs, _ = w.shape
    eye = jnp.eye(nb, dtype=w.dtype)
    return jnp.einsum('lncd,nm->lncmd', w, eye).reshape(depth, nb * bs, nb * bs)


def _stack_weights(p):
    vec = lambda v: v.reshape(v.shape[0], 1, -1)
    bf = lambda w: w.astype(BF16)
    out = {k: vec(p[k]) for k in (
        'g_f1_pre', 'g_f1_post', 'g_mix_pre', 'conv_b', 'b_rg_a', 'b_rg_x', 'lru_lambda', 'g_out_a',
        'g_out_b', 'g_out_c', 'g_mix_post', 'g_f2_pre', 'g_f2_post', 'g_ple_pre', 'g_ple_post')}
    out.update({k: bf(p[k]) for k in (
        'w_f1_gate', 'w_f1_up', 'w_f1_down', 'w_f2_gate', 'w_f2_up', 'w_f2_down', 'w_out',
        'w_ple_gate', 'w_ple_proj')})
    out['w_in'] = bf(jnp.pad(p['w_in'], ((0, 0), (0, 0), (0, Z_COLS - p['w_in'].shape[-1]))))
    out['conv_w'] = p['conv_w']
    out['w_rg_a'] = bf(_block_diag(p['w_rg_a']))
    out['w_rg_x'] = bf(_block_diag(p['w_rg_x']))
    out['b_mlstm_gate'] = jnp.concatenate([p['b_mlstm_i'], p['b_mlstm_f']], axis=1)[:, :, None]
    return out


def _gate_rows(z3, col0):
    lo = (ZC_GATE - col0) * LANES
    return jnp.swapaxes(z3[:, :, lo:lo + N_GATES], 1, 2)


def _layer(x, pe_all, sw, layer, cache, kv_bufs=None):
    b, t_len, d = x.shape
    if cache is None:
        rows, z, new_k, new_v, z_cls, g_rows = _premix(x.reshape(b * t_len, d), sw, layer,
                                                       kv_bufs, t_len)
        z3 = z.reshape(b, t_len, Z_COLS)
        ya = _attn_prompt(z3, z_cls)
        conv_buf = jnp.zeros((b, CONV_W - 1, D_B), F32)
        h0 = jnp.zeros((b, D_B), F32)
        C0 = jnp.zeros((b, N_HEADS_C, HEAD_DIM, HEAD_DIM), F32)
        n0 = jnp.zeros((b, N_HEADS_C, HEAD_DIM), F32)
        m0 = jnp.zeros((b, N_HEADS_C), F32)
        zc, col0, n_valid = z3, 0, MLSTM_CHUNK
    else:
        kt_all, vt_all, conv_buf, h0, C0, n0, m0 = cache
        rows, z = _premix(x.reshape(b * t_len, d), sw, layer)
        z3 = z.reshape(b, t_len, Z_COLS)
        new_k = z3[:, :, ZC_KA * LANES:ZC_KA * LANES + D_A].reshape(b, t_len, N_HEADS_A, HEAD_DIM)
        new_v = z3[:, :, ZC_VA * LANES:ZC_VA * LANES + D_A].reshape(b, t_len, N_HEADS_A, HEAD_DIM)
        ya = _attn_sample(z3, kt_all, vt_all, layer)
        col0 = ZC_QC
        zc = jnp.pad(z3[:, :, col0 * LANES:], ((0, 0), (0, MLSTM_CHUNK - t_len), (0, 0)))
        g_rows = _gate_rows(zc, col0)
        n_valid = t_len
    yb, new_buf, h_last = _rglru(z3, conv_buf, h0, sw, layer)
    yc, C1, n1, m1 = _mlstm_t(zc, col0, g_rows, n_valid, sw, layer, C0, n0, m0)
    yc = yc[:, :t_len]
    rows = _postmix(rows, ya.reshape(b * t_len, D_A), yb.reshape(b * t_len, D_B), z,
                    yc.reshape(b * t_len, D_C), pe_all, sw, layer)
    return rows.reshape(b, t_len, d), (new_k, new_v, new_buf, h_last, C1, n1, m1)


def _feature_major(cache):
    depth, b, w_buf, nh, dh = cache.shape
    return jnp.transpose(cache, (0, 1, 3, 4, 2)).reshape(depth, b, nh * dh, w_buf)


def kernel(x_prompt, x_sample, cache_k, cache_v, state_conv, state_h, state_C, state_n, state_m, p_prompt, p_sample, g_f1_pre, w_f1_gate, w_f1_up, w_f1_down, g_f1_post, g_mix_pre, w_in, conv_w, conv_b, w_rg_a, b_rg_a, w_rg_x, b_rg_x, lru_lambda, b_mlstm_i, b_mlstm_f, g_out_a, g_out_b, g_out_c, w_out, g_mix_post, g_f2_pre, w_f2_gate, w_f2_up, w_f2_down, g_f2_post, g_ple_pre, w_ple_gate, w_ple_proj, g_ple_post):
    depth = w_in.shape[0]
    sw = _stack_weights(dict(
        g_f1_pre=g_f1_pre, w_f1_gate=w_f1_gate, w_f1_up=w_f1_up, w_f1_down=w_f1_down,
        g_f1_post=g_f1_post, g_mix_pre=g_mix_pre, w_in=w_in, conv_w=conv_w, conv_b=conv_b,
        w_rg_a=w_rg_a, b_rg_a=b_rg_a, w_rg_x=w_rg_x, b_rg_x=b_rg_x, lru_lambda=lru_lambda,
        b_mlstm_i=b_mlstm_i, b_mlstm_f=b_mlstm_f, g_out_a=g_out_a, g_out_b=g_out_b,
        g_out_c=g_out_c, w_out=w_out, g_mix_post=g_mix_post, g_f2_pre=g_f2_pre,
        w_f2_gate=w_f2_gate, w_f2_up=w_f2_up, w_f2_down=w_f2_down, g_f2_post=g_f2_post,
        g_ple_pre=g_ple_pre, w_ple_gate=w_ple_gate, w_ple_proj=w_ple_proj, g_ple_post=g_ple_post))
    xp, xs = x_prompt, x_sample
    pe_prompt = p_prompt.reshape(depth, -1, p_prompt.shape[-1])
    pe_sample = p_sample.reshape(depth, -1, p_sample.shape[-1])
    kt_all = _feature_major(cache_k)
    vt_all = _feature_major(cache_v)
    b_p, s_p, _ = x_prompt.shape
    keep = min(DILATED_CONFIGS[-1][0], s_p)
    kv_bufs = (jnp.zeros((depth, b_p, D_A, keep), F32), jnp.zeros((depth, b_p, D_A, keep), F32))
    sp = [[] for _ in range(5)]
    ss = [[] for _ in range(7)]
    for i in range(depth):
        xp, st_p = _layer(xp, pe_prompt, sw, i, None, kv_bufs)
        kv_bufs = st_p[:2]
        cache_i = (kt_all, vt_all, state_conv[i], state_h[i], state_C[i], state_n[i], state_m[i])
        xs, st_s = _layer(xs, pe_sample, sw, i, cache_i)
        for j in range(5):
            sp[j].append(st_p[2 + j])
        for j in range(7):
            ss[j].append(st_s[j])
    k_prompt, v_prompt = [
        jnp.transpose(buf.reshape(depth, b_p, N_HEADS_A, HEAD_DIM, keep), (0, 1, 4, 2, 3))
        for buf in kv_bufs]
    conv_prompt, h_prompt, C_prompt, n_prompt, m_prompt = [jnp.stack(a) for a in sp]
    k_sample, v_sample, conv_sample, h_sample, C_sample, n_sample, m_sample = [jnp.stack(a) for a in ss]
    return (xp, xs, k_prompt, v_prompt, k_sample, v_sample, conv_prompt, conv_sample,
            h_prompt, h_sample, C_prompt, C_sample, n_prompt, n_sample, m_prompt, m_sample)
```
